```python
import math
import jax
import jax.numpy as jnp
from jax import lax
import numpy as np

D_MODEL = 1024
BATCH = 4
SEQ = 4096
DEPTH = 4

GRID_W = 64
CTX_LEN = 256

BRANCH_WIDTH = D_MODEL // 2
N_BRANCH = 3
HEAD_DIM = 64
NA_HEADS = BRANCH_WIDTH // HEAD_DIM
NA_KH_MAX = 8
NA_KW = 16
ROPE_THETA = 10000.0
NEG_INF = -1e30
POOL_WINDOWS = (2, 4, 8, 16)
POOL_GROUPS = len(POOL_WINDOWS)
POOL_GROUP_DIM = BRANCH_WIDTH // POOL_GROUPS
HY_ORDER = 2
HY_SHORT = 3
HY_EMB_DIM = 33
HY_FILTER_HIDDEN = 64
HY_MIN_DECAY = math.log(1e-2) / 0.3
HY_MAX_DECAY = math.log(1e-2) / 1.5
N_EXPERTS = 16
N_GROUPS = 4
EXPERTS_PER_GROUP = N_EXPERTS // N_GROUPS
TOP_K = 2
EXPERT_FF = D_MODEL // 2
N_MOD = 6
RMS_EPS = 1e-6
POOL_OFF = 3 * BRANCH_WIDTH
HY_OFF = POOL_OFF + BRANCH_WIDTH
GATE_OFF = HY_OFF + (HY_ORDER + 1) * BRANCH_WIDTH
IN_WIDTH = GATE_OFF + N_BRANCH * D_MODEL

kernel_name = 'hybrid_natten_pool_hyena_moe_dit'


def rms_norm(x, g):
    xf = x.astype(jnp.float32)
    y = xf * lax.rsqrt(jnp.mean(xf * xf, axis=-1, keepdims=True) + RMS_EPS)
    return (y * g.astype(jnp.float32)).astype(x.dtype)


def modulate(h, shift, scale):
    return h * (1.0 + scale) + shift


def split_heads(t):
    b, n, _ = t.shape
    return t.reshape(b, n, NA_HEADS, HEAD_DIM).transpose(0, 2, 1, 3)


def merge_heads(t):
    b, h, n, d = t.shape
    return t.transpose(0, 2, 1, 3).reshape(b, n, h * d)


def axial_rope_tables(n_tokens):
    t = jnp.arange(n_tokens)
    row = (t // GRID_W).astype(jnp.float32)
    col = (t % GRID_W).astype(jnp.float32)
    n_freq = HEAD_DIM // 4
    inv = ROPE_THETA ** (-jnp.arange(n_freq, dtype=jnp.float32) / n_freq)
    ang = jnp.concatenate([row[:, None] * inv, col[:, None] * inv], axis=-1)
    return jnp.cos(ang), jnp.sin(ang)


def apply_rope(x, cos, sin):
    xf = x.astype(jnp.float32)
    x1, x2 = jnp.split(xf, 2, axis=-1)
    return jnp.concatenate([x1 * cos - x2 * sin, x1 * sin + x2 * cos], axis=-1).astype(x.dtype)


def neighbourhood_attention(q_rot, q_plain, k_rot, v, k_ctx, v_ctx, rpb):
    b, h, s, dh = q_rot.shape
    rows = s // GRID_W
    kh = min(NA_KH_MAX, rows)
    r = jnp.arange(rows)
    row_start = jnp.clip(r - kh // 2, 0, rows - kh)
    key_rows = row_start[:, None] + jnp.arange(kh)[None, :]
    j = jnp.arange(GRID_W)
    col_start = jnp.clip(j - NA_KW // 2, 0, GRID_W - NA_KW)
    col_mask = (j[None, :] >= col_start[:, None]) & (j[None, :] < col_start[:, None] + NA_KW)
    dr = key_rows - r[:, None] + NA_KH_MAX - 1
    dc = jnp.clip(j[None, :] - j[:, None], -(NA_KW - 1), NA_KW - 1) + NA_KW - 1
    bias = rpb.astype(jnp.float32)[:, dr[:, None, :, None], dc[None, :, None, :]]
    bias = jnp.where(col_mask[None, None, :, None, :], bias, NEG_INF)
    q_g = q_rot.reshape(b, h, rows, GRID_W, dh)
    qp_g = q_plain.reshape(b, h, rows, GRID_W, dh)
    k_g = k_rot.reshape(b, h, rows, GRID_W, dh)[:, :, key_rows]
    v_g = v.reshape(b, h, rows, GRID_W, dh)[:, :, key_rows]
    scale = dh ** -0.5
    s_lat = jnp.einsum('bhrqd,bhrikd->bhrqik', q_g, k_g, preferred_element_type=jnp.float32) * scale + bias
    s_ctx = jnp.einsum('bhrqd,bhcd->bhrqc', qp_g, k_ctx, preferred_element_type=jnp.float32) * scale
    n_lat = kh * GRID_W
    scores = jnp.concatenate([s_lat.reshape(b, h, rows, GRID_W, n_lat), s_ctx], axis=-1)
    p = jax.nn.softmax(scores, axis=-1).astype(v.dtype)
    p_lat = p[..., :n_lat].reshape(b, h, rows, GRID_W, kh, GRID_W)
    out = jnp.einsum('bhrqik,bhrikd->bhrqd', p_lat, v_g) + jnp.einsum('bhrqc,bhcd->bhrqd', p[..., n_lat:], v_ctx)
    return out.reshape(b, h, s, dh)


def context_attention(q, k, v):
    s = jnp.einsum('bhqd,bhkd->bhqk', q, k, preferred_element_type=jnp.float32) * (HEAD_DIM ** -0.5)
    p = jax.nn.softmax(s, axis=-1).astype(v.dtype)
    return jnp.einsum('bhqk,bhkd->bhqd', p, v)


def multiscale_pool(u, pool_w, pool_scale):
    b, n, _ = u.shape
    uf = u.astype(jnp.float32)
    csum = jnp.concatenate([jnp.zeros((b, 1, BRANCH_WIDTH), jnp.float32), jnp.cumsum(uf, axis=1)], axis=1)
    t = jnp.arange(n)
    groups = []
    for g, w in enumerate(POOL_WINDOWS):
        lo = jnp.maximum(t - w // 2, 0)
        hi = jnp.minimum(t + w - w // 2, n)
        sl = slice(g * POOL_GROUP_DIM, (g + 1) * POOL_GROUP_DIM)
        cs = csum[..., sl]
        mean = (cs[:, hi] - cs[:, lo]) / (hi - lo).astype(jnp.float32)[None, :, None]
        groups.append(mean - uf[..., sl])
    y = jnp.stack(groups, axis=2)
    y = jnp.einsum('blgc,gcd->blgd', y, pool_w.astype(jnp.float32)).reshape(b, n, BRANCH_WIDTH)
    return (y * pool_scale.astype(jnp.float32)).astype(u.dtype)


def depthwise_conv3(u, w, bias):
    y = lax.conv_general_dilated(u, w[:, None, :].astype(u.dtype), window_strides=(1,),
                                 padding=((HY_SHORT // 2, HY_SHORT // 2),),
                                 dimension_numbers=('NWC', 'WIO', 'NWC'),
                                 feature_group_count=u.shape[-1])
    return y + bias.astype(u.dtype)


def hyena_filter_spectrum(n, w1, b1, freq, w2, b2, w3):
    f32 = jnp.float32
    t = jnp.linspace(0.0, 1.0, n, dtype=f32)[:, None]
    bands = (HY_EMB_DIM - 1) // 2
    ang = (2.0 * math.pi / n) * jnp.arange(n, dtype=f32)[:, None]
    f = jnp.linspace(1e-4, bands - 1, bands, dtype=f32)[None, :]
    z = jnp.concatenate([t, jnp.cos(f * ang), -jnp.sin(f * ang)], axis=-1)
    fr = freq.astype(f32)
    hh = jnp.sin(fr * (z @ w1.astype(f32) + b1.astype(f32)))
    hh = jnp.sin(fr * (hh @ w2.astype(f32) + b2.astype(f32)))
    hh = (hh @ w3.astype(f32)).reshape(n, HY_ORDER, 2, BRANCH_WIDTH)
    deltas = jnp.abs(jnp.linspace(HY_MIN_DECAY, HY_MAX_DECAY, BRANCH_WIDTH, dtype=f32))
    hh = hh * jnp.exp(-t * deltas)[:, None, None, :]
    fwd, bwd = hh[:, :, 0], hh[:, :, 1]
    buf = jnp.concatenate([fwd, jnp.zeros((1, HY_ORDER, BRANCH_WIDTH), f32), bwd[:0:-1]], axis=0)
    buf = buf / jnp.sum(jnp.abs(buf), axis=0, keepdims=True)
    return jnp.fft.rfft(buf, axis=0)


def bidirectional_long_conv(u, h_spec, skip):
    n = u.shape[1]
    y = jnp.fft.irfft(jnp.fft.rfft(u, n=2 * n, axis=1) * h_spec[None], n=2 * n, axis=1)[:, :n]
    return y + u * skip


def hyena_mixer(u, short_w, short_b, w1, b1, freq, w2, b2, w3, skip):
    n = u.shape[1]
    uc = depthwise_conv3(u, short_w, short_b).astype(jnp.float32)
    v, x1, x2 = jnp.split(uc, HY_ORDER + 1, axis=-1)
    h_spec = hyena_filter_spectrum(n, w1, b1, freq, w2, b2, w3)
    z = v
    for o, gate in enumerate((x1, x2)):
        z = gate * bidirectional_long_conv(z, h_spec[:, o], skip[o].astype(jnp.float32))
    return z.astype(u.dtype)


def merge_branches(p, y_attn, pool_w, pool_scale, hy_params, w_branch, w_out):
    b, n, _ = p.shape
    y_pool = multiscale_pool(p[..., POOL_OFF:HY_OFF], pool_w, pool_scale)
    y_hy = hyena_mixer(p[..., HY_OFF:GATE_OFF], *hy_params)
    gates = jax.nn.sigmoid(p[..., GATE_OFF:].astype(jnp.float32)).reshape(b, n, N_BRANCH, D_MODEL)
    ys = jnp.stack([y_attn.astype(p.dtype), y_pool, y_hy], axis=2)
    br = jnp.einsum('blnc,ncd->blnd', ys, w_branch, preferred_element_type=jnp.float32)
    merged = jnp.sum(gates * br, axis=2).astype(p.dtype)
    return merged @ w_out


def token_mixers(hl, hc, w_in, rpb, pool_w, pool_scale, hy_params, w_branch, w_out, cos, sin, need_ctx):
    pl = hl @ w_in
    pc = hc @ w_in
    ql, kl, vl = [split_heads(t) for t in jnp.split(pl[..., :POOL_OFF], 3, axis=-1)]
    qc, kc, vc = [split_heads(t) for t in jnp.split(pc[..., :POOL_OFF], 3, axis=-1)]
    ya_l = neighbourhood_attention(apply_rope(ql, cos, sin), ql, apply_rope(kl, cos, sin), vl, kc, vc, rpb)
    out_l = merge_branches(pl, merge_heads(ya_l), pool_w, pool_scale, hy_params, w_branch, w_out)
    out_c = None
    if need_ctx:
        ya_c = context_attention(qc, kc, vc)
        out_c = merge_branches(pc, merge_heads(ya_c), pool_w, pool_scale, hy_params, w_branch, w_out)
    return out_l, out_c


def grouped_moe(h, w_router, router_bias, w_gate, w_up, w_down):
    aff = jax.nn.sigmoid((h @ w_router).astype(jnp.float32))
    biased = (aff + router_bias.astype(jnp.float32)).reshape(-1, N_GROUPS, EXPERTS_PER_GROUP)
    group_score = jnp.sum(lax.top_k(biased, TOP_K)[0], axis=-1)
    group = jnp.argmax(group_score, axis=-1)
    in_group = jnp.take_along_axis(biased, group[:, None, None], axis=1)[:, 0]
    local = lax.top_k(in_group, TOP_K)[1]
    expert = group[:, None] * EXPERTS_PER_GROUP + local
    wsel = jnp.take_along_axis(aff, expert, axis=-1)
    wsel = wsel / jnp.sum(wsel, axis=-1, keepdims=True)
    gates = jnp.einsum('tk,tke->te', wsel, jax.nn.one_hot(expert, N_EXPERTS, dtype=jnp.float32))
    out = jnp.zeros(h.shape, jnp.float32)
    for e in range(N_EXPERTS):
        hid = jax.nn.silu(h @ w_gate[e]) * (h @ w_up[e])
        out = out + gates[:, e:e + 1] * (hid @ w_down[e]).astype(jnp.float32)
    return out.astype(h.dtype)


def setup_inputs(seed: int = 0) -> dict:
    key = jax.random.key(seed)
    ks = iter(jax.random.split(key, 32))
    D = D_MODEL

    def nrm(shape, scale):
        return jax.random.normal(next(ks), shape, jnp.float32) * scale

    return {
        'x': nrm((BATCH, SEQ, D), 1.0),
        'c': nrm((BATCH, D), 1.0),
        'ctx': nrm((BATCH, CTX_LEN, D), 1.0),
        'c_ctx': nrm((D,), 1.0),
        'norm1_g': 1.0 + nrm((DEPTH, D), 0.02),
        'norm2_g': 1.0 + nrm((DEPTH, D), 0.02),
        'w_mod': nrm((DEPTH, D, N_MOD * D), 0.5 * D ** -0.5),
        'b_mod': nrm((DEPTH, N_MOD * D), 0.02),
        'w_in': nrm((DEPTH, D, IN_WIDTH), D ** -0.5),
        'rpb': nrm((DEPTH, NA_HEADS, 2 * NA_KH_MAX - 1, 2 * NA_KW - 1), 0.1),
        'pool_w': nrm((DEPTH, POOL_GROUPS, POOL_GROUP_DIM, POOL_GROUP_DIM), POOL_GROUP_DIM ** -0.5),
        'pool_scale': 1.0 + nrm((DEPTH, BRANCH_WIDTH), 0.02),
        'hy_short_w': nrm((DEPTH, HY_SHORT, (HY_ORDER + 1) * BRANCH_WIDTH), HY_SHORT ** -0.5),
        'hy_short_b': nrm((DEPTH, (HY_ORDER + 1) * BRANCH_WIDTH), 0.02),
        'hy_w1': nrm((DEPTH, HY_EMB_DIM, HY_FILTER_HIDDEN), HY_EMB_DIM ** -0.5),
        'hy_b1': nrm((DEPTH, HY_FILTER_HIDDEN), 0.02),
        'hy_freq': 1.0 + nrm((DEPTH, HY_FILTER_HIDDEN), 0.02),
        'hy_w2': nrm((DEPTH, HY_FILTER_HIDDEN, HY_FILTER_HIDDEN), HY_FILTER_HIDDEN ** -0.5),
        'hy_b2': nrm((DEPTH, HY_FILTER_HIDDEN), 0.02),
        'hy_w3': nrm((DEPTH, HY_FILTER_HIDDEN, HY_ORDER * 2 * BRANCH_WIDTH), HY_FILTER_HIDDEN ** -0.5),
        'hy_skip': nrm((DEPTH, HY_ORDER, BRANCH_WIDTH), 0.5),
        'w_branch': nrm((DEPTH, N_BRANCH, BRANCH_WIDTH, D), BRANCH_WIDTH ** -0.5),
        'w_out': nrm((DEPTH, D, D), D ** -0.5),
        'w_router': nrm((D, N_EXPERTS), D ** -0.5),
        'router_bias': nrm((N_EXPERTS,), 0.01),
        'w_gate_e': nrm((DEPTH, N_EXPERTS, D, EXPERT_FF), D ** -0.5),
        'w_up_e': nrm((DEPTH, N_EXPERTS, D, EXPERT_FF), D ** -0.5),
        'w_down_e': nrm((DEPTH, N_EXPERTS, EXPERT_FF, D), EXPERT_FF ** -0.5),
        'final_g': 1.0 + nrm((D,), 0.02),
    }


def reference(x, c, ctx, c_ctx, norm1_g, norm2_g, w_mod, b_mod, w_in, rpb, pool_w, pool_scale,
              hy_short_w, hy_short_b, hy_w1, hy_b1, hy_freq, hy_w2, hy_b2, hy_w3, hy_skip,
              w_branch, w_out, w_router, router_bias, w_gate_e, w_up_e, w_down_e, final_g):
    b, s, d = x.shape
    n_ctx = ctx.shape[1]
    cos, sin = axial_rope_tables(s)
    xl, xc = x, ctx
    for l in range(DEPTH):
        need_ctx = l < DEPTH - 1
        ml = (jax.nn.silu(c) @ w_mod[l] + b_mod[l]).reshape(b, 1, N_MOD, d)
        mc = (jax.nn.silu(c_ctx) @ w_mod[l] + b_mod[l]).reshape(1, 1, N_MOD, d)
        hl = modulate(rms_norm(xl, norm1_g[l]), ml[:, :, 0], ml[:, :, 1])
        hc = modulate(rms_norm(xc, norm1_g[l]), mc[:, :, 0], mc[:, :, 1])
        hy_params = (hy_short_w[l], hy_short_b[l], hy_w1[l], hy_b1[l], hy_freq[l],
                     hy_w2[l], hy_b2[l], hy_w3[l], hy_skip[l])
        out_l, out_c = token_mixers(hl, hc, w_in[l], rpb[l], pool_w[l], pool_scale[l], hy_params,
                                    w_branch[l], w_out[l], cos, sin, need_ctx)
        xl = xl + ml[:, :, 2] * out_l
        hl2 = modulate(rms_norm(xl, norm2_g[l]), ml[:, :, 3], ml[:, :, 4])
        if need_ctx:
            xc = xc + mc[:, :, 2] * out_c
            hc2 = modulate(rms_norm(xc, norm2_g[l]), mc[:, :, 3], mc[:, :, 4])
            tokens = jnp.concatenate([hl2.reshape(b * s, d), hc2.reshape(b * n_ctx, d)], axis=0)
            f = grouped_moe(tokens, w_router, router_bias, w_gate_e[l], w_up_e[l], w_down_e[l])
            xl = xl + ml[:, :, 5] * f[:b * s].reshape(b, s, d)
            xc = xc + mc[:, :, 5] * f[b * s:].reshape(b, n_ctx, d)
        else:
            f = grouped_moe(hl2.reshape(b * s, d), w_router, router_bias, w_gate_e[l], w_up_e[l], w_down_e[l])
            xl = xl + ml[:, :, 5] * f.reshape(b, s, d)
    return rms_norm(xl, final_g)
```

```python
import functools
import math

import jax
import jax.numpy as jnp
from jax import lax
from jax.experimental import pallas as pl
from jax.experimental.pallas import tpu as pltpu

D_MODEL = 1024
DEPTH = 4
GRID_W = 64
BRANCH_WIDTH = D_MODEL // 2
N_BRANCH = 3
HEAD_DIM = 64
NA_HEADS = BRANCH_WIDTH // HEAD_DIM
NA_KH_MAX = 8
NA_KW = 16
ROPE_THETA = 10000.0
NEG_INF = -1e30
POOL_WINDOWS = (2, 4, 8, 16)
POOL_GROUPS = len(POOL_WINDOWS)
POOL_GROUP_DIM = BRANCH_WIDTH // POOL_GROUPS
HY_ORDER = 2
HY_SHORT = 3
HY_EMB_DIM = 33
HY_MIN_DECAY = math.log(1e-2) / 0.3
HY_MAX_DECAY = math.log(1e-2) / 1.5
N_EXPERTS = 16
N_GROUPS = 4
EXPERTS_PER_GROUP = N_EXPERTS // N_GROUPS
TOP_K = 2
EXPERT_FF = D_MODEL // 2
N_MOD = 6
RMS_EPS = 1e-6
POOL_OFF = 3 * BRANCH_WIDTH
HY_OFF = POOL_OFF + BRANCH_WIDTH
GATE_OFF = HY_OFF + (HY_ORDER + 1) * BRANCH_WIDTH
IN_WIDTH = GATE_OFF + N_BRANCH * D_MODEL

VMEM_LIMIT_BYTES = 48 * 1024 * 1024
MOE_TILE = 256


def _mm_kernel(a_ref, b_ref, o_ref):
    o_ref[...] = jnp.dot(a_ref[...].astype(jnp.bfloat16), b_ref[...].astype(jnp.bfloat16),
                         preferred_element_type=jnp.float32).astype(o_ref.dtype)


def _pick_tile(n, cands):
    for c in cands:
        if n % c == 0:
            return c
    return n


def matmul(a, b, out_dtype=jnp.float32):
    m, k = a.shape
    _, n = b.shape
    tm = _pick_tile(m, (1024, 512, 256, 128, 64, 32, 16, 8))
    tn = _pick_tile(n, (1664, 1024, 512, 256, 128))
    return pl.pallas_call(
        _mm_kernel,
        grid=(m // tm, n // tn),
        in_specs=[pl.BlockSpec((tm, k), lambda i, j: (i, 0)),
                  pl.BlockSpec((k, tn), lambda i, j: (0, j))],
        out_specs=pl.BlockSpec((tm, tn), lambda i, j: (i, j)),
        out_shape=jax.ShapeDtypeStruct((m, n), out_dtype),
        compiler_params=pltpu.CompilerParams(
            dimension_semantics=("parallel", "parallel"), vmem_limit_bytes=VMEM_LIMIT_BYTES),
        name="dense_matmul",
    )(a, b)


def _expert_kernel(tile_expert_ref, tile_valid_ref, x_ref, wg_ref, wu_ref, wd_ref, o_ref):
    i = pl.program_id(0)

    @pl.when(tile_valid_ref[i] > 0)
    def _():
        x = x_ref[...].astype(jnp.bfloat16)
        g = jnp.dot(x, wg_ref[...].astype(jnp.bfloat16), preferred_element_type=jnp.float32)
        u = jnp.dot(x, wu_ref[...].astype(jnp.bfloat16), preferred_element_type=jnp.float32)
        hid = (g * jax.nn.sigmoid(g)) * u
        o_ref[...] = jnp.dot(hid.astype(jnp.bfloat16), wd_ref[...].astype(jnp.bfloat16),
                             preferred_element_type=jnp.float32)

    @pl.when(tile_valid_ref[i] == 0)
    def _():
        o_ref[...] = jnp.zeros_like(o_ref)


def expert_ffn(x_sorted, tile_expert, tile_valid, w_gate, w_up, w_down):
    p, d = x_sorted.shape
    n_tiles = p // MOE_TILE
    grid_spec = pltpu.PrefetchScalarGridSpec(
        num_scalar_prefetch=2,
        grid=(n_tiles,),
        in_specs=[
            pl.BlockSpec((MOE_TILE, d), lambda i, te, tv: (i, 0)),
            pl.BlockSpec((None, d, EXPERT_FF), lambda i, te, tv: (te[i], 0, 0)),
            pl.BlockSpec((None, d, EXPERT_FF), lambda i, te, tv: (te[i], 0, 0)),
            pl.BlockSpec((None, EXPERT_FF, d), lambda i, te, tv: (te[i], 0, 0)),
        ],
        out_specs=pl.BlockSpec((MOE_TILE, d), lambda i, te, tv: (i, 0)),
    )
    return pl.pallas_call(
        _expert_kernel,
        grid_spec=grid_spec,
        out_shape=jax.ShapeDtypeStruct((p, d), jnp.float32),
        compiler_params=pltpu.CompilerParams(
            dimension_semantics=("arbitrary",), vmem_limit_bytes=VMEM_LIMIT_BYTES),
        name="expert_ffn",
    )(tile_expert, tile_valid, x_sorted, w_gate, w_up, w_down)


def grouped_moe(h, w_router, router_bias, w_gate, w_up, w_down):
    t, d = h.shape
    aff = jax.nn.sigmoid(jnp.dot(h, w_router, precision=lax.Precision.HIGHEST).astype(jnp.float32))
    biased = (aff + router_bias.astype(jnp.float32)).reshape(-1, N_GROUPS, EXPERTS_PER_GROUP)
    group_score = jnp.sum(lax.top_k(biased, TOP_K)[0], axis=-1)
    group = jnp.argmax(group_score, axis=-1)
    in_group = jnp.take_along_axis(biased, group[:, None, None], axis=1)[:, 0]
    local = lax.top_k(in_group, TOP_K)[1]
    expert = (group[:, None] * EXPERTS_PER_GROUP + local).astype(jnp.int32)
    wsel = jnp.take_along_axis(aff, expert, axis=-1)
    wsel = wsel / jnp.sum(wsel, axis=-1, keepdims=True)

    flat_e = expert.reshape(-1)
    onehot = (flat_e[:, None] == jnp.arange(N_EXPERTS, dtype=jnp.int32)[None, :]).astype(jnp.int32)
    csum = jnp.cumsum(onehot, axis=0)
    rank = jnp.take_along_axis(csum, flat_e[:, None], axis=1)[:, 0] - 1
    counts = csum[-1]
    tiles_per = (counts + MOE_TILE - 1) // MOE_TILE
    tile_end = jnp.cumsum(tiles_per)
    tile_start = tile_end - tiles_per
    pos = tile_start[flat_e] * MOE_TILE + rank
    n_tiles = (t * TOP_K) // MOE_TILE + N_EXPERTS
    p = n_tiles * MOE_TILE
    token_of = jnp.zeros((p,), jnp.int32).at[pos].set(jnp.arange(t * TOP_K, dtype=jnp.int32) // TOP_K)
    tile_ids = jnp.arange(n_tiles, dtype=jnp.int32)
    tile_valid = (tile_ids < tile_end[-1]).astype(jnp.int32)
    tile_expert = jnp.minimum(jnp.searchsorted(tile_end, tile_ids, side='right'), N_EXPERTS - 1).astype(jnp.int32)
    last_expert = tile_expert[jnp.maximum(tile_end[-1] - 1, 0)]
    tile_expert = jnp.where(tile_valid > 0, tile_expert, last_expert)

    x_sorted = h.astype(jnp.bfloat16)[token_of]
    y_sorted = expert_ffn(x_sorted, tile_expert, tile_valid, w_gate, w_up, w_down)
    y = y_sorted[pos].reshape(t, TOP_K, d)
    return jnp.sum(y * wsel[:, :, None], axis=1)


def rms_norm(x, g):
    xf = x.astype(jnp.float32)
    y = xf * lax.rsqrt(jnp.mean(xf * xf, axis=-1, keepdims=True) + RMS_EPS)
    return (y * g.astype(jnp.float32)).astype(x.dtype)


def modulate(h, shift, scale):
    return h * (1.0 + scale) + shift


def split_heads(t):
    b, n, _ = t.shape
    return t.reshape(b, n, NA_HEADS, HEAD_DIM).transpose(0, 2, 1, 3)


def merge_heads(t):
    b, h, n, d = t.shape
    return t.transpose(0, 2, 1, 3).reshape(b, n, h * d)


def axial_rope_tables(n_tokens):
    t = jnp.arange(n_tokens)
    row = (t // GRID_W).astype(jnp.float32)
    col = (t % GRID_W).astype(jnp.float32)
    n_freq = HEAD_DIM // 4
    inv = ROPE_THETA ** (-jnp.arange(n_freq, dtype=jnp.float32) / n_freq)
    ang = jnp.concatenate([row[:, None] * inv, col[:, None] * inv], axis=-1)
    return jnp.cos(ang), jnp.sin(ang)


def apply_rope(x, cos, sin):
    xf = x.astype(jnp.float32)
    x1, x2 = jnp.split(xf, 2, axis=-1)
    return jnp.concatenate([x1 * cos - x2 * sin, x1 * sin + x2 * cos], axis=-1).astype(x.dtype)


def neighbourhood_attention(q_rot, q_plain, k_rot, v, k_ctx, v_ctx, rpb):
    b, h, s, dh = q_rot.shape
    rows = s // GRID_W
    kh = min(NA_KH_MAX, rows)
    r = jnp.arange(rows)
    row_start = jnp.clip(r - kh // 2, 0, rows - kh)
    key_rows = row_start[:, None] + jnp.arange(kh)[None, :]
    j = jnp.arange(GRID_W)
    col_start = jnp.clip(j - NA_KW // 2, 0, GRID_W - NA_KW)
    col_mask = (j[None, :] >= col_start[:, None]) & (j[None, :] < col_start[:, None] + NA_KW)
    dr = key_rows - r[:, None] + NA_KH_MAX - 1
    dc = jnp.clip(j[None, :] - j[:, None], -(NA_KW - 1), NA_KW - 1) + NA_KW - 1
    bias = rpb.astype(jnp.float32)[:, dr[:, None, :, None], dc[None, :, None, :]]
    bias = jnp.where(col_mask[None, None, :, None, :], bias, NEG_INF)
    q_g = q_rot.reshape(b, h, rows, GRID_W, dh)
    qp_g = q_plain.reshape(b, h, rows, GRID_W, dh)
    k_g = k_rot.reshape(b, h, rows, GRID_W, dh)[:, :, key_rows]
    v_g = v.reshape(b, h, rows, GRID_W, dh)[:, :, key_rows]
    scale = dh ** -0.5
    s_lat = jnp.einsum('bhrqd,bhrikd->bhrqik', q_g, k_g, preferred_element_type=jnp.float32) * scale + bias
    s_ctx = jnp.einsum('bhrqd,bhcd->bhrqc', qp_g, k_ctx, preferred_element_type=jnp.float32) * scale
    n_lat = kh * GRID_W
    scores = jnp.concatenate([s_lat.reshape(b, h, rows, GRID_W, n_lat), s_ctx], axis=-1)
    p = jax.nn.softmax(scores, axis=-1).astype(v.dtype)
    p_lat = p[..., :n_lat].reshape(b, h, rows, GRID_W, kh, GRID_W)
    out = jnp.einsum('bhrqik,bhrikd->bhrqd', p_lat, v_g) + jnp.einsum('bhrqc,bhcd->bhrqd', p[..., n_lat:], v_ctx)
    return out.reshape(b, h, s, dh)


def context_attention(q, k, v):
    s = jnp.einsum('bhqd,bhkd->bhqk', q, k, preferred_element_type=jnp.float32) * (HEAD_DIM ** -0.5)
    p = jax.nn.softmax(s, axis=-1).astype(v.dtype)
    return jnp.einsum('bhqk,bhkd->bhqd', p, v)


def multiscale_pool(u, pool_w, pool_scale):
    b, n, _ = u.shape
    uf = u.astype(jnp.float32)
    csum = jnp.concatenate([jnp.zeros((b, 1, BRANCH_WIDTH), jnp.float32), jnp.cumsum(uf, axis=1)], axis=1)
    t = jnp.arange(n)
    groups = []
    for g, w in enumerate(POOL_WINDOWS):
        lo = jnp.maximum(t - w // 2, 0)
        hi = jnp.minimum(t + w - w // 2, n)
        sl = slice(g * POOL_GROUP_DIM, (g + 1) * POOL_GROUP_DIM)
        cs = csum[..., sl]
        mean = (cs[:, hi] - cs[:, lo]) / (hi - lo).astype(jnp.float32)[None, :, None]
        groups.append(mean - uf[..., sl])
    y = jnp.stack(groups, axis=2)
    y = jnp.einsum('blgc,gcd->blgd', y, pool_w.astype(jnp.float32)).reshape(b, n, BRANCH_WIDTH)
    return (y * pool_scale.astype(jnp.float32)).astype(u.dtype)


def depthwise_conv3(u, w, bias):
    y = lax.conv_general_dilated(u, w[:, None, :].astype(u.dtype), window_strides=(1,),
                                 padding=((HY_SHORT // 2, HY_SHORT // 2),),
                                 dimension_numbers=('NWC', 'WIO', 'NWC'),
                                 feature_group_count=u.shape[-1])
    return y + bias.astype(u.dtype)


def hyena_filter_spectrum(n, w1, b1, freq, w2, b2, w3):
    f32 = jnp.float32
    t = jnp.linspace(0.0, 1.0, n, dtype=f32)[:, None]
    bands = (HY_EMB_DIM - 1) // 2
    ang = (2.0 * math.pi / n) * jnp.arange(n, dtype=f32)[:, None]
    f = jnp.linspace(1e-4, bands - 1, bands, dtype=f32)[None, :]
    z = jnp.concatenate([t, jnp.cos(f * ang), -jnp.sin(f * ang)], axis=-1)
    fr = freq.astype(f32)
    hh = jnp.sin(fr * (z @ w1.astype(f32) + b1.astype(f32)))
    hh = jnp.sin(fr * (hh @ w2.astype(f32) + b2.astype(f32)))
    hh = (hh @ w3.astype(f32)).reshape(n, HY_ORDER, 2, BRANCH_WIDTH)
    deltas = jnp.abs(jnp.linspace(HY_MIN_DECAY, HY_MAX_DECAY, BRANCH_WIDTH, dtype=f32))
    hh = hh * jnp.exp(-t * deltas)[:, None, None, :]
    fwd, bwd = hh[:, :, 0], hh[:, :, 1]
    buf = jnp.concatenate([fwd, jnp.zeros((1, HY_ORDER, BRANCH_WIDTH), f32), bwd[:0:-1]], axis=0)
    buf = buf / jnp.sum(jnp.abs(buf), axis=0, keepdims=True)
    return jnp.fft.rfft(buf, axis=0)


def bidirectional_long_conv(u, h_spec, skip):
    n = u.shape[1]
    y = jnp.fft.irfft(jnp.fft.rfft(u, n=2 * n, axis=1) * h_spec[None], n=2 * n, axis=1)[:, :n]
    return y + u * skip


def hyena_mixer(u, short_w, short_b, w1, b1, freq, w2, b2, w3, skip):
    n = u.shape[1]
    uc = depthwise_conv3(u, short_w, short_b).astype(jnp.float32)
    v, x1, x2 = jnp.split(uc, HY_ORDER + 1, axis=-1)
    h_spec = hyena_filter_spectrum(n, w1, b1, freq, w2, b2, w3)
    z = v
    for o, gate in enumerate((x1, x2)):
        z = gate * bidirectional_long_conv(z, h_spec[:, o], skip[o].astype(jnp.float32))
    return z.astype(u.dtype)


def merge_branches(p, y_attn, pool_w, pool_scale, hy_params, w_branch, w_out):
    b, n, _ = p.shape
    y_pool = multiscale_pool(p[..., POOL_OFF:HY_OFF], pool_w, pool_scale)
    y_hy = hyena_mixer(p[..., HY_OFF:GATE_OFF], *hy_params)
    gates = jax.nn.sigmoid(p[..., GATE_OFF:].astype(jnp.float32)).reshape(b, n, N_BRANCH, D_MODEL)
    ys = (y_attn.astype(p.dtype), y_pool, y_hy)
    merged = 0.0
    for i in range(N_BRANCH):
        br = matmul(ys[i].reshape(b * n, BRANCH_WIDTH), w_branch[i]).reshape(b, n, D_MODEL)
        merged = merged + gates[:, :, i] * br
    return matmul(merged.reshape(b * n, D_MODEL), w_out).reshape(b, n, D_MODEL)


def token_mixers(hl, hc, w_in, rpb, pool_w, pool_scale, hy_params, w_branch, w_out, cos, sin, need_ctx):
    b, s, d = hl.shape
    n_ctx = hc.shape[1]
    pl_ = matmul(hl.reshape(b * s, d), w_in).reshape(b, s, IN_WIDTH)
    pc = matmul(hc.reshape(b * n_ctx, d), w_in).reshape(b, n_ctx, IN_WIDTH)
    ql, kl, vl = [split_heads(t) for t in jnp.split(pl_[..., :POOL_OFF], 3, axis=-1)]
    qc, kc, vc = [split_heads(t) for t in jnp.split(pc[..., :POOL_OFF], 3, axis=-1)]
    ya_l = neighbourhood_attention(apply_rope(ql, cos, sin), ql, apply_rope(kl, cos, sin), vl, kc, vc, rpb)
    out_l = merge_branches(pl_, merge_heads(ya_l), pool_w, pool_scale, hy_params, w_branch, w_out)
    out_c = None
    if need_ctx:
        ya_c = context_attention(qc, kc, vc)
        out_c = merge_branches(pc, merge_heads(ya_c), pool_w, pool_scale, hy_params, w_branch, w_out)
    return out_l, out_c


def kernel(x, c, ctx, c_ctx, norm1_g, norm2_g, w_mod, b_mod, w_in, rpb, pool_w, pool_scale, hy_short_w, hy_short_b, hy_w1, hy_b1, hy_freq, hy_w2, hy_b2, hy_w3, hy_skip, w_branch, w_out, w_router, router_bias, w_gate_e, w_up_e, w_down_e, final_g):
    b, s, d = x.shape
    n_ctx = ctx.shape[1]
    cos, sin = axial_rope_tables(s)
    xl, xc = x, ctx
    for l in range(DEPTH):
        need_ctx = l < DEPTH - 1
        ml = (jax.nn.silu(c) @ w_mod[l] + b_mod[l]).reshape(b, 1, N_MOD, d)
        mc = (jax.nn.silu(c_ctx) @ w_mod[l] + b_mod[l]).reshape(1, 1, N_MOD, d)
        hl = modulate(rms_norm(xl, norm1_g[l]), ml[:, :, 0], ml[:, :, 1])
        hc = modulate(rms_norm(xc, norm1_g[l]), mc[:, :, 0], mc[:, :, 1])
        hy_params = (hy_short_w[l], hy_short_b[l], hy_w1[l], hy_b1[l], hy_freq[l],
                     hy_w2[l], hy_b2[l], hy_w3[l], hy_skip[l])
        out_l, out_c = token_mixers(hl, hc, w_in[l], rpb[l], pool_w[l], pool_scale[l], hy_params,
                                    w_branch[l], w_out[l], cos, sin, need_ctx)
        xl = xl + ml[:, :, 2] * out_l
        hl2 = modulate(rms_norm(xl, norm2_g[l]), ml[:, :, 3], ml[:, :, 4])
        if need_ctx:
            xc = xc + mc[:, :, 2] * out_c
            hc2 = modulate(rms_norm(xc, norm2_g[l]), mc[:, :, 3], mc[:, :, 4])
            tokens = jnp.concatenate([hl2.reshape(b * s, d), hc2.reshape(b * n_ctx, d)], axis=0)
            f = grouped_moe(tokens, w_router, router_bias, w_gate_e[l], w_up_e[l], w_down_e[l])
            xl = xl + ml[:, :, 5] * f[:b * s].reshape(b, s, d)
            xc = xc + mc[:, :, 5] * f[b * s:].reshape(b, n_ctx, d)
        else:
            f = grouped_moe(hl2.reshape(b * s, d), w_router, router_bias, w_gate_e[l], w_up_e[l], w_down_e[l])
            xl = xl + ml[:, :, 5] * f.reshape(b, s, d)
    return rms_norm(xl, final_g)
```

```python
import functools
import math

import jax
import jax.numpy as jnp
from jax import lax
from jax.experimental import pallas as pl
from jax.experimental.pallas import tpu as pltpu

D_MODEL = 1024
DEPTH = 4
GRID_W = 64
BRANCH_WIDTH = D_MODEL // 2
N_BRANCH = 3
HEAD_DIM = 64
NA_HEADS = BRANCH_WIDTH // HEAD_DIM
NA_KH_MAX = 8
NA_KW = 16
ROPE_THETA = 10000.0
NEG_INF = -1e30
POOL_WINDOWS = (2, 4, 8, 16)
POOL_GROUPS = len(POOL_WINDOWS)
POOL_GROUP_DIM = BRANCH_WIDTH // POOL_GROUPS
HY_ORDER = 2
HY_SHORT = 3
HY_EMB_DIM = 33
HY_MIN_DECAY = math.log(1e-2) / 0.3
HY_MAX_DECAY = math.log(1e-2) / 1.5
N_EXPERTS = 16
N_GROUPS = 4
EXPERTS_PER_GROUP = N_EXPERTS // N_GROUPS
TOP_K = 2
EXPERT_FF = D_MODEL // 2
N_MOD = 6
RMS_EPS = 1e-6
POOL_OFF = 3 * BRANCH_WIDTH
HY_OFF = POOL_OFF + BRANCH_WIDTH
GATE_OFF = HY_OFF + (HY_ORDER + 1) * BRANCH_WIDTH
IN_WIDTH = GATE_OFF + N_BRANCH * D_MODEL

VMEM_LIMIT_BYTES = 48 * 1024 * 1024
MOE_TILE = 256


def _mm_kernel(a_ref, b_ref, o_ref):
    o_ref[...] = jnp.dot(a_ref[...].astype(jnp.bfloat16), b_ref[...].astype(jnp.bfloat16),
                         preferred_element_type=jnp.float32).astype(o_ref.dtype)


def _pick_tile(n, cands):
    for c in cands:
        if n % c == 0:
            return c
    return n


def matmul(a, b, out_dtype=jnp.float32):
    m, k = a.shape
    _, n = b.shape
    tm = _pick_tile(m, (1024, 512, 256, 128, 64, 32, 16, 8))
    tn = _pick_tile(n, (1664, 1024, 512, 256, 128))
    return pl.pallas_call(
        _mm_kernel,
        grid=(m // tm, n // tn),
        in_specs=[pl.BlockSpec((tm, k), lambda i, j: (i, 0)),
                  pl.BlockSpec((k, tn), lambda i, j: (0, j))],
        out_specs=pl.BlockSpec((tm, tn), lambda i, j: (i, j)),
        out_shape=jax.ShapeDtypeStruct((m, n), out_dtype),
        compiler_params=pltpu.CompilerParams(
            dimension_semantics=("parallel", "parallel"), vmem_limit_bytes=VMEM_LIMIT_BYTES),
        name="dense_matmul",
    )(a, b)


def _expert_kernel(tile_expert_ref, tile_valid_ref, x_ref, wg_ref, wu_ref, wd_ref, o_ref):
    i = pl.program_id(0)

    @pl.when(tile_valid_ref[i] > 0)
    def _():
        x = x_ref[...].astype(jnp.bfloat16)
        g = jnp.dot(x, wg_ref[...].astype(jnp.bfloat16), preferred_element_type=jnp.float32)
        u = jnp.dot(x, wu_ref[...].astype(jnp.bfloat16), preferred_element_type=jnp.float32)
        hid = (g * jax.nn.sigmoid(g)) * u
        o_ref[...] = jnp.dot(hid.astype(jnp.bfloat16), wd_ref[...].astype(jnp.bfloat16),
                             preferred_element_type=jnp.float32)

    @pl.when(tile_valid_ref[i] == 0)
    def _():
        o_ref[...] = jnp.zeros_like(o_ref)


def expert_ffn(x_sorted, tile_expert, tile_valid, w_gate, w_up, w_down):
    p, d = x_sorted.shape
    n_tiles = p // MOE_TILE
    grid_spec = pltpu.PrefetchScalarGridSpec(
        num_scalar_prefetch=2,
        grid=(n_tiles,),
        in_specs=[
            pl.BlockSpec((MOE_TILE, d), lambda i, te, tv: (i, 0)),
            pl.BlockSpec((None, d, EXPERT_FF), lambda i, te, tv: (te[i], 0, 0)),
            pl.BlockSpec((None, d, EXPERT_FF), lambda i, te, tv: (te[i], 0, 0)),
            pl.BlockSpec((None, EXPERT_FF, d), lambda i, te, tv: (te[i], 0, 0)),
        ],
        out_specs=pl.BlockSpec((MOE_TILE, d), lambda i, te, tv: (i, 0)),
    )
    return pl.pallas_call(
        _expert_kernel,
        grid_spec=grid_spec,
        out_shape=jax.ShapeDtypeStruct((p, d), jnp.float32),
        compiler_params=pltpu.CompilerParams(
            dimension_semantics=("arbitrary",), vmem_limit_bytes=VMEM_LIMIT_BYTES),
        name="expert_ffn",
    )(tile_expert, tile_valid, x_sorted, w_gate, w_up, w_down)


def grouped_moe(h, w_router, router_bias, w_gate, w_up, w_down):
    t, d = h.shape
    aff = jax.nn.sigmoid(jnp.dot(h, w_router, precision=lax.Precision.HIGHEST).astype(jnp.float32))
    biased = (aff + router_bias.astype(jnp.float32)).reshape(-1, N_GROUPS, EXPERTS_PER_GROUP)
    group_score = jnp.sum(lax.top_k(biased, TOP_K)[0], axis=-1)
    group = jnp.argmax(group_score, axis=-1)
    in_group = jnp.take_along_axis(biased, group[:, None, None], axis=1)[:, 0]
    local = lax.top_k(in_group, TOP_K)[1]
    expert = (group[:, None] * EXPERTS_PER_GROUP + local).astype(jnp.int32)
    wsel = jnp.take_along_axis(aff, expert, axis=-1)
    wsel = wsel / jnp.sum(wsel, axis=-1, keepdims=True)

    flat_e = expert.reshape(-1)
    onehot = (flat_e[:, None] == jnp.arange(N_EXPERTS, dtype=jnp.int32)[None, :]).astype(jnp.int32)
    csum = jnp.cumsum(onehot, axis=0)
    rank = jnp.take_along_axis(csum, flat_e[:, None], axis=1)[:, 0] - 1
    counts = csum[-1]
    tiles_per = (counts + MOE_TILE - 1) // MOE_TILE
    tile_end = jnp.cumsum(tiles_per)
    tile_start = tile_end - tiles_per
    pos = tile_start[flat_e] * MOE_TILE + rank
    n_tiles = (t * TOP_K) // MOE_TILE + N_EXPERTS
    p = n_tiles * MOE_TILE
    token_of = jnp.zeros((p,), jnp.int32).at[pos].set(jnp.arange(t * TOP_K, dtype=jnp.int32) // TOP_K)
    tile_ids = jnp.arange(n_tiles, dtype=jnp.int32)
    tile_valid = (tile_ids < tile_end[-1]).astype(jnp.int32)
    tile_expert = jnp.minimum(jnp.searchsorted(tile_end, tile_ids, side='right'), N_EXPERTS - 1).astype(jnp.int32)
    last_expert = tile_expert[jnp.maximum(tile_end[-1] - 1, 0)]
    tile_expert = jnp.where(tile_valid > 0, tile_expert, last_expert)

    x_sorted = h.astype(jnp.bfloat16)[token_of]
    y_sorted = expert_ffn(x_sorted, tile_expert, tile_valid, w_gate, w_up, w_down)
    y = y_sorted[pos].reshape(t, TOP_K, d)
    return jnp.sum(y * wsel[:, :, None], axis=1)


def rms_norm(x, g):
    xf = x.astype(jnp.float32)
    y = xf * lax.rsqrt(jnp.mean(xf * xf, axis=-1, keepdims=True) + RMS_EPS)
    return (y * g.astype(jnp.float32)).astype(x.dtype)


def modulate(h, shift, scale):
    return h * (1.0 + scale) + shift


def split_heads(t):
    b, n, _ = t.shape
    return t.reshape(b, n, NA_HEADS, HEAD_DIM).transpose(0, 2, 1, 3)


def merge_heads(t):
    b, h, n, d = t.shape
    return t.transpose(0, 2, 1, 3).reshape(b, n, h * d)


def axial_rope_tables(n_tokens):
    t = jnp.arange(n_tokens)
    row = (t // GRID_W).astype(jnp.float32)
    col = (t % GRID_W).astype(jnp.float32)
    n_freq = HEAD_DIM // 4
    inv = ROPE_THETA ** (-jnp.arange(n_freq, dtype=jnp.float32) / n_freq)
    ang = jnp.concatenate([row[:, None] * inv, col[:, None] * inv], axis=-1)
    return jnp.cos(ang), jnp.sin(ang)


def apply_rope(x, cos, sin):
    xf = x.astype(jnp.float32)
    x1, x2 = jnp.split(xf, 2, axis=-1)
    return jnp.concatenate([x1 * cos - x2 * sin, x1 * sin + x2 * cos], axis=-1).astype(x.dtype)


def _natten_bias_patterns(rpb):
    j = jnp.arange(GRID_W)
    col_start = jnp.clip(j - NA_KW // 2, 0, GRID_W - NA_KW)
    col_mask = (j[None, :] >= col_start[:, None]) & (j[None, :] < col_start[:, None] + NA_KW)
    dc = jnp.clip(j[None, :] - j[:, None], -(NA_KW - 1), NA_KW - 1) + NA_KW - 1
    dr = jnp.arange(NA_KH_MAX)[:, None] + jnp.arange(NA_KH_MAX)[None, :]
    bias = rpb.astype(jnp.float32)[:, dr[:, None, :, None], dc[None, :, None, :]]
    bias = jnp.where(col_mask[None, None, :, None, :], bias, NEG_INF)
    return bias.reshape(NA_HEADS, NA_KH_MAX, GRID_W, NA_KH_MAX * GRID_W)


def _natten_kernel(qr_ref, qp_ref, kr_ref, v_ref, kc_ref, vc_ref, bias_ref, o_ref):
    rows = qr_ref.shape[0] // GRID_W
    n_lat = NA_KH_MAX * GRID_W
    scale = HEAD_DIM ** -0.5
    kc = kc_ref[...]
    vc = vc_ref[...]
    nt = (((1,), (1,)), ((), ()))

    def body(r, carry):
        row_start = jnp.clip(r - NA_KH_MAX // 2, 0, rows - NA_KH_MAX)
        pattern = row_start - r + NA_KH_MAX - 1
        q0 = pl.multiple_of(r * GRID_W, GRID_W)
        k0 = pl.multiple_of(row_start * GRID_W, GRID_W)
        qr = qr_ref[pl.ds(q0, GRID_W), :]
        qp = qp_ref[pl.ds(q0, GRID_W), :]
        ks = kr_ref[pl.ds(k0, n_lat), :]
        vs = v_ref[pl.ds(k0, n_lat), :]
        s_lat = lax.dot_general(qr, ks, nt, preferred_element_type=jnp.float32) * scale + bias_ref[pattern]
        s_ctx = lax.dot_general(qp, kc, nt, preferred_element_type=jnp.float32) * scale
        m = jnp.maximum(jnp.max(s_lat, axis=-1, keepdims=True), jnp.max(s_ctx, axis=-1, keepdims=True))
        p_lat = jnp.exp(s_lat - m)
        p_ctx = jnp.exp(s_ctx - m)
        den = jnp.sum(p_lat, axis=-1, keepdims=True) + jnp.sum(p_ctx, axis=-1, keepdims=True)
        o = (jnp.dot(p_lat.astype(jnp.bfloat16), vs, preferred_element_type=jnp.float32)
             + jnp.dot(p_ctx.astype(jnp.bfloat16), vc, preferred_element_type=jnp.float32))
        o_ref[pl.ds(q0, GRID_W), :] = (o / den).astype(o_ref.dtype)
        return carry

    lax.fori_loop(0, rows, body, 0)


def neighbourhood_attention(q_rot, q_plain, k_rot, v, k_ctx, v_ctx, rpb):
    b, h, s, dh = q_rot.shape
    n_ctx = k_ctx.shape[2]
    bias = _natten_bias_patterns(rpb)
    lat = pl.BlockSpec((None, None, s, dh), lambda i, j: (i, j, 0, 0))
    cx = pl.BlockSpec((None, None, n_ctx, dh), lambda i, j: (i, j, 0, 0))
    return pl.pallas_call(
        _natten_kernel,
        grid=(b, h),
        in_specs=[lat, lat, lat, lat, cx, cx,
                  pl.BlockSpec((None, NA_KH_MAX, GRID_W, NA_KH_MAX * GRID_W), lambda i, j: (j, 0, 0, 0))],
        out_specs=pl.BlockSpec((None, None, s, dh), lambda i, j: (i, j, 0, 0)),
        out_shape=jax.ShapeDtypeStruct((b, h, s, dh), jnp.float32),
        compiler_params=pltpu.CompilerParams(
            dimension_semantics=("parallel", "parallel"), vmem_limit_bytes=VMEM_LIMIT_BYTES),
        name="natten",
    )(q_rot, q_plain, k_rot, v, k_ctx, v_ctx, bias)


def _ctx_attn_kernel(q_ref, k_ref, v_ref, o_ref):
    nt = (((1,), (1,)), ((), ()))
    s = lax.dot_general(q_ref[...], k_ref[...], nt, preferred_element_type=jnp.float32) * (HEAD_DIM ** -0.5)
    p = jnp.exp(s - jnp.max(s, axis=-1, keepdims=True))
    den = jnp.sum(p, axis=-1, keepdims=True)
    o = jnp.dot(p.astype(jnp.bfloat16), v_ref[...], preferred_element_type=jnp.float32)
    o_ref[...] = (o / den).astype(o_ref.dtype)


def context_attention(q, k, v):
    b, h, n, dh = q.shape
    spec = pl.BlockSpec((None, None, n, dh), lambda i, j: (i, j, 0, 0))
    return pl.pallas_call(
        _ctx_attn_kernel,
        grid=(b, h),
        in_specs=[spec, spec, spec],
        out_specs=spec,
        out_shape=jax.ShapeDtypeStruct((b, h, n, dh), jnp.float32),
        compiler_params=pltpu.CompilerParams(dimension_semantics=("parallel", "parallel")),
        name="ctx_attention",
    )(q, k, v)


POOL_PAD = 16


def _pool_kernel(u_ref, w_ref, scale_ref, o_ref, pad_ref):
    n = u_ref.shape[0]
    u = u_ref[...]
    pad_ref[pl.ds(0, POOL_PAD), :] = jnp.zeros((POOL_PAD, POOL_GROUP_DIM), jnp.float32)
    pad_ref[pl.ds(POOL_PAD + n, POOL_PAD), :] = jnp.zeros((POOL_PAD, POOL_GROUP_DIM), jnp.float32)
    pad_ref[pl.ds(POOL_PAD, n), :] = u
    t = lax.broadcasted_iota(jnp.int32, (n, 1), 0)
    for g, w in enumerate(POOL_WINDOWS):
        @pl.when(pl.program_id(1) == g)
        def _(w=w):
            first = POOL_PAD - w // 2
            acc = pad_ref[pl.ds(first, n), :]
            for k in range(1, w):
                acc = acc + pad_ref[pl.ds(first + k, n), :]
            cnt = jnp.minimum(t + (w - w // 2), n) - jnp.maximum(t - w // 2, 0)
            y = acc / cnt.astype(jnp.float32) - u
            y = jnp.dot(y.astype(jnp.bfloat16), w_ref[...].astype(jnp.bfloat16), preferred_element_type=jnp.float32)
            o_ref[...] = y * scale_ref[...]


def multiscale_pool(p, pool_w, pool_scale):
    b, n, _ = p.shape
    col0 = POOL_OFF // POOL_GROUP_DIM
    return pl.pallas_call(
        _pool_kernel,
        grid=(b, POOL_GROUPS),
        in_specs=[pl.BlockSpec((None, n, POOL_GROUP_DIM), lambda i, g: (i, 0, col0 + g)),
                  pl.BlockSpec((None, POOL_GROUP_DIM, POOL_GROUP_DIM), lambda i, g: (g, 0, 0)),
                  pl.BlockSpec((1, POOL_GROUP_DIM), lambda i, g: (0, g))],
        out_specs=pl.BlockSpec((None, n, POOL_GROUP_DIM), lambda i, g: (i, 0, g)),
        out_shape=jax.ShapeDtypeStruct((b, n, BRANCH_WIDTH), jnp.float32),
        scratch_shapes=[pltpu.VMEM((n + 2 * POOL_PAD, POOL_GROUP_DIM), jnp.float32)],
        compiler_params=pltpu.CompilerParams(
            dimension_semantics=("parallel", "parallel"), vmem_limit_bytes=VMEM_LIMIT_BYTES),
        name="multiscale_pool",
    )(p, pool_w, pool_scale.reshape(1, BRANCH_WIDTH))


def depthwise_conv3(u, w, bias):
    y = lax.conv_general_dilated(u, w[:, None, :].astype(u.dtype), window_strides=(1,),
                                 padding=((HY_SHORT // 2, HY_SHORT // 2),),
                                 dimension_numbers=('NWC', 'WIO', 'NWC'),
                                 feature_group_count=u.shape[-1])
    return y + bias.astype(u.dtype)


def hyena_filter_spectrum(n, w1, b1, freq, w2, b2, w3):
    f32 = jnp.float32
    t = jnp.linspace(0.0, 1.0, n, dtype=f32)[:, None]
    bands = (HY_EMB_DIM - 1) // 2
    ang = (2.0 * math.pi / n) * jnp.arange(n, dtype=f32)[:, None]
    f = jnp.linspace(1e-4, bands - 1, bands, dtype=f32)[None, :]
    z = jnp.concatenate([t, jnp.cos(f * ang), -jnp.sin(f * ang)], axis=-1)
    fr = freq.astype(f32)
    hh = jnp.sin(fr * (z @ w1.astype(f32) + b1.astype(f32)))
    hh = jnp.sin(fr * (hh @ w2.astype(f32) + b2.astype(f32)))
    hh = (hh @ w3.astype(f32)).reshape(n, HY_ORDER, 2, BRANCH_WIDTH)
    deltas = jnp.abs(jnp.linspace(HY_MIN_DECAY, HY_MAX_DECAY, BRANCH_WIDTH, dtype=f32))
    hh = hh * jnp.exp(-t * deltas)[:, None, None, :]
    fwd, bwd = hh[:, :, 0], hh[:, :, 1]
    buf = jnp.concatenate([fwd, jnp.zeros((1, HY_ORDER, BRANCH_WIDTH), f32), bwd[:0:-1]], axis=0)
    buf = buf / jnp.sum(jnp.abs(buf), axis=0, keepdims=True)
    return jnp.fft.rfft(buf, axis=0)


def bidirectional_long_conv(u, h_spec, skip):
    n = u.shape[1]
    y = jnp.fft.irfft(jnp.fft.rfft(u, n=2 * n, axis=1) * h_spec[None], n=2 * n, axis=1)[:, :n]
    return y + u * skip


def hyena_mixer(u, short_w, short_b, w1, b1, freq, w2, b2, w3, skip):
    n = u.shape[1]
    uc = depthwise_conv3(u, short_w, short_b).astype(jnp.float32)
    v, x1, x2 = jnp.split(uc, HY_ORDER + 1, axis=-1)
    h_spec = hyena_filter_spectrum(n, w1, b1, freq, w2, b2, w3)
    z = v
    for o, gate in enumerate((x1, x2)):
        z = gate * bidirectional_long_conv(z, h_spec[:, o], skip[o].astype(jnp.float32))
    return z.astype(u.dtype)


def merge_branches(p, y_attn, pool_w, pool_scale, hy_params, w_branch, w_out):
    b, n, _ = p.shape
    y_pool = multiscale_pool(p, pool_w, pool_scale)
    y_hy = hyena_mixer(p[..., HY_OFF:GATE_OFF], *hy_params)
    gates = jax.nn.sigmoid(p[..., GATE_OFF:].astype(jnp.float32)).reshape(b, n, N_BRANCH, D_MODEL)
    ys = (y_attn.astype(p.dtype), y_pool, y_hy)
    merged = 0.0
    for i in range(N_BRANCH):
        br = matmul(ys[i].reshape(b * n, BRANCH_WIDTH), w_branch[i]).reshape(b, n, D_MODEL)
        merged = merged + gates[:, :, i] * br
    return matmul(merged.reshape(b * n, D_MODEL), w_out).reshape(b, n, D_MODEL)


def token_mixers(hl, hc, w_in, rpb, pool_w, pool_scale, hy_params, w_branch, w_out, cos, sin, need_ctx):
    b, s, d = hl.shape
    n_ctx = hc.shape[1]
    pl_ = matmul(hl.reshape(b * s, d), w_in).reshape(b, s, IN_WIDTH)
    pc = matmul(hc.reshape(b * n_ctx, d), w_in).reshape(b, n_ctx, IN_WIDTH)
    ql, kl, vl = [split_heads(t) for t in jnp.split(pl_[..., :POOL_OFF], 3, axis=-1)]
    qc, kc, vc = [split_heads(t) for t in jnp.split(pc[..., :POOL_OFF], 3, axis=-1)]
    bf = jnp.bfloat16
    ya_l = neighbourhood_attention(apply_rope(ql, cos, sin).astype(bf), ql.astype(bf), apply_rope(kl, cos, sin).astype(bf),
                                   vl.astype(bf), kc.astype(bf), vc.astype(bf), rpb)
    out_l = merge_branches(pl_, merge_heads(ya_l), pool_w, pool_scale, hy_params, w_branch, w_out)
    out_c = None
    if need_ctx:
        ya_c = context_attention(qc.astype(bf), kc.astype(bf), vc.astype(bf))
        out_c = merge_branches(pc, merge_heads(ya_c), pool_w, pool_scale, hy_params, w_branch, w_out)
    return out_l, out_c


def kernel(x, c, ctx, c_ctx, norm1_g, norm2_g, w_mod, b_mod, w_in, rpb, pool_w, pool_scale, hy_short_w, hy_short_b, hy_w1, hy_b1, hy_freq, hy_w2, hy_b2, hy_w3, hy_skip, w_branch, w_out, w_router, router_bias, w_gate_e, w_up_e, w_down_e, final_g):
    b, s, d = x.shape
    n_ctx = ctx.shape[1]
    cos, sin = axial_rope_tables(s)
    xl, xc = x, ctx
    for l in range(DEPTH):
        need_ctx = l < DEPTH - 1
        ml = (jax.nn.silu(c) @ w_mod[l] + b_mod[l]).reshape(b, 1, N_MOD, d)
        mc = (jax.nn.silu(c_ctx) @ w_mod[l] + b_mod[l]).reshape(1, 1, N_MOD, d)
        hl = modulate(rms_norm(xl, norm1_g[l]), ml[:, :, 0], ml[:, :, 1])
        hc = modulate(rms_norm(xc, norm1_g[l]), mc[:, :, 0], mc[:, :, 1])
        hy_params = (hy_short_w[l], hy_short_b[l], hy_w1[l], hy_b1[l], hy_freq[l],
                     hy_w2[l], hy_b2[l], hy_w3[l], hy_skip[l])
        out_l, out_c = token_mixers(hl, hc, w_in[l], rpb[l], pool_w[l], pool_scale[l], hy_params,
                                    w_branch[l], w_out[l], cos, sin, need_ctx)
        xl = xl + ml[:, :, 2] * out_l
        hl2 = modulate(rms_norm(xl, norm2_g[l]), ml[:, :, 3], ml[:, :, 4])
        if need_ctx:
            xc = xc + mc[:, :, 2] * out_c
            hc2 = modulate(rms_norm(xc, norm2_g[l]), mc[:, :, 3], mc[:, :, 4])
            tokens = jnp.concatenate([hl2.reshape(b * s, d), hc2.reshape(b * n_ctx, d)], axis=0)
            f = grouped_moe(tokens, w_router, router_bias, w_gate_e[l], w_up_e[l], w_down_e[l])
            xl = xl + ml[:, :, 5] * f[:b * s].reshape(b, s, d)
            xc = xc + mc[:, :, 5] * f[b * s:].reshape(b, n_ctx, d)
        else:
            f = grouped_moe(hl2.reshape(b * s, d), w_router, router_bias, w_gate_e[l], w_up_e[l], w_down_e[l])
            xl = xl + ml[:, :, 5] * f.reshape(b, s, d)
    return rms_norm(xl, final_g)
```

```python
import functools
import math

import jax
import jax.numpy as jnp
from jax import lax
from jax.experimental import pallas as pl
from jax.experimental.pallas import tpu as pltpu

D_MODEL = 1024
DEPTH = 4
GRID_W = 64
BRANCH_WIDTH = D_MODEL // 2
N_BRANCH = 3
HEAD_DIM = 64
NA_HEADS = BRANCH_WIDTH // HEAD_DIM
NA_KH_MAX = 8
NA_KW = 16
ROPE_THETA = 10000.0
NEG_INF = -1e30
POOL_WINDOWS = (2, 4, 8, 16)
POOL_GROUPS = len(POOL_WINDOWS)
POOL_GROUP_DIM = BRANCH_WIDTH // POOL_GROUPS
HY_ORDER = 2
HY_SHORT = 3
HY_EMB_DIM = 33
HY_FILTER_HIDDEN = 64
HY_MIN_DECAY = math.log(1e-2) / 0.3
HY_MAX_DECAY = math.log(1e-2) / 1.5
N_EXPERTS = 16
N_GROUPS = 4
EXPERTS_PER_GROUP = N_EXPERTS // N_GROUPS
TOP_K = 2
EXPERT_FF = D_MODEL // 2
N_MOD = 6
RMS_EPS = 1e-6
POOL_OFF = 3 * BRANCH_WIDTH
HY_OFF = POOL_OFF + BRANCH_WIDTH
GATE_OFF = HY_OFF + (HY_ORDER + 1) * BRANCH_WIDTH
IN_WIDTH = GATE_OFF + N_BRANCH * D_MODEL

VMEM_LIMIT_BYTES = 48 * 1024 * 1024
MOE_TILE = 256


def _mm_kernel(a_ref, b_ref, o_ref):
    o_ref[...] = jnp.dot(a_ref[...].astype(jnp.bfloat16), b_ref[...].astype(jnp.bfloat16),
                         preferred_element_type=jnp.float32).astype(o_ref.dtype)


def _pick_tile(n, cands):
    for c in cands:
        if n % c == 0:
            return c
    return n


def matmul(a, b, out_dtype=jnp.float32):
    m, k = a.shape
    _, n = b.shape
    tm = _pick_tile(m, (1024, 512, 256, 128, 64, 32, 16, 8))
    tn = _pick_tile(n, (1664, 1024, 512, 256, 128))
    return pl.pallas_call(
        _mm_kernel,
        grid=(m // tm, n // tn),
        in_specs=[pl.BlockSpec((tm, k), lambda i, j: (i, 0)),
                  pl.BlockSpec((k, tn), lambda i, j: (0, j))],
        out_specs=pl.BlockSpec((tm, tn), lambda i, j: (i, j)),
        out_shape=jax.ShapeDtypeStruct((m, n), out_dtype),
        compiler_params=pltpu.CompilerParams(
            dimension_semantics=("parallel", "parallel"), vmem_limit_bytes=VMEM_LIMIT_BYTES),
        name="dense_matmul",
    )(a, b)


def _expert_kernel(src_ref, dst_ref, tile_expert_ref, n_valid_ref, h_hbm, wg_ref, wu_ref, wd_ref, y_hbm,
                   xbuf, obuf, zbuf, gsem, ssem, zsem):
    i = pl.program_id(0)
    n_valid = n_valid_ref[0]
    slot = i % 2

    def gather(tile, s):
        def row(r, carry):
            pltpu.make_async_copy(h_hbm.at[pl.ds(src_ref[tile * MOE_TILE + r], 1)],
                                  xbuf.at[s, pl.ds(r, 1)], gsem.at[s]).start()
            return carry
        lax.fori_loop(0, MOE_TILE, row, 0, unroll=8)

    def wait_rows(buf, sem, s):
        pltpu.make_async_copy(buf.at[s], buf.at[s], sem.at[s]).wait()

    @pl.when((i == 0) & (n_valid > 0))
    def _():
        gather(0, 0)

    @pl.when(i + 1 < n_valid)
    def _():
        gather(i + 1, 1 - slot)

    @pl.when(i < n_valid)
    def _():
        wait_rows(xbuf, gsem, slot)

        @pl.when(i >= 2)
        def _():
            wait_rows(obuf, ssem, slot)

        x = xbuf[slot].astype(jnp.bfloat16)
        g = jnp.dot(x, wg_ref[...].astype(jnp.bfloat16), preferred_element_type=jnp.float32)
        u = jnp.dot(x, wu_ref[...].astype(jnp.bfloat16), preferred_element_type=jnp.float32)
        hid = (g * jax.nn.sigmoid(g)) * u
        obuf[slot] = jnp.dot(hid.astype(jnp.bfloat16), wd_ref[...].astype(jnp.bfloat16),
                             preferred_element_type=jnp.float32)

        def row(r, carry):
            pltpu.make_async_copy(obuf.at[slot, pl.ds(r, 1)],
                                  y_hbm.at[pl.ds(dst_ref[i * MOE_TILE + r], 1)], ssem.at[slot]).start()
            return carry
        lax.fori_loop(0, MOE_TILE, row, 0, unroll=8)

    @pl.when(i >= n_valid)
    def _():
        zbuf[...] = jnp.zeros_like(zbuf)

        def row(r, carry):
            pltpu.make_async_copy(zbuf.at[pl.ds(r, 1)], y_hbm.at[pl.ds(dst_ref[i * MOE_TILE + r], 1)],
                                  zsem.at[0]).start()
            return carry
        lax.fori_loop(0, MOE_TILE, row, 0, unroll=8)
        pltpu.make_async_copy(zbuf, zbuf, zsem.at[0]).wait()

    @pl.when(i == pl.num_programs(0) - 1)
    def _():
        @pl.when(n_valid >= 1)
        def _():
            wait_rows(obuf, ssem, (n_valid - 1) % 2)

        @pl.when(n_valid >= 2)
        def _():
            wait_rows(obuf, ssem, n_valid % 2)


def expert_ffn(h, src_row, dst_row, tile_expert, n_valid, w_gate, w_up, w_down):
    _, d = h.shape
    p = src_row.shape[0]
    n_tiles = p // MOE_TILE
    grid_spec = pltpu.PrefetchScalarGridSpec(
        num_scalar_prefetch=4,
        grid=(n_tiles,),
        in_specs=[
            pl.BlockSpec(memory_space=pl.ANY),
            pl.BlockSpec((None, d, EXPERT_FF), lambda i, s, t, te, nv: (te[i], 0, 0)),
            pl.BlockSpec((None, d, EXPERT_FF), lambda i, s, t, te, nv: (te[i], 0, 0)),
            pl.BlockSpec((None, EXPERT_FF, d), lambda i, s, t, te, nv: (te[i], 0, 0)),
        ],
        out_specs=pl.BlockSpec(memory_space=pl.ANY),
        scratch_shapes=[pltpu.VMEM((2, MOE_TILE, d), jnp.float32), pltpu.VMEM((2, MOE_TILE, d), jnp.float32),
                        pltpu.VMEM((MOE_TILE, d), jnp.float32),
                        pltpu.SemaphoreType.DMA((2,)), pltpu.SemaphoreType.DMA((2,)), pltpu.SemaphoreType.DMA((1,))],
    )
    return pl.pallas_call(
        _expert_kernel,
        grid_spec=grid_spec,
        out_shape=jax.ShapeDtypeStruct((p, d), jnp.float32),
        compiler_params=pltpu.CompilerParams(
            dimension_semantics=("arbitrary",), vmem_limit_bytes=VMEM_LIMIT_BYTES),
        name="expert_ffn",
    )(src_row, dst_row, tile_expert, n_valid, h, w_gate, w_up, w_down)


def grouped_moe(h, w_router, router_bias, w_gate, w_up, w_down):
    t, d = h.shape
    aff = jax.nn.sigmoid(jnp.dot(h, w_router, precision=lax.Precision.HIGHEST).astype(jnp.float32))
    biased = (aff + router_bias.astype(jnp.float32)).reshape(-1, N_GROUPS, EXPERTS_PER_GROUP)
    group_score = jnp.sum(lax.top_k(biased, TOP_K)[0], axis=-1)
    group = jnp.argmax(group_score, axis=-1)
    in_group = jnp.take_along_axis(biased, group[:, None, None], axis=1)[:, 0]
    local = lax.top_k(in_group, TOP_K)[1]
    expert = (group[:, None] * EXPERTS_PER_GROUP + local).astype(jnp.int32)
    wsel = jnp.take_along_axis(aff, expert, axis=-1)
    wsel = wsel / jnp.sum(wsel, axis=-1, keepdims=True)

    flat_e = expert.reshape(-1)
    onehot = (flat_e[:, None] == jnp.arange(N_EXPERTS, dtype=jnp.int32)[None, :]).astype(jnp.int32)
    csum = jnp.cumsum(onehot, axis=0)
    rank = jnp.take_along_axis(csum, flat_e[:, None], axis=1)[:, 0] - 1
    counts = csum[-1]
    tiles_per = (counts + MOE_TILE - 1) // MOE_TILE
    tile_end = jnp.cumsum(tiles_per)
    tile_start = tile_end - tiles_per
    pos = tile_start[flat_e] * MOE_TILE + rank
    n_tiles = (t * TOP_K) // MOE_TILE + N_EXPERTS
    p = n_tiles * MOE_TILE
    n_assign = t * TOP_K
    assign_of = jnp.full((p,), -1, jnp.int32).at[pos].set(jnp.arange(n_assign, dtype=jnp.int32))
    is_pad = assign_of < 0
    src_row = jnp.where(is_pad, 0, assign_of // TOP_K)
    dst_row = jnp.where(is_pad, n_assign + jnp.cumsum(is_pad.astype(jnp.int32)) - 1, assign_of)
    tile_ids = jnp.arange(n_tiles, dtype=jnp.int32)
    n_valid = tile_end[-1].astype(jnp.int32)
    tile_expert = jnp.minimum(jnp.searchsorted(tile_end, tile_ids, side='right'), N_EXPERTS - 1).astype(jnp.int32)
    last_expert = tile_expert[jnp.maximum(n_valid - 1, 0)]
    tile_expert = jnp.where(tile_ids < n_valid, tile_expert, last_expert)

    y = expert_ffn(h, src_row, dst_row, tile_expert, n_valid.reshape(1), w_gate, w_up, w_down)
    y = y[:n_assign].reshape(t, TOP_K, d)
    return jnp.sum(y * wsel[:, :, None], axis=1)


def rms_norm(x, g):
    xf = x.astype(jnp.float32)
    y = xf * lax.rsqrt(jnp.mean(xf * xf, axis=-1, keepdims=True) + RMS_EPS)
    return (y * g.astype(jnp.float32)).astype(x.dtype)


def modulate(h, shift, scale):
    return h * (1.0 + scale) + shift


def split_heads(t):
    b, n, _ = t.shape
    return t.reshape(b, n, NA_HEADS, HEAD_DIM).transpose(0, 2, 1, 3)


def merge_heads(t):
    b, h, n, d = t.shape
    return t.transpose(0, 2, 1, 3).reshape(b, n, h * d)


def axial_rope_tables(n_tokens):
    t = jnp.arange(n_tokens)
    row = (t // GRID_W).astype(jnp.float32)
    col = (t % GRID_W).astype(jnp.float32)
    n_freq = HEAD_DIM // 4
    inv = ROPE_THETA ** (-jnp.arange(n_freq, dtype=jnp.float32) / n_freq)
    ang = jnp.concatenate([row[:, None] * inv, col[:, None] * inv], axis=-1)
    return jnp.cos(ang), jnp.sin(ang)


def apply_rope(x, cos, sin):
    xf = x.astype(jnp.float32)
    x1, x2 = jnp.split(xf, 2, axis=-1)
    return jnp.concatenate([x1 * cos - x2 * sin, x1 * sin + x2 * cos], axis=-1).astype(x.dtype)


def _natten_bias_patterns(rpb):
    j = jnp.arange(GRID_W)
    col_start = jnp.clip(j - NA_KW // 2, 0, GRID_W - NA_KW)
    col_mask = (j[None, :] >= col_start[:, None]) & (j[None, :] < col_start[:, None] + NA_KW)
    dc = jnp.clip(j[None, :] - j[:, None], -(NA_KW - 1), NA_KW - 1) + NA_KW - 1
    dr = jnp.arange(NA_KH_MAX)[:, None] + jnp.arange(NA_KH_MAX)[None, :]
    bias = rpb.astype(jnp.float32)[:, dr[:, None, :, None], dc[None, :, None, :]]
    bias = jnp.where(col_mask[None, None, :, None, :], bias, NEG_INF)
    return bias.reshape(NA_HEADS, NA_KH_MAX, GRID_W, NA_KH_MAX * GRID_W)


def _natten_kernel(qr_ref, qp_ref, kr_ref, v_ref, kc_ref, vc_ref, bias_ref, o_ref):
    rows = qr_ref.shape[0] // GRID_W
    n_lat = NA_KH_MAX * GRID_W
    scale = HEAD_DIM ** -0.5
    kc = kc_ref[...]
    vc = vc_ref[...]
    nt = (((1,), (1,)), ((), ()))

    def body(r, carry):
        row_start = jnp.clip(r - NA_KH_MAX // 2, 0, rows - NA_KH_MAX)
        pattern = row_start - r + NA_KH_MAX - 1
        q0 = pl.multiple_of(r * GRID_W, GRID_W)
        k0 = pl.multiple_of(row_start * GRID_W, GRID_W)
        qr = qr_ref[pl.ds(q0, GRID_W), :]
        qp = qp_ref[pl.ds(q0, GRID_W), :]
        ks = kr_ref[pl.ds(k0, n_lat), :]
        vs = v_ref[pl.ds(k0, n_lat), :]
        s_lat = lax.dot_general(qr, ks, nt, preferred_element_type=jnp.float32) * scale + bias_ref[pattern]
        s_ctx = lax.dot_general(qp, kc, nt, preferred_element_type=jnp.float32) * scale
        m = jnp.maximum(jnp.max(s_lat, axis=-1, keepdims=True), jnp.max(s_ctx, axis=-1, keepdims=True))
        p_lat = jnp.exp(s_lat - m)
        p_ctx = jnp.exp(s_ctx - m)
        den = jnp.sum(p_lat, axis=-1, keepdims=True) + jnp.sum(p_ctx, axis=-1, keepdims=True)
        o = (jnp.dot(p_lat.astype(jnp.bfloat16), vs, preferred_element_type=jnp.float32)
             + jnp.dot(p_ctx.astype(jnp.bfloat16), vc, preferred_element_type=jnp.float32))
        o_ref[pl.ds(q0, GRID_W), :] = (o / den).astype(o_ref.dtype)
        return carry

    lax.fori_loop(0, rows, body, 0)


def neighbourhood_attention(q_rot, q_plain, k_rot, v, k_ctx, v_ctx, rpb):
    b, h, s, dh = q_rot.shape
    n_ctx = k_ctx.shape[2]
    bias = _natten_bias_patterns(rpb)
    lat = pl.BlockSpec((None, None, s, dh), lambda i, j: (i, j, 0, 0))
    cx = pl.BlockSpec((None, None, n_ctx, dh), lambda i, j: (i, j, 0, 0))
    return pl.pallas_call(
        _natten_kernel,
        grid=(b, h),
        in_specs=[lat, lat, lat, lat, cx, cx,
                  pl.BlockSpec((None, NA_KH_MAX, GRID_W, NA_KH_MAX * GRID_W), lambda i, j: (j, 0, 0, 0))],
        out_specs=pl.BlockSpec((None, None, s, dh), lambda i, j: (i, j, 0, 0)),
        out_shape=jax.ShapeDtypeStruct((b, h, s, dh), jnp.float32),
        compiler_params=pltpu.CompilerParams(
            dimension_semantics=("parallel", "parallel"), vmem_limit_bytes=VMEM_LIMIT_BYTES),
        name="natten",
    )(q_rot, q_plain, k_rot, v, k_ctx, v_ctx, bias)


def _ctx_attn_kernel(q_ref, k_ref, v_ref, o_ref):
    nt = (((1,), (1,)), ((), ()))
    s = lax.dot_general(q_ref[...], k_ref[...], nt, preferred_element_type=jnp.float32) * (HEAD_DIM ** -0.5)
    p = jnp.exp(s - jnp.max(s, axis=-1, keepdims=True))
    den = jnp.sum(p, axis=-1, keepdims=True)
    o = jnp.dot(p.astype(jnp.bfloat16), v_ref[...], preferred_element_type=jnp.float32)
    o_ref[...] = (o / den).astype(o_ref.dtype)


def context_attention(q, k, v):
    b, h, n, dh = q.shape
    spec = pl.BlockSpec((None, None, n, dh), lambda i, j: (i, j, 0, 0))
    return pl.pallas_call(
        _ctx_attn_kernel,
        grid=(b, h),
        in_specs=[spec, spec, spec],
        out_specs=spec,
        out_shape=jax.ShapeDtypeStruct((b, h, n, dh), jnp.float32),
        compiler_params=pltpu.CompilerParams(dimension_semantics=("parallel", "parallel")),
        name="ctx_attention",
    )(q, k, v)


POOL_PAD = 16


def _pool_kernel(u_ref, w_ref, scale_ref, o_ref, pad_ref):
    n = u_ref.shape[0]
    u = u_ref[...]
    pad_ref[pl.ds(0, POOL_PAD), :] = jnp.zeros((POOL_PAD, POOL_GROUP_DIM), jnp.float32)
    pad_ref[pl.ds(POOL_PAD + n, POOL_PAD), :] = jnp.zeros((POOL_PAD, POOL_GROUP_DIM), jnp.float32)
    pad_ref[pl.ds(POOL_PAD, n), :] = u
    t = lax.broadcasted_iota(jnp.int32, (n, 1), 0)
    for g, w in enumerate(POOL_WINDOWS):
        @pl.when(pl.program_id(1) == g)
        def _(w=w):
            first = POOL_PAD - w // 2
            acc = pad_ref[pl.ds(first, n), :]
            for k in range(1, w):
                acc = acc + pad_ref[pl.ds(first + k, n), :]
            cnt = jnp.minimum(t + (w - w // 2), n) - jnp.maximum(t - w // 2, 0)
            y = acc / cnt.astype(jnp.float32) - u
            y = jnp.dot(y.astype(jnp.bfloat16), w_ref[...].astype(jnp.bfloat16), preferred_element_type=jnp.float32)
            o_ref[...] = y * scale_ref[...]


def multiscale_pool(p, pool_w, pool_scale):
    b, n, _ = p.shape
    col0 = POOL_OFF // POOL_GROUP_DIM
    return pl.pallas_call(
        _pool_kernel,
        grid=(b, POOL_GROUPS),
        in_specs=[pl.BlockSpec((None, n, POOL_GROUP_DIM), lambda i, g: (i, 0, col0 + g)),
                  pl.BlockSpec((None, POOL_GROUP_DIM, POOL_GROUP_DIM), lambda i, g: (g, 0, 0)),
                  pl.BlockSpec((1, POOL_GROUP_DIM), lambda i, g: (0, g))],
        out_specs=pl.BlockSpec((None, n, POOL_GROUP_DIM), lambda i, g: (i, 0, g)),
        out_shape=jax.ShapeDtypeStruct((b, n, BRANCH_WIDTH), jnp.float32),
        scratch_shapes=[pltpu.VMEM((n + 2 * POOL_PAD, POOL_GROUP_DIM), jnp.float32)],
        compiler_params=pltpu.CompilerParams(
            dimension_semantics=("parallel", "parallel"), vmem_limit_bytes=VMEM_LIMIT_BYTES),
        name="multiscale_pool",
    )(p, pool_w, pool_scale.reshape(1, BRANCH_WIDTH))


HY_CHUNK = 64
HY_PITCH = 72
HY_LANES = 128
HY_FEAT = 128
HY_VMEM_LIMIT = 56 * 1024 * 1024


def _cis_tables(num, den):
    ang = (-2.0 * math.pi / den) * (num % den).astype(jnp.float32)
    return jnp.cos(ang), jnp.sin(ang)


def _stack_complex(mr, mi):
    return jnp.concatenate([jnp.concatenate([mr, -mi], axis=-1), jnp.concatenate([mi, mr], axis=-1)], axis=-2)


def _hy_dft_tables(n):
    big = 2 * n
    n1_full = big // HY_CHUNK
    k1 = jnp.arange(n1_full, dtype=jnp.int32)
    n1 = jnp.arange(n1_full, dtype=jnp.int32)
    n2 = jnp.arange(HY_CHUNK, dtype=jnp.int32)
    mr, mi = _cis_tables(k1[None, :, None] * (HY_CHUNK * n1[None, None, :] + n2[:, None, None]), big)
    half = n1_full // 2
    fwd_a = _stack_complex(mr[:, :, :half], mi[:, :, :half])
    fwd_a_real = jnp.concatenate([mr, mi], axis=-2)
    mrt = jnp.transpose(mr[:, :, :half], (0, 2, 1))
    mit = -jnp.transpose(mi[:, :, :half], (0, 2, 1))
    inv_a = _stack_complex(mrt, mit)
    er, ei = _cis_tables(n2[:, None] * n2[None, :], HY_CHUNK)
    fwd_b = _stack_complex(er, ei)
    inv_b = _stack_complex(er, -ei)
    bf = jnp.bfloat16
    return fwd_a.astype(bf), fwd_a_real.astype(bf), inv_a.astype(bf), fwd_b.astype(bf), inv_b.astype(bf)


def _hy_positions(n):
    f32 = jnp.float32
    t = jnp.linspace(0.0, 1.0, n, dtype=f32)[:, None]
    bands = (HY_EMB_DIM - 1) // 2
    ang = (2.0 * math.pi / n) * jnp.arange(n, dtype=f32)[:, None]
    f = jnp.linspace(1e-4, bands - 1, bands, dtype=f32)[None, :]
    z = jnp.concatenate([t, jnp.cos(f * ang), -jnp.sin(f * ang)], axis=-1)
    z = jnp.pad(z, ((0, 0), (0, HY_FEAT - HY_EMB_DIM)))
    zb = jnp.concatenate([z[:1], z[:0:-1]], axis=0)
    return z, zb


def _filter_mlp(z, w1, b1, fr, w2, b2):
    hp = lax.Precision.HIGHEST
    h = jnp.sin(fr * (jnp.dot(z, w1, precision=hp, preferred_element_type=jnp.float32) + b1))
    return jnp.sin(fr * (jnp.dot(h, w2, precision=hp, preferred_element_type=jnp.float32) + b2))


def _filter_taps(zf_ref, zb_ref, w1_ref, b1_ref, fr_ref, w2_ref, b2_ref, w3f_ref, w3b_ref, delta_ref, hid_ref):
    n = zf_ref.shape[0]
    hp = lax.Precision.HIGHEST

    @pl.when((pl.program_id(1) == 0) & (pl.program_id(2) == 0))
    def _():
        args = (w1_ref[...], b1_ref[...], fr_ref[...], w2_ref[...], b2_ref[...])
        hid_ref[0] = _filter_mlp(zf_ref[...], *args)
        hid_ref[1] = _filter_mlp(zb_ref[...], *args)

    delta = delta_ref[...]
    hf = jnp.dot(hid_ref[0], w3f_ref[...], precision=hp, preferred_element_type=jnp.float32)
    hf = hf * jnp.exp(-zf_ref[:, 0:1] * delta)
    hb = jnp.dot(hid_ref[1], w3b_ref[...], precision=hp, preferred_element_type=jnp.float32)
    hb = hb * jnp.exp(-zb_ref[:, 0:1] * delta)
    row = lax.broadcasted_iota(jnp.int32, (n, 1), 0)
    hb = jnp.where(row > 0, hb, 0.0)
    norm = jnp.sum(jnp.abs(hf), axis=0, keepdims=True) + jnp.sum(jnp.abs(hb), axis=0, keepdims=True)
    return hf / norm, hb / norm


def _filter_spec_kernel(zf_ref, zb_ref, w1_ref, b1_ref, fr_ref, w2_ref, b2_ref, w3f_ref, w3b_ref, delta_ref,
                        fa_ref, fb_ref, hr_ref, hi_ref, hid_ref, tbuf, sbuf):
    n = zf_ref.shape[0]
    chunks = n // HY_CHUNK
    n1_full = 2 * chunks
    hf, hb = _filter_taps(zf_ref, zb_ref, w1_ref, b1_ref, fr_ref, w2_ref, b2_ref, w3f_ref, w3b_ref, delta_ref, hid_ref)
    for c in range(chunks):
        tbuf[pl.ds(c * HY_PITCH, HY_CHUNK), :] = hf[c * HY_CHUNK:(c + 1) * HY_CHUNK]
        tbuf[pl.ds((chunks + c) * HY_PITCH, HY_CHUNK), :] = hb[c * HY_CHUNK:(c + 1) * HY_CHUNK]

    def stage_a(n2, carry):
        rows = tbuf[pl.ds(n2, n1_full, stride=HY_PITCH), :].astype(jnp.bfloat16)
        a = jnp.dot(fa_ref[n2], rows, preferred_element_type=jnp.float32)
        sbuf[0, pl.ds(n2, n1_full, stride=HY_PITCH), :] = a[:n1_full]
        sbuf[1, pl.ds(n2, n1_full, stride=HY_PITCH), :] = a[n1_full:]
        return carry

    lax.fori_loop(0, HY_CHUNK, stage_a, 0)
    inv_len = 1.0 / (2 * n)

    def stage_b(k1, carry):
        r0 = pl.multiple_of(k1 * HY_PITCH, 8)
        x = jnp.concatenate([sbuf[0, pl.ds(r0, HY_CHUNK), :], sbuf[1, pl.ds(r0, HY_CHUNK), :]], axis=0)
        y = jnp.dot(fb_ref[...], x.astype(jnp.bfloat16), preferred_element_type=jnp.float32) * inv_len
        o0 = pl.multiple_of(k1 * HY_CHUNK, HY_CHUNK)
        hr_ref[pl.ds(o0, HY_CHUNK), :] = y[:HY_CHUNK]
        hi_ref[pl.ds(o0, HY_CHUNK), :] = y[HY_CHUNK:]
        return carry

    lax.fori_loop(0, n1_full, stage_b, 0)


def hyena_filter_spectra(n, hy_w1, hy_b1, hy_freq, hy_w2, hy_b2, hy_w3, tables):
    depth = hy_w1.shape[0]
    zf, zb = _hy_positions(n)
    tiles = BRANCH_WIDTH // HY_LANES
    w1 = jnp.pad(hy_w1, ((0, 0), (0, HY_FEAT - HY_EMB_DIM), (0, 0)))
    deltas = jnp.abs(jnp.linspace(HY_MIN_DECAY, HY_MAX_DECAY, BRANCH_WIDTH, dtype=jnp.float32)).reshape(1, BRANCH_WIDTH)
    fwd_a_real, fwd_b = tables[1], tables[3]
    n1_full = 2 * n // HY_CHUNK
    one = pl.Buffered(1)
    full = lambda shape: pl.BlockSpec(shape, lambda l, o, j: (0,) * len(shape), pipeline_mode=one)
    per_layer = lambda shape: pl.BlockSpec((None,) + shape, lambda l, o, j: (l,) + (0,) * len(shape))
    out_spec = pl.BlockSpec((None, None, 2 * n, HY_LANES), lambda l, o, j: (l, o, 0, j), pipeline_mode=one)
    out_sds = jax.ShapeDtypeStruct((depth, HY_ORDER, 2 * n, BRANCH_WIDTH), jnp.float32)
    return pl.pallas_call(
        _filter_spec_kernel,
        grid=(depth, HY_ORDER, tiles),
        in_specs=[full((n, HY_FEAT)), full((n, HY_FEAT)),
                  per_layer((HY_FEAT, HY_FILTER_HIDDEN)), per_layer((1, HY_FILTER_HIDDEN)),
                  per_layer((1, HY_FILTER_HIDDEN)),
                  per_layer((HY_FILTER_HIDDEN, HY_FILTER_HIDDEN)), per_layer((1, HY_FILTER_HIDDEN)),
                  pl.BlockSpec((None, HY_FILTER_HIDDEN, HY_LANES), lambda l, o, j: (l, 0, o * 2 * tiles + j)),
                  pl.BlockSpec((None, HY_FILTER_HIDDEN, HY_LANES), lambda l, o, j: (l, 0, o * 2 * tiles + tiles + j)),
                  pl.BlockSpec((1, HY_LANES), lambda l, o, j: (0, j)),
                  full((HY_CHUNK, 2 * n1_full, n1_full)), full((2 * HY_CHUNK, 2 * HY_CHUNK))],
        out_specs=[out_spec, out_spec],
        out_shape=[out_sds, out_sds],
        scratch_shapes=[pltpu.VMEM((2, n, HY_FILTER_HIDDEN), jnp.float32),
                        pltpu.VMEM((n1_full * HY_PITCH, HY_LANES), jnp.float32),
                        pltpu.VMEM((2, n1_full * HY_PITCH, HY_LANES), jnp.float32)],
        compiler_params=pltpu.CompilerParams(
            dimension_semantics=("arbitrary", "arbitrary", "arbitrary"), vmem_limit_bytes=HY_VMEM_LIMIT),
        name="hyena_filter_spectra",
    )(zf, zb, w1, hy_b1[:, None, :], hy_freq[:, None, :], hy_w2, hy_b2[:, None, :], hy_w3, hy_w3, deltas,
      fwd_a_real, fwd_b)


def _short_conv(u_ref, b, w_ref, bias_ref, pad_ref):
    n = u_ref.shape[1]
    x = u_ref[b]
    pad_ref[pl.ds(0, 8), :] = jnp.zeros((8, HY_LANES), jnp.float32)
    pad_ref[pl.ds(8 + n, 8), :] = jnp.zeros((8, HY_LANES), jnp.float32)
    pad_ref[pl.ds(8, n), :] = x
    return (pad_ref[pl.ds(7, n), :] * w_ref[0:1, :] + x * w_ref[1:2, :] + pad_ref[pl.ds(9, n), :] * w_ref[2:3, :]
            + bias_ref[...])


def _hyena_conv_kernel(z_ref, g_ref, wz_ref, bz_ref, wg_ref, bg_ref, skip_ref, hr_ref, hi_ref,
                       fa_ref, ia_ref, fb_ref, ib_ref, o_ref, pad_ref, zbuf, sbuf, *, conv_z):
    n = z_ref.shape[1]
    chunks = n // HY_CHUNK
    n1_full = 2 * chunks
    for b in range(2):
        z = _short_conv(z_ref, b, wz_ref, bz_ref, pad_ref) if conv_z else z_ref[b]
        for c in range(chunks):
            zbuf[b, pl.ds(c * HY_PITCH, HY_CHUNK), :] = z[c * HY_CHUNK:(c + 1) * HY_CHUNK]

    def stage_a(n2, carry):
        x = jnp.concatenate([zbuf[0, pl.ds(n2, chunks, stride=HY_PITCH), :],
                             zbuf[1, pl.ds(n2, chunks, stride=HY_PITCH), :]], axis=0)
        a = jnp.dot(fa_ref[n2], x.astype(jnp.bfloat16), preferred_element_type=jnp.float32)
        sbuf[0, pl.ds(n2, n1_full, stride=HY_PITCH), :] = a[:n1_full]
        sbuf[1, pl.ds(n2, n1_full, stride=HY_PITCH), :] = a[n1_full:]
        return carry

    lax.fori_loop(0, HY_CHUNK, stage_a, 0)

    def stage_b(k1, carry):
        r0 = pl.multiple_of(k1 * HY_PITCH, 8)
        x = jnp.concatenate([sbuf[0, pl.ds(r0, HY_CHUNK), :], sbuf[1, pl.ds(r0, HY_CHUNK), :]], axis=0)
        s = jnp.dot(fb_ref[...], x.astype(jnp.bfloat16), preferred_element_type=jnp.float32)
        h0 = pl.multiple_of(k1 * HY_CHUNK, HY_CHUNK)
        hr = hr_ref[pl.ds(h0, HY_CHUNK), :]
        hi = hi_ref[pl.ds(h0, HY_CHUNK), :]
        sr, si = s[:HY_CHUNK], s[HY_CHUNK:]
        y = jnp.concatenate([sr * hr - si * hi, sr * hi + si * hr], axis=0)
        c = jnp.dot(ib_ref[...], y.astype(jnp.bfloat16), preferred_element_type=jnp.float32)
        sbuf[0, pl.ds(r0, HY_CHUNK), :] = c[:HY_CHUNK]
        sbuf[1, pl.ds(r0, HY_CHUNK), :] = c[HY_CHUNK:]
        return carry

    lax.fori_loop(0, n1_full, stage_b, 0)
    skip = skip_ref[...]

    def stage_c(n2, carry):
        x = jnp.concatenate([sbuf[0, pl.ds(n2, n1_full, stride=HY_PITCH), :],
                             sbuf[1, pl.ds(n2, n1_full, stride=HY_PITCH), :]], axis=0)
        y = jnp.dot(ia_ref[n2], x.astype(jnp.bfloat16), preferred_element_type=jnp.float32)
        for b in range(2):
            zb = zbuf[b, pl.ds(n2, chunks, stride=HY_PITCH), :]
            zbuf[b, pl.ds(n2, chunks, stride=HY_PITCH), :] = y[b * chunks:(b + 1) * chunks] + zb * skip
        return carry

    lax.fori_loop(0, HY_CHUNK, stage_c, 0)
    for b in range(2):
        g = _short_conv(g_ref, b, wg_ref, bg_ref, pad_ref)
        for c in range(chunks):
            o_ref[b, pl.ds(c * HY_CHUNK, HY_CHUNK), :] = (
                g[c * HY_CHUNK:(c + 1) * HY_CHUNK] * zbuf[b, pl.ds(c * HY_PITCH, HY_CHUNK), :])


def hyena_long_conv(z, z_col0, g, g_col0, conv_z, short_w, short_b, w_off_z, w_off_g, skip, hr, hi, tables):
    b, n, _ = g.shape
    tiles = BRANCH_WIDTH // HY_LANES
    fwd_a, _, inv_a, fwd_b, inv_b = tables
    n1_full = 2 * n // HY_CHUNK
    one = pl.Buffered(1)
    zc, gc = z_col0 // HY_LANES, g_col0 // HY_LANES
    wz, wg = w_off_z // HY_LANES, w_off_g // HY_LANES
    const = lambda shape: pl.BlockSpec(shape, lambda j, p: (0,) * len(shape), pipeline_mode=one)
    return pl.pallas_call(
        functools.partial(_hyena_conv_kernel, conv_z=conv_z),
        grid=(tiles, b // 2),
        in_specs=[pl.BlockSpec((2, n, HY_LANES), lambda j, p: (p, 0, zc + j), pipeline_mode=one),
                  pl.BlockSpec((2, n, HY_LANES), lambda j, p: (p, 0, gc + j), pipeline_mode=one),
                  pl.BlockSpec((HY_SHORT, HY_LANES), lambda j, p: (0, wz + j)),
                  pl.BlockSpec((1, HY_LANES), lambda j, p: (0, wz + j)),
                  pl.BlockSpec((HY_SHORT, HY_LANES), lambda j, p: (0, wg + j)),
                  pl.BlockSpec((1, HY_LANES), lambda j, p: (0, wg + j)),
                  pl.BlockSpec((1, HY_LANES), lambda j, p: (0, j)),
                  pl.BlockSpec((2 * n, HY_LANES), lambda j, p: (0, j), pipeline_mode=one),
                  pl.BlockSpec((2 * n, HY_LANES), lambda j, p: (0, j), pipeline_mode=one),
                  const((HY_CHUNK, 2 * n1_full, n1_full)), const((HY_CHUNK, n1_full, 2 * n1_full)),
                  const((2 * HY_CHUNK, 2 * HY_CHUNK)), const((2 * HY_CHUNK, 2 * HY_CHUNK))],
        out_specs=pl.BlockSpec((2, n, HY_LANES), lambda j, p: (p, 0, j), pipeline_mode=one),
        out_shape=jax.ShapeDtypeStruct((b, n, BRANCH_WIDTH), jnp.float32),
        scratch_shapes=[pltpu.VMEM((n + 16, HY_LANES), jnp.float32),
                        pltpu.VMEM((2, (n // HY_CHUNK) * HY_PITCH, HY_LANES), jnp.float32),
                        pltpu.VMEM((2, n1_full * HY_PITCH, HY_LANES), jnp.float32)],
        compiler_params=pltpu.CompilerParams(
            dimension_semantics=("parallel", "parallel"), vmem_limit_bytes=HY_VMEM_LIMIT),
        name="hyena_long_conv",
    )(z, g, short_w, short_b.reshape(1, -1), short_w, short_b.reshape(1, -1), skip.reshape(1, -1), hr, hi,
      fwd_a, inv_a, fwd_b, inv_b)


def hyena_mixer(p, short_w, short_b, skip, hr, hi, tables):
    z1 = hyena_long_conv(p, HY_OFF, p, HY_OFF + BRANCH_WIDTH, True, short_w, short_b, 0, BRANCH_WIDTH,
                         skip[0], hr[0], hi[0], tables)
    return hyena_long_conv(z1, 0, p, HY_OFF + 2 * BRANCH_WIDTH, False, short_w, short_b, 0, 2 * BRANCH_WIDTH,
                           skip[1], hr[1], hi[1], tables)


def merge_branches(p, y_attn, pool_w, pool_scale, hy_params, w_branch, w_out):
    b, n, _ = p.shape
    y_pool = multiscale_pool(p, pool_w, pool_scale)
    y_hy = hyena_mixer(p, *hy_params)
    gates = jax.nn.sigmoid(p[..., GATE_OFF:].astype(jnp.float32)).reshape(b, n, N_BRANCH, D_MODEL)
    ys = (y_attn.astype(p.dtype), y_pool, y_hy)
    merged = 0.0
    for i in range(N_BRANCH):
        br = matmul(ys[i].reshape(b * n, BRANCH_WIDTH), w_branch[i]).reshape(b, n, D_MODEL)
        merged = merged + gates[:, :, i] * br
    return matmul(merged.reshape(b * n, D_MODEL), w_out).reshape(b, n, D_MODEL)


def token_mixers(hl, hc, w_in, rpb, pool_w, pool_scale, hy_lat, hy_ctx, w_branch, w_out, cos, sin, need_ctx):
    b, s, d = hl.shape
    n_ctx = hc.shape[1]
    pl_ = matmul(hl.reshape(b * s, d), w_in).reshape(b, s, IN_WIDTH)
    pc = matmul(hc.reshape(b * n_ctx, d), w_in).reshape(b, n_ctx, IN_WIDTH)
    ql, kl, vl = [split_heads(t) for t in jnp.split(pl_[..., :POOL_OFF], 3, axis=-1)]
    qc, kc, vc = [split_heads(t) for t in jnp.split(pc[..., :POOL_OFF], 3, axis=-1)]
    bf = jnp.bfloat16
    ya_l = neighbourhood_attention(apply_rope(ql, cos, sin).astype(bf), ql.astype(bf), apply_rope(kl, cos, sin).astype(bf),
                                   vl.astype(bf), kc.astype(bf), vc.astype(bf), rpb)
    out_l = merge_branches(pl_, merge_heads(ya_l), pool_w, pool_scale, hy_lat, w_branch, w_out)
    out_c = None
    if need_ctx:
        ya_c = context_attention(qc.astype(bf), kc.astype(bf), vc.astype(bf))
        out_c = merge_branches(pc, merge_heads(ya_c), pool_w, pool_scale, hy_ctx, w_branch, w_out)
    return out_l, out_c


def kernel(x, c, ctx, c_ctx, norm1_g, norm2_g, w_mod, b_mod, w_in, rpb, pool_w, pool_scale, hy_short_w, hy_short_b, hy_w1, hy_b1, hy_freq, hy_w2, hy_b2, hy_w3, hy_skip, w_branch, w_out, w_router, router_bias, w_gate_e, w_up_e, w_down_e, final_g):
    b, s, d = x.shape
    n_ctx = ctx.shape[1]
    cos, sin = axial_rope_tables(s)
    tables_l = _hy_dft_tables(s)
    tables_c = _hy_dft_tables(n_ctx)
    spec_l = hyena_filter_spectra(s, hy_w1, hy_b1, hy_freq, hy_w2, hy_b2, hy_w3, tables_l)
    spec_c = hyena_filter_spectra(n_ctx, hy_w1[:DEPTH - 1], hy_b1[:DEPTH - 1], hy_freq[:DEPTH - 1], hy_w2[:DEPTH - 1],
                                  hy_b2[:DEPTH - 1], hy_w3[:DEPTH - 1], tables_c)
    xl, xc = x, ctx
    for l in range(DEPTH):
        need_ctx = l < DEPTH - 1
        ml = (jax.nn.silu(c) @ w_mod[l] + b_mod[l]).reshape(b, 1, N_MOD, d)
        mc = (jax.nn.silu(c_ctx) @ w_mod[l] + b_mod[l]).reshape(1, 1, N_MOD, d)
        hl = modulate(rms_norm(xl, norm1_g[l]), ml[:, :, 0], ml[:, :, 1])
        hc = modulate(rms_norm(xc, norm1_g[l]), mc[:, :, 0], mc[:, :, 1])
        hy_lat = (hy_short_w[l], hy_short_b[l], hy_skip[l], spec_l[0][l], spec_l[1][l], tables_l)
        hy_ctx = (hy_short_w[l], hy_short_b[l], hy_skip[l], spec_c[0][l], spec_c[1][l], tables_c)
        out_l, out_c = token_mixers(hl, hc, w_in[l], rpb[l], pool_w[l], pool_scale[l], hy_lat, hy_ctx,
                                    w_branch[l], w_out[l], cos, sin, need_ctx)
        xl = xl + ml[:, :, 2] * out_l
        hl2 = modulate(rms_norm(xl, norm2_g[l]), ml[:, :, 3], ml[:, :, 4])
        if need_ctx:
            xc = xc + mc[:, :, 2] * out_c
            hc2 = modulate(rms_norm(xc, norm2_g[l]), mc[:, :, 3], mc[:, :, 4])
            tokens = jnp.concatenate([hl2.reshape(b * s, d), hc2.reshape(b * n_ctx, d)], axis=0)
            f = grouped_moe(tokens, w_router, router_bias, w_gate_e[l], w_up_e[l], w_down_e[l])
            xl = xl + ml[:, :, 5] * f[:b * s].reshape(b, s, d)
            xc = xc + mc[:, :, 5] * f[b * s:].reshape(b, n_ctx, d)
        else:
            f = grouped_moe(hl2.reshape(b * s, d), w_router, router_bias, w_gate_e[l], w_up_e[l], w_down_e[l])
            xl = xl + ml[:, :, 5] * f.reshape(b, s, d)
    return rms_norm(xl, final_g)
```

```python
import functools
import math

import jax
import jax.numpy as jnp
from jax import lax
from jax.experimental import pallas as pl
from jax.experimental.pallas import tpu as pltpu

D_MODEL = 1024
DEPTH = 4
GRID_W = 64
BRANCH_WIDTH = D_MODEL // 2
N_BRANCH = 3
HEAD_DIM = 64
NA_HEADS = BRANCH_WIDTH // HEAD_DIM
NA_KH_MAX = 8
NA_KW = 16
ROPE_THETA = 10000.0
NEG_INF = -1e30
POOL_WINDOWS = (2, 4, 8, 16)
POOL_GROUPS = len(POOL_WINDOWS)
POOL_GROUP_DIM = BRANCH_WIDTH // POOL_GROUPS
HY_ORDER = 2
HY_SHORT = 3
HY_EMB_DIM = 33
HY_FILTER_HIDDEN = 64
HY_MIN_DECAY = math.log(1e-2) / 0.3
HY_MAX_DECAY = math.log(1e-2) / 1.5
N_EXPERTS = 16
N_GROUPS = 4
EXPERTS_PER_GROUP = N_EXPERTS // N_GROUPS
TOP_K = 2
EXPERT_FF = D_MODEL // 2
N_MOD = 6
RMS_EPS = 1e-6
POOL_OFF = 3 * BRANCH_WIDTH
HY_OFF = POOL_OFF + BRANCH_WIDTH
GATE_OFF = HY_OFF + (HY_ORDER + 1) * BRANCH_WIDTH
IN_WIDTH = GATE_OFF + N_BRANCH * D_MODEL

VMEM_LIMIT_BYTES = 48 * 1024 * 1024
MOE_TILE = 256


def _mm_kernel(a_ref, b_ref, o_ref):
    o_ref[...] = jnp.dot(a_ref[...].astype(jnp.bfloat16), b_ref[...].astype(jnp.bfloat16),
                         preferred_element_type=jnp.float32).astype(o_ref.dtype)


def _pick_tile(n, cands):
    for c in cands:
        if n % c == 0:
            return c
    return n


def matmul(a, b, out_dtype=jnp.float32):
    m, k = a.shape
    _, n = b.shape
    tm = _pick_tile(m, (1024, 512, 256, 128, 64, 32, 16, 8))
    tn = _pick_tile(n, (1664, 1024, 512, 256, 128))
    return pl.pallas_call(
        _mm_kernel,
        grid=(m // tm, n // tn),
        in_specs=[pl.BlockSpec((tm, k), lambda i, j: (i, 0)),
                  pl.BlockSpec((k, tn), lambda i, j: (0, j))],
        out_specs=pl.BlockSpec((tm, tn), lambda i, j: (i, j)),
        out_shape=jax.ShapeDtypeStruct((m, n), out_dtype),
        compiler_params=pltpu.CompilerParams(
            dimension_semantics=("parallel", "parallel"), vmem_limit_bytes=VMEM_LIMIT_BYTES),
        name="dense_matmul",
    )(a, b)


def _expert_kernel(src_ref, dst_ref, tile_expert_ref, n_valid_ref, h_hbm, wg_ref, wu_ref, wd_ref, y_hbm,
                   xbuf, obuf, zbuf, gsem, ssem, zsem):
    i = pl.program_id(0)
    n_valid = n_valid_ref[0]
    slot = i % 2

    def gather(tile, s):
        def row(r, carry):
            pltpu.make_async_copy(h_hbm.at[pl.ds(src_ref[tile * MOE_TILE + r], 1)],
                                  xbuf.at[s, pl.ds(r, 1)], gsem.at[s]).start()
            return carry
        lax.fori_loop(0, MOE_TILE, row, 0, unroll=8)

    def wait_rows(buf, sem, s):
        pltpu.make_async_copy(buf.at[s], buf.at[s], sem.at[s]).wait()

    @pl.when((i == 0) & (n_valid > 0))
    def _():
        gather(0, 0)

    @pl.when(i + 1 < n_valid)
    def _():
        gather(i + 1, 1 - slot)

    @pl.when(i < n_valid)
    def _():
        wait_rows(xbuf, gsem, slot)

        @pl.when(i >= 2)
        def _():
            wait_rows(obuf, ssem, slot)

        x = xbuf[slot].astype(jnp.bfloat16)
        g = jnp.dot(x, wg_ref[...].astype(jnp.bfloat16), preferred_element_type=jnp.float32)
        u = jnp.dot(x, wu_ref[...].astype(jnp.bfloat16), preferred_element_type=jnp.float32)
        hid = (g * jax.nn.sigmoid(g)) * u
        obuf[slot] = jnp.dot(hid.astype(jnp.bfloat16), wd_ref[...].astype(jnp.bfloat16),
                             preferred_element_type=jnp.float32)

        def row(r, carry):
            pltpu.make_async_copy(obuf.at[slot, pl.ds(r, 1)],
                                  y_hbm.at[pl.ds(dst_ref[i * MOE_TILE + r], 1)], ssem.at[slot]).start()
            return carry
        lax.fori_loop(0, MOE_TILE, row, 0, unroll=8)

    @pl.when(i >= n_valid)
    def _():
        zbuf[...] = jnp.zeros_like(zbuf)

        def row(r, carry):
            pltpu.make_async_copy(zbuf.at[pl.ds(r, 1)], y_hbm.at[pl.ds(dst_ref[i * MOE_TILE + r], 1)],
                                  zsem.at[0]).start()
            return carry
        lax.fori_loop(0, MOE_TILE, row, 0, unroll=8)
        pltpu.make_async_copy(zbuf, zbuf, zsem.at[0]).wait()

    @pl.when(i == pl.num_programs(0) - 1)
    def _():
        @pl.when(n_valid >= 1)
        def _():
            wait_rows(obuf, ssem, (n_valid - 1) % 2)

        @pl.when(n_valid >= 2)
        def _():
            wait_rows(obuf, ssem, n_valid % 2)


def expert_ffn(h, src_row, dst_row, tile_expert, n_valid, w_gate, w_up, w_down):
    _, d = h.shape
    p = src_row.shape[0]
    n_tiles = p // MOE_TILE
    grid_spec = pltpu.PrefetchScalarGridSpec(
        num_scalar_prefetch=4,
        grid=(n_tiles,),
        in_specs=[
            pl.BlockSpec(memory_space=pl.ANY),
            pl.BlockSpec((None, d, EXPERT_FF), lambda i, s, t, te, nv: (te[i], 0, 0)),
            pl.BlockSpec((None, d, EXPERT_FF), lambda i, s, t, te, nv: (te[i], 0, 0)),
            pl.BlockSpec((None, EXPERT_FF, d), lambda i, s, t, te, nv: (te[i], 0, 0)),
        ],
        out_specs=pl.BlockSpec(memory_space=pl.ANY),
        scratch_shapes=[pltpu.VMEM((2, MOE_TILE, d), jnp.float32), pltpu.VMEM((2, MOE_TILE, d), jnp.float32),
                        pltpu.VMEM((MOE_TILE, d), jnp.float32),
                        pltpu.SemaphoreType.DMA((2,)), pltpu.SemaphoreType.DMA((2,)), pltpu.SemaphoreType.DMA((1,))],
    )
    return pl.pallas_call(
        _expert_kernel,
        grid_spec=grid_spec,
        out_shape=jax.ShapeDtypeStruct((p, d), jnp.float32),
        compiler_params=pltpu.CompilerParams(
            dimension_semantics=("arbitrary",), vmem_limit_bytes=VMEM_LIMIT_BYTES),
        name="expert_ffn",
    )(src_row, dst_row, tile_expert, n_valid, h, w_gate, w_up, w_down)


def grouped_moe(h, w_router, router_bias, w_gate, w_up, w_down):
    t, d = h.shape
    aff = jax.nn.sigmoid(jnp.dot(h, w_router, precision=lax.Precision.HIGHEST).astype(jnp.float32))
    biased = (aff + router_bias.astype(jnp.float32)).reshape(-1, N_GROUPS, EXPERTS_PER_GROUP)
    group_score = jnp.sum(lax.top_k(biased, TOP_K)[0], axis=-1)
    group = jnp.argmax(group_score, axis=-1)
    group_hot = group[:, None] == jnp.arange(N_GROUPS)[None, :]
    in_group = jnp.sum(jnp.where(group_hot[:, :, None], biased, 0.0), axis=1)
    local = lax.top_k(in_group, TOP_K)[1]
    expert = (group[:, None] * EXPERTS_PER_GROUP + local).astype(jnp.int32)
    expert_hot = expert[:, :, None] == jnp.arange(N_EXPERTS, dtype=jnp.int32)[None, None, :]
    wsel = jnp.sum(jnp.where(expert_hot, aff[:, None, :], 0.0), axis=-1)
    wsel = wsel / jnp.sum(wsel, axis=-1, keepdims=True)

    flat_e = expert.reshape(-1)
    onehot = (flat_e[:, None] == jnp.arange(N_EXPERTS, dtype=jnp.int32)[None, :]).astype(jnp.int32)
    csum = jnp.cumsum(onehot, axis=0)
    rank = jnp.sum(csum * onehot, axis=1) - 1
    counts = csum[-1]
    tiles_per = (counts + MOE_TILE - 1) // MOE_TILE
    tile_end = jnp.cumsum(tiles_per)
    tile_start = tile_end - tiles_per
    pos = jnp.sum(onehot * tile_start[None, :], axis=1) * MOE_TILE + rank
    n_tiles = (t * TOP_K) // MOE_TILE + N_EXPERTS
    p = n_tiles * MOE_TILE
    n_assign = t * TOP_K
    assign_of = jnp.full((p,), -1, jnp.int32).at[pos].set(jnp.arange(n_assign, dtype=jnp.int32))
    is_pad = assign_of < 0
    src_row = jnp.where(is_pad, 0, assign_of // TOP_K)
    dst_row = jnp.where(is_pad, n_assign + jnp.cumsum(is_pad.astype(jnp.int32)) - 1, assign_of)
    tile_ids = jnp.arange(n_tiles, dtype=jnp.int32)
    n_valid = tile_end[-1].astype(jnp.int32)
    tile_expert = jnp.minimum(jnp.searchsorted(tile_end, tile_ids, side='right'), N_EXPERTS - 1).astype(jnp.int32)
    last_expert = tile_expert[jnp.maximum(n_valid - 1, 0)]
    tile_expert = jnp.where(tile_ids < n_valid, tile_expert, last_expert)

    y = expert_ffn(h, src_row, dst_row, tile_expert, n_valid.reshape(1), w_gate, w_up, w_down)
    y = y[:n_assign].reshape(t, TOP_K, d)
    return jnp.sum(y * wsel[:, :, None], axis=1)


def rms_norm(x, g):
    xf = x.astype(jnp.float32)
    y = xf * lax.rsqrt(jnp.mean(xf * xf, axis=-1, keepdims=True) + RMS_EPS)
    return (y * g.astype(jnp.float32)).astype(x.dtype)


def modulate(h, shift, scale):
    return h * (1.0 + scale) + shift


def split_heads(t):
    b, n, _ = t.shape
    return t.reshape(b, n, NA_HEADS, HEAD_DIM).transpose(0, 2, 1, 3)


def merge_heads(t):
    b, h, n, d = t.shape
    return t.transpose(0, 2, 1, 3).reshape(b, n, h * d)


def axial_rope_tables(n_tokens):
    t = jnp.arange(n_tokens)
    row = (t // GRID_W).astype(jnp.float32)
    col = (t % GRID_W).astype(jnp.float32)
    n_freq = HEAD_DIM // 4
    inv = ROPE_THETA ** (-jnp.arange(n_freq, dtype=jnp.float32) / n_freq)
    ang = jnp.concatenate([row[:, None] * inv, col[:, None] * inv], axis=-1)
    return jnp.cos(ang), jnp.sin(ang)


def apply_rope(x, cos, sin):
    xf = x.astype(jnp.float32)
    x1, x2 = jnp.split(xf, 2, axis=-1)
    return jnp.concatenate([x1 * cos - x2 * sin, x1 * sin + x2 * cos], axis=-1).astype(x.dtype)


def _natten_bias_patterns(rpb):
    j = jnp.arange(GRID_W)
    col_start = jnp.clip(j - NA_KW // 2, 0, GRID_W - NA_KW)
    col_mask = (j[None, :] >= col_start[:, None]) & (j[None, :] < col_start[:, None] + NA_KW)
    dc = jnp.clip(j[None, :] - j[:, None], -(NA_KW - 1), NA_KW - 1) + NA_KW - 1
    dc_hot = (dc[None, :, :] == jnp.arange(2 * NA_KW - 1)[:, None, None]).astype(jnp.float32)
    by_col = jnp.einsum('hrc,cqk->hrqk', rpb.astype(jnp.float32), dc_hot, precision=lax.Precision.HIGHEST)
    by_col = jnp.where(col_mask[None, None], by_col, NEG_INF)
    bias = jnp.stack([by_col[:, p:p + NA_KH_MAX] for p in range(NA_KH_MAX)], axis=1)
    bias = jnp.transpose(bias, (0, 1, 3, 2, 4))
    return bias.reshape(NA_HEADS, NA_KH_MAX, GRID_W, NA_KH_MAX * GRID_W)


def _natten_kernel(qr_ref, qp_ref, kr_ref, v_ref, kc_ref, vc_ref, bias_ref, o_ref):
    rows = qr_ref.shape[0] // GRID_W
    n_lat = NA_KH_MAX * GRID_W
    scale = HEAD_DIM ** -0.5
    kc = kc_ref[...]
    vc = vc_ref[...]
    nt = (((1,), (1,)), ((), ()))

    def body(r, carry):
        row_start = jnp.clip(r - NA_KH_MAX // 2, 0, rows - NA_KH_MAX)
        pattern = row_start - r + NA_KH_MAX - 1
        q0 = pl.multiple_of(r * GRID_W, GRID_W)
        k0 = pl.multiple_of(row_start * GRID_W, GRID_W)
        qr = qr_ref[pl.ds(q0, GRID_W), :]
        qp = qp_ref[pl.ds(q0, GRID_W), :]
        ks = kr_ref[pl.ds(k0, n_lat), :]
        vs = v_ref[pl.ds(k0, n_lat), :]
        s_lat = lax.dot_general(qr, ks, nt, preferred_element_type=jnp.float32) * scale + bias_ref[pattern]
        s_ctx = lax.dot_general(qp, kc, nt, preferred_element_type=jnp.float32) * scale
        m = jnp.maximum(jnp.max(s_lat, axis=-1, keepdims=True), jnp.max(s_ctx, axis=-1, keepdims=True))
        p_lat = jnp.exp(s_lat - m)
        p_ctx = jnp.exp(s_ctx - m)
        den = jnp.sum(p_lat, axis=-1, keepdims=True) + jnp.sum(p_ctx, axis=-1, keepdims=True)
        o = (jnp.dot(p_lat.astype(jnp.bfloat16), vs, preferred_element_type=jnp.float32)
             + jnp.dot(p_ctx.astype(jnp.bfloat16), vc, preferred_element_type=jnp.float32))
        o_ref[pl.ds(q0, GRID_W), :] = (o / den).astype(o_ref.dtype)
        return carry

    lax.fori_loop(0, rows, body, 0, unroll=2)


def neighbourhood_attention(q_rot, q_plain, k_rot, v, k_ctx, v_ctx, rpb):
    b, h, s, dh = q_rot.shape
    n_ctx = k_ctx.shape[2]
    bias = _natten_bias_patterns(rpb)
    lat = pl.BlockSpec((None, None, s, dh), lambda i, j: (i, j, 0, 0))
    cx = pl.BlockSpec((None, None, n_ctx, dh), lambda i, j: (i, j, 0, 0))
    return pl.pallas_call(
        _natten_kernel,
        grid=(b, h),
        in_specs=[lat, lat, lat, lat, cx, cx,
                  pl.BlockSpec((None, NA_KH_MAX, GRID_W, NA_KH_MAX * GRID_W), lambda i, j: (j, 0, 0, 0))],
        out_specs=pl.BlockSpec((None, None, s, dh), lambda i, j: (i, j, 0, 0)),
        out_shape=jax.ShapeDtypeStruct((b, h, s, dh), jnp.float32),
        compiler_params=pltpu.CompilerParams(
            dimension_semantics=("parallel", "parallel"), vmem_limit_bytes=VMEM_LIMIT_BYTES),
        name="natten",
    )(q_rot, q_plain, k_rot, v, k_ctx, v_ctx, bias)


def _ctx_attn_kernel(q_ref, k_ref, v_ref, o_ref):
    nt = (((1,), (1,)), ((), ()))
    s = lax.dot_general(q_ref[...], k_ref[...], nt, preferred_element_type=jnp.float32) * (HEAD_DIM ** -0.5)
    p = jnp.exp(s - jnp.max(s, axis=-1, keepdims=True))
    den = jnp.sum(p, axis=-1, keepdims=True)
    o = jnp.dot(p.astype(jnp.bfloat16), v_ref[...], preferred_element_type=jnp.float32)
    o_ref[...] = (o / den).astype(o_ref.dtype)


def context_attention(q, k, v):
    b, h, n, dh = q.shape
    spec = pl.BlockSpec((None, None, n, dh), lambda i, j: (i, j, 0, 0))
    return pl.pallas_call(
        _ctx_attn_kernel,
        grid=(b, h),
        in_specs=[spec, spec, spec],
        out_specs=spec,
        out_shape=jax.ShapeDtypeStruct((b, h, n, dh), jnp.float32),
        compiler_params=pltpu.CompilerParams(dimension_semantics=("parallel", "parallel")),
        name="ctx_attention",
    )(q, k, v)


POOL_PAD = 16


def _pool_kernel(u_ref, w_ref, scale_ref, o_ref, pad_ref):
    n = u_ref.shape[0]
    u = u_ref[...]
    pad_ref[pl.ds(0, POOL_PAD), :] = jnp.zeros((POOL_PAD, POOL_GROUP_DIM), jnp.float32)
    pad_ref[pl.ds(POOL_PAD + n, POOL_PAD), :] = jnp.zeros((POOL_PAD, POOL_GROUP_DIM), jnp.float32)
    pad_ref[pl.ds(POOL_PAD, n), :] = u
    t = lax.broadcasted_iota(jnp.int32, (n, 1), 0)
    for g, w in enumerate(POOL_WINDOWS):
        @pl.when(pl.program_id(1) == g)
        def _(w=w):
            first = POOL_PAD - w // 2
            acc = pad_ref[pl.ds(first, n), :]
            for k in range(1, w):
                acc = acc + pad_ref[pl.ds(first + k, n), :]
            cnt = jnp.minimum(t + (w - w // 2), n) - jnp.maximum(t - w // 2, 0)
            y = acc / cnt.astype(jnp.float32) - u
            y = jnp.dot(y.astype(jnp.bfloat16), w_ref[...].astype(jnp.bfloat16), preferred_element_type=jnp.float32)
            o_ref[...] = y * scale_ref[...]


def multiscale_pool(p, pool_w, pool_scale):
    b, n, _ = p.shape
    col0 = POOL_OFF // POOL_GROUP_DIM
    return pl.pallas_call(
        _pool_kernel,
        grid=(b, POOL_GROUPS),
        in_specs=[pl.BlockSpec((None, n, POOL_GROUP_DIM), lambda i, g: (i, 0, col0 + g)),
                  pl.BlockSpec((None, POOL_GROUP_DIM, POOL_GROUP_DIM), lambda i, g: (g, 0, 0)),
                  pl.BlockSpec((1, POOL_GROUP_DIM), lambda i, g: (0, g))],
        out_specs=pl.BlockSpec((None, n, POOL_GROUP_DIM), lambda i, g: (i, 0, g)),
        out_shape=jax.ShapeDtypeStruct((b, n, BRANCH_WIDTH), jnp.float32),
        scratch_shapes=[pltpu.VMEM((n + 2 * POOL_PAD, POOL_GROUP_DIM), jnp.float32)],
        compiler_params=pltpu.CompilerParams(
            dimension_semantics=("parallel", "parallel"), vmem_limit_bytes=VMEM_LIMIT_BYTES),
        name="multiscale_pool",
    )(p, pool_w, pool_scale.reshape(1, BRANCH_WIDTH))


HY_CHUNK = 64
HY_PITCH = 72
HY_LANES = 128
HY_FEAT = 128
HY_VMEM_LIMIT = 56 * 1024 * 1024


def _cis_tables(num, den):
    ang = (-2.0 * math.pi / den) * (num % den).astype(jnp.float32)
    return jnp.cos(ang), jnp.sin(ang)


def _stack_complex(mr, mi):
    return jnp.concatenate([jnp.concatenate([mr, -mi], axis=-1), jnp.concatenate([mi, mr], axis=-1)], axis=-2)


def _hy_dft_tables(n):
    big = 2 * n
    n1_full = big // HY_CHUNK
    k1 = jnp.arange(n1_full, dtype=jnp.int32)
    n1 = jnp.arange(n1_full, dtype=jnp.int32)
    n2 = jnp.arange(HY_CHUNK, dtype=jnp.int32)
    mr, mi = _cis_tables(k1[None, :, None] * (HY_CHUNK * n1[None, None, :] + n2[:, None, None]), big)
    half = n1_full // 2
    fwd_a = _stack_complex(mr[:, :, :half], mi[:, :, :half])
    fwd_a_real = jnp.concatenate([mr, mi], axis=-2)
    mrt = jnp.transpose(mr[:, :, :half], (0, 2, 1))
    mit = -jnp.transpose(mi[:, :, :half], (0, 2, 1))
    inv_a = _stack_complex(mrt, mit)
    er, ei = _cis_tables(n2[:, None] * n2[None, :], HY_CHUNK)
    fwd_b = _stack_complex(er, ei)
    inv_b = _stack_complex(er, -ei)
    bf = jnp.bfloat16
    return fwd_a.astype(bf), fwd_a_real.astype(bf), inv_a.astype(bf), fwd_b.astype(bf), inv_b.astype(bf)


def _hy_positions(n):
    f32 = jnp.float32
    t = jnp.linspace(0.0, 1.0, n, dtype=f32)[:, None]
    bands = (HY_EMB_DIM - 1) // 2
    ang = (2.0 * math.pi / n) * jnp.arange(n, dtype=f32)[:, None]
    f = jnp.linspace(1e-4, bands - 1, bands, dtype=f32)[None, :]
    z = jnp.concatenate([t, jnp.cos(f * ang), -jnp.sin(f * ang)], axis=-1)
    z = jnp.pad(z, ((0, 0), (0, HY_FEAT - HY_EMB_DIM)))
    zb = jnp.concatenate([z[:1], z[:0:-1]], axis=0)
    return z, zb


def _filter_mlp(z, w1, b1, fr, w2, b2):
    hp = lax.Precision.HIGHEST
    h = jnp.sin(fr * (jnp.dot(z, w1, precision=hp, preferred_element_type=jnp.float32) + b1))
    return jnp.sin(fr * (jnp.dot(h, w2, precision=hp, preferred_element_type=jnp.float32) + b2))


def _filter_taps(zf_ref, zb_ref, w1_ref, b1_ref, fr_ref, w2_ref, b2_ref, w3f_ref, w3b_ref, delta_ref, hid_ref):
    n = zf_ref.shape[0]
    hp = lax.Precision.HIGHEST

    @pl.when((pl.program_id(1) == 0) & (pl.program_id(2) == 0))
    def _():
        args = (w1_ref[...], b1_ref[...], fr_ref[...], w2_ref[...], b2_ref[...])
        hid_ref[0] = _filter_mlp(zf_ref[...], *args)
        hid_ref[1] = _filter_mlp(zb_ref[...], *args)

    delta = delta_ref[...]
    hf = jnp.dot(hid_ref[0], w3f_ref[...], precision=hp, preferred_element_type=jnp.float32)
    hf = hf * jnp.exp(-zf_ref[:, 0:1] * delta)
    hb = jnp.dot(hid_ref[1], w3b_ref[...], precision=hp, preferred_element_type=jnp.float32)
    hb = hb * jnp.exp(-zb_ref[:, 0:1] * delta)
    row = lax.broadcasted_iota(jnp.int32, (n, 1), 0)
    hb = jnp.where(row > 0, hb, 0.0)
    norm = jnp.sum(jnp.abs(hf), axis=0, keepdims=True) + jnp.sum(jnp.abs(hb), axis=0, keepdims=True)
    return hf / norm, hb / norm


def _filter_spec_kernel(zf_ref, zb_ref, w1_ref, b1_ref, fr_ref, w2_ref, b2_ref, w3f_ref, w3b_ref, delta_ref,
                        fa_ref, fb_ref, hr_ref, hi_ref, hid_ref, tbuf, sbuf):
    n = zf_ref.shape[0]
    chunks = n // HY_CHUNK
    n1_full = 2 * chunks
    hf, hb = _filter_taps(zf_ref, zb_ref, w1_ref, b1_ref, fr_ref, w2_ref, b2_ref, w3f_ref, w3b_ref, delta_ref, hid_ref)
    for c in range(chunks):
        tbuf[pl.ds(c * HY_PITCH, HY_CHUNK), :] = hf[c * HY_CHUNK:(c + 1) * HY_CHUNK]
        tbuf[pl.ds((chunks + c) * HY_PITCH, HY_CHUNK), :] = hb[c * HY_CHUNK:(c + 1) * HY_CHUNK]

    def stage_a(n2, carry):
        rows = tbuf[pl.ds(n2, n1_full, stride=HY_PITCH), :].astype(jnp.bfloat16)
        a = jnp.dot(fa_ref[n2], rows, preferred_element_type=jnp.float32)
        sbuf[0, pl.ds(n2, n1_full, stride=HY_PITCH), :] = a[:n1_full]
        sbuf[1, pl.ds(n2, n1_full, stride=HY_PITCH), :] = a[n1_full:]
        return carry

    lax.fori_loop(0, HY_CHUNK, stage_a, 0, unroll=2)
    inv_len = 1.0 / (2 * n)

    def stage_b(k1, carry):
        r0 = pl.multiple_of(k1 * HY_PITCH, 8)
        x = jnp.concatenate([sbuf[0, pl.ds(r0, HY_CHUNK), :], sbuf[1, pl.ds(r0, HY_CHUNK), :]], axis=0)
        y = jnp.dot(fb_ref[...], x.astype(jnp.bfloat16), preferred_element_type=jnp.float32) * inv_len
        o0 = pl.multiple_of(k1 * HY_CHUNK, HY_CHUNK)
        hr_ref[pl.ds(o0, HY_CHUNK), :] = y[:HY_CHUNK]
        hi_ref[pl.ds(o0, HY_CHUNK), :] = y[HY_CHUNK:]
        return carry

    lax.fori_loop(0, n1_full, stage_b, 0, unroll=4)


def hyena_filter_spectra(n, hy_w1, hy_b1, hy_freq, hy_w2, hy_b2, hy_w3, tables):
    depth = hy_w1.shape[0]
    zf, zb = _hy_positions(n)
    tiles = BRANCH_WIDTH // HY_LANES
    w1 = jnp.pad(hy_w1, ((0, 0), (0, HY_FEAT - HY_EMB_DIM), (0, 0)))
    deltas = jnp.abs(jnp.linspace(HY_MIN_DECAY, HY_MAX_DECAY, BRANCH_WIDTH, dtype=jnp.float32)).reshape(1, BRANCH_WIDTH)
    fwd_a_real, fwd_b = tables[1], tables[3]
    n1_full = 2 * n // HY_CHUNK
    one = pl.Buffered(1)
    full = lambda shape: pl.BlockSpec(shape, lambda l, o, j: (0,) * len(shape), pipeline_mode=one)
    per_layer = lambda shape: pl.BlockSpec((None,) + shape, lambda l, o, j: (l,) + (0,) * len(shape))
    out_spec = pl.BlockSpec((None, None, 2 * n, HY_LANES), lambda l, o, j: (l, o, 0, j), pipeline_mode=one)
    out_sds = jax.ShapeDtypeStruct((depth, HY_ORDER, 2 * n, BRANCH_WIDTH), jnp.float32)
    return pl.pallas_call(
        _filter_spec_kernel,
        grid=(depth, HY_ORDER, tiles),
        in_specs=[full((n, HY_FEAT)), full((n, HY_FEAT)),
                  per_layer((HY_FEAT, HY_FILTER_HIDDEN)), per_layer((1, HY_FILTER_HIDDEN)),
                  per_layer((1, HY_FILTER_HIDDEN)),
                  per_layer((HY_FILTER_HIDDEN, HY_FILTER_HIDDEN)), per_layer((1, HY_FILTER_HIDDEN)),
                  pl.BlockSpec((None, HY_FILTER_HIDDEN, HY_LANES), lambda l, o, j: (l, 0, o * 2 * tiles + j)),
                  pl.BlockSpec((None, HY_FILTER_HIDDEN, HY_LANES), lambda l, o, j: (l, 0, o * 2 * tiles + tiles + j)),
                  pl.BlockSpec((1, HY_LANES), lambda l, o, j: (0, j)),
                  full((HY_CHUNK, 2 * n1_full, n1_full)), full((2 * HY_CHUNK, 2 * HY_CHUNK))],
        out_specs=[out_spec, out_spec],
        out_shape=[out_sds, out_sds],
        scratch_shapes=[pltpu.VMEM((2, n, HY_FILTER_HIDDEN), jnp.float32),
                        pltpu.VMEM((n1_full * HY_PITCH, HY_LANES), jnp.float32),
                        pltpu.VMEM((2, n1_full * HY_PITCH, HY_LANES), jnp.float32)],
        compiler_params=pltpu.CompilerParams(
            dimension_semantics=("arbitrary", "arbitrary", "arbitrary"), vmem_limit_bytes=HY_VMEM_LIMIT),
        name="hyena_filter_spectra",
    )(zf, zb, w1, hy_b1[:, None, :], hy_freq[:, None, :], hy_w2, hy_b2[:, None, :], hy_w3, hy_w3, deltas,
      fwd_a_real, fwd_b)


def _short_conv(u_ref, b, w_ref, bias_ref, pad_ref):
    n = u_ref.shape[1]
    x = u_ref[b]
    pad_ref[pl.ds(0, 8), :] = jnp.zeros((8, HY_LANES), jnp.float32)
    pad_ref[pl.ds(8 + n, 8), :] = jnp.zeros((8, HY_LANES), jnp.float32)
    pad_ref[pl.ds(8, n), :] = x
    return (pad_ref[pl.ds(7, n), :] * w_ref[0:1, :] + x * w_ref[1:2, :] + pad_ref[pl.ds(9, n), :] * w_ref[2:3, :]
            + bias_ref[...])


def _hyena_conv_kernel(z_ref, g_ref, wz_ref, bz_ref, wg_ref, bg_ref, skip_ref, hr_ref, hi_ref,
                       fa_ref, ia_ref, fb_ref, ib_ref, o_ref, pad_ref, zbuf, sbuf, *, conv_z):
    n = z_ref.shape[1]
    chunks = n // HY_CHUNK
    n1_full = 2 * chunks
    for b in range(2):
        z = _short_conv(z_ref, b, wz_ref, bz_ref, pad_ref) if conv_z else z_ref[b]
        for c in range(chunks):
            zbuf[b, pl.ds(c * HY_PITCH, HY_CHUNK), :] = z[c * HY_CHUNK:(c + 1) * HY_CHUNK]

    def stage_a(n2, carry):
        x = jnp.concatenate([zbuf[0, pl.ds(n2, chunks, stride=HY_PITCH), :],
                             zbuf[1, pl.ds(n2, chunks, stride=HY_PITCH), :]], axis=0)
        a = jnp.dot(fa_ref[n2], x.astype(jnp.bfloat16), preferred_element_type=jnp.float32)
        sbuf[0, pl.ds(n2, n1_full, stride=HY_PITCH), :] = a[:n1_full]
        sbuf[1, pl.ds(n2, n1_full, stride=HY_PITCH), :] = a[n1_full:]
        return carry

    lax.fori_loop(0, HY_CHUNK, stage_a, 0, unroll=2)

    def stage_b(k1, carry):
        r0 = pl.multiple_of(k1 * HY_PITCH, 8)
        x = jnp.concatenate([sbuf[0, pl.ds(r0, HY_CHUNK), :], sbuf[1, pl.ds(r0, HY_CHUNK), :]], axis=0)
        s = jnp.dot(fb_ref[...], x.astype(jnp.bfloat16), preferred_element_type=jnp.float32)
        h0 = pl.multiple_of(k1 * HY_CHUNK, HY_CHUNK)
        hr = hr_ref[pl.ds(h0, HY_CHUNK), :]
        hi = hi_ref[pl.ds(h0, HY_CHUNK), :]
        sr, si = s[:HY_CHUNK], s[HY_CHUNK:]
        y = jnp.concatenate([sr * hr - si * hi, sr * hi + si * hr], axis=0)
        c = jnp.dot(ib_ref[...], y.astype(jnp.bfloat16), preferred_element_type=jnp.float32)
        sbuf[0, pl.ds(r0, HY_CHUNK), :] = c[:HY_CHUNK]
        sbuf[1, pl.ds(r0, HY_CHUNK), :] = c[HY_CHUNK:]
        return carry

    lax.fori_loop(0, n1_full, stage_b, 0, unroll=8)
    skip = skip_ref[...]

    def stage_c(n2, carry):
        x = jnp.concatenate([sbuf[0, pl.ds(n2, n1_full, stride=HY_PITCH), :],
                             sbuf[1, pl.ds(n2, n1_full, stride=HY_PITCH), :]], axis=0)
        y = jnp.dot(ia_ref[n2], x.astype(jnp.bfloat16), preferred_element_type=jnp.float32)
        for b in range(2):
            zb = zbuf[b, pl.ds(n2, chunks, stride=HY_PITCH), :]
            zbuf[b, pl.ds(n2, chunks, stride=HY_PITCH), :] = y[b * chunks:(b + 1) * chunks] + zb * skip
        return carry

    lax.fori_loop(0, HY_CHUNK, stage_c, 0, unroll=4)
    for b in range(2):
        g = _short_conv(g_ref, b, wg_ref, bg_ref, pad_ref)
        for c in range(chunks):
            o_ref[b, pl.ds(c * HY_CHUNK, HY_CHUNK), :] = (
                g[c * HY_CHUNK:(c + 1) * HY_CHUNK] * zbuf[b, pl.ds(c * HY_PITCH, HY_CHUNK), :])


def hyena_long_conv(z, z_col0, g, g_col0, conv_z, short_w, short_b, w_off_z, w_off_g, skip, hr, hi, tables):
    b, n, _ = g.shape
    tiles = BRANCH_WIDTH // HY_LANES
    fwd_a, _, inv_a, fwd_b, inv_b = tables
    n1_full = 2 * n // HY_CHUNK
    one = pl.Buffered(1)
    zc, gc = z_col0 // HY_LANES, g_col0 // HY_LANES
    wz, wg = w_off_z // HY_LANES, w_off_g // HY_LANES
    const = lambda shape: pl.BlockSpec(shape, lambda j, p: (0,) * len(shape), pipeline_mode=one)
    return pl.pallas_call(
        functools.partial(_hyena_conv_kernel, conv_z=conv_z),
        grid=(tiles, b // 2),
        in_specs=[pl.BlockSpec((2, n, HY_LANES), lambda j, p: (p, 0, zc + j), pipeline_mode=one),
                  pl.BlockSpec((2, n, HY_LANES), lambda j, p: (p, 0, gc + j), pipeline_mode=one),
                  pl.BlockSpec((HY_SHORT, HY_LANES), lambda j, p: (0, wz + j)),
                  pl.BlockSpec((1, HY_LANES), lambda j, p: (0, wz + j)),
                  pl.BlockSpec((HY_SHORT, HY_LANES), lambda j, p: (0, wg + j)),
                  pl.BlockSpec((1, HY_LANES), lambda j, p: (0, wg + j)),
                  pl.BlockSpec((1, HY_LANES), lambda j, p: (0, j)),
                  pl.BlockSpec((2 * n, HY_LANES), lambda j, p: (0, j), pipeline_mode=one),
                  pl.BlockSpec((2 * n, HY_LANES), lambda j, p: (0, j), pipeline_mode=one),
                  const((HY_CHUNK, 2 * n1_full, n1_full)), const((HY_CHUNK, n1_full, 2 * n1_full)),
                  const((2 * HY_CHUNK, 2 * HY_CHUNK)), const((2 * HY_CHUNK, 2 * HY_CHUNK))],
        out_specs=pl.BlockSpec((2, n, HY_LANES), lambda j, p: (p, 0, j), pipeline_mode=one),
        out_shape=jax.ShapeDtypeStruct((b, n, BRANCH_WIDTH), jnp.float32),
        scratch_shapes=[pltpu.VMEM((n + 16, HY_LANES), jnp.float32),
                        pltpu.VMEM((2, (n // HY_CHUNK) * HY_PITCH, HY_LANES), jnp.float32),
                        pltpu.VMEM((2, n1_full * HY_PITCH, HY_LANES), jnp.float32)],
        compiler_params=pltpu.CompilerParams(
            dimension_semantics=("parallel", "parallel"), vmem_limit_bytes=HY_VMEM_LIMIT),
        name="hyena_long_conv",
    )(z, g, short_w, short_b.reshape(1, -1), short_w, short_b.reshape(1, -1), skip.reshape(1, -1), hr, hi,
      fwd_a, inv_a, fwd_b, inv_b)


def hyena_mixer(p, short_w, short_b, skip, hr, hi, tables):
    z1 = hyena_long_conv(p, HY_OFF, p, HY_OFF + BRANCH_WIDTH, True, short_w, short_b, 0, BRANCH_WIDTH,
                         skip[0], hr[0], hi[0], tables)
    return hyena_long_conv(z1, 0, p, HY_OFF + 2 * BRANCH_WIDTH, False, short_w, short_b, 0, 2 * BRANCH_WIDTH,
                           skip[1], hr[1], hi[1], tables)


def merge_branches(p, y_attn, pool_w, pool_scale, hy_params, w_branch, w_out):
    b, n, _ = p.shape
    y_pool = multiscale_pool(p, pool_w, pool_scale)
    y_hy = hyena_mixer(p, *hy_params)
    gates = jax.nn.sigmoid(p[..., GATE_OFF:].astype(jnp.float32)).reshape(b, n, N_BRANCH, D_MODEL)
    ys = (y_attn.astype(p.dtype), y_pool, y_hy)
    merged = 0.0
    for i in range(N_BRANCH):
        br = matmul(ys[i].reshape(b * n, BRANCH_WIDTH), w_branch[i]).reshape(b, n, D_MODEL)
        merged = merged + gates[:, :, i] * br
    return matmul(merged.reshape(b * n, D_MODEL), w_out).reshape(b, n, D_MODEL)


def token_mixers(hl, hc, w_in, rpb, pool_w, pool_scale, hy_lat, hy_ctx, w_branch, w_out, cos, sin, need_ctx):
    b, s, d = hl.shape
    n_ctx = hc.shape[1]
    pl_ = matmul(hl.reshape(b * s, d), w_in).reshape(b, s, IN_WIDTH)
    pc = matmul(hc.reshape(b * n_ctx, d), w_in).reshape(b, n_ctx, IN_WIDTH)
    ql, kl, vl = [split_heads(t) for t in jnp.split(pl_[..., :POOL_OFF], 3, axis=-1)]
    qc, kc, vc = [split_heads(t) for t in jnp.split(pc[..., :POOL_OFF], 3, axis=-1)]
    bf = jnp.bfloat16
    ya_l = neighbourhood_attention(apply_rope(ql, cos, sin).astype(bf), ql.astype(bf), apply_rope(kl, cos, sin).astype(bf),
                                   vl.astype(bf), kc.astype(bf), vc.astype(bf), rpb)
    out_l = merge_branches(pl_, merge_heads(ya_l), pool_w, pool_scale, hy_lat, w_branch, w_out)
    out_c = None
    if need_ctx:
        ya_c = context_attention(qc.astype(bf), kc.astype(bf), vc.astype(bf))
        out_c = merge_branches(pc, merge_heads(ya_c), pool_w, pool_scale, hy_ctx, w_branch, w_out)
    return out_l, out_c


def kernel(x, c, ctx, c_ctx, norm1_g, norm2_g, w_mod, b_mod, w_in, rpb, pool_w, pool_scale, hy_short_w, hy_short_b, hy_w1, hy_b1, hy_freq, hy_w2, hy_b2, hy_w3, hy_skip, w_branch, w_out, w_router, router_bias, w_gate_e, w_up_e, w_down_e, final_g):
    b, s, d = x.shape
    n_ctx = ctx.shape[1]
    cos, sin = axial_rope_tables(s)
    tables_l = _hy_dft_tables(s)
    tables_c = _hy_dft_tables(n_ctx)
    spec_l = hyena_filter_spectra(s, hy_w1, hy_b1, hy_freq, hy_w2, hy_b2, hy_w3, tables_l)
    spec_c = hyena_filter_spectra(n_ctx, hy_w1[:DEPTH - 1], hy_b1[:DEPTH - 1], hy_freq[:DEPTH - 1], hy_w2[:DEPTH - 1],
                                  hy_b2[:DEPTH - 1], hy_w3[:DEPTH - 1], tables_c)
    xl, xc = x, ctx
    for l in range(DEPTH):
        need_ctx = l < DEPTH - 1
        ml = (jax.nn.silu(c) @ w_mod[l] + b_mod[l]).reshape(b, 1, N_MOD, d)
        mc = (jax.nn.silu(c_ctx) @ w_mod[l] + b_mod[l]).reshape(1, 1, N_MOD, d)
        hl = modulate(rms_norm(xl, norm1_g[l]), ml[:, :, 0], ml[:, :, 1])
        hc = modulate(rms_norm(xc, norm1_g[l]), mc[:, :, 0], mc[:, :, 1])
        hy_lat = (hy_short_w[l], hy_short_b[l], hy_skip[l], spec_l[0][l], spec_l[1][l], tables_l)
        hy_ctx = (hy_short_w[l], hy_short_b[l], hy_skip[l], spec_c[0][l], spec_c[1][l], tables_c)
        out_l, out_c = token_mixers(hl, hc, w_in[l], rpb[l], pool_w[l], pool_scale[l], hy_lat, hy_ctx,
                                    w_branch[l], w_out[l], cos, sin, need_ctx)
        xl = xl + ml[:, :, 2] * out_l
        hl2 = modulate(rms_norm(xl, norm2_g[l]), ml[:, :, 3], ml[:, :, 4])
        if need_ctx:
            xc = xc + mc[:, :, 2] * out_c
            hc2 = modulate(rms_norm(xc, norm2_g[l]), mc[:, :, 3], mc[:, :, 4])
            tokens = jnp.concatenate([hl2.reshape(b * s, d), hc2.reshape(b * n_ctx, d)], axis=0)
            f = grouped_moe(tokens, w_router, router_bias, w_gate_e[l], w_up_e[l], w_down_e[l])
            xl = xl + ml[:, :, 5] * f[:b * s].reshape(b, s, d)
            xc = xc + mc[:, :, 5] * f[b * s:].reshape(b, n_ctx, d)
        else:
            f = grouped_moe(hl2.reshape(b * s, d), w_router, router_bias, w_gate_e[l], w_up_e[l], w_down_e[l])
            xl = xl + ml[:, :, 5] * f.reshape(b, s, d)
    return rms_norm(xl, final_g)
```

```python
import functools
import math

import jax
import jax.numpy as jnp
from jax import lax
from jax.experimental import pallas as pl
from jax.experimental.pallas import tpu as pltpu

D_MODEL = 1024
DEPTH = 4
GRID_W = 64
BRANCH_WIDTH = D_MODEL // 2
N_BRANCH = 3
HEAD_DIM = 64
NA_HEADS = BRANCH_WIDTH // HEAD_DIM
NA_KH_MAX = 8
NA_KW = 16
ROPE_THETA = 10000.0
NEG_INF = -1e30
POOL_WINDOWS = (2, 4, 8, 16)
POOL_GROUPS = len(POOL_WINDOWS)
POOL_GROUP_DIM = BRANCH_WIDTH // POOL_GROUPS
HY_ORDER = 2
HY_SHORT = 3
HY_EMB_DIM = 33
HY_FILTER_HIDDEN = 64
HY_MIN_DECAY = math.log(1e-2) / 0.3
HY_MAX_DECAY = math.log(1e-2) / 1.5
N_EXPERTS = 16
N_GROUPS = 4
EXPERTS_PER_GROUP = N_EXPERTS // N_GROUPS
TOP_K = 2
EXPERT_FF = D_MODEL // 2
N_MOD = 6
RMS_EPS = 1e-6
POOL_OFF = 3 * BRANCH_WIDTH
HY_OFF = POOL_OFF + BRANCH_WIDTH
GATE_OFF = HY_OFF + (HY_ORDER + 1) * BRANCH_WIDTH
IN_WIDTH = GATE_OFF + N_BRANCH * D_MODEL

VMEM_LIMIT_BYTES = 48 * 1024 * 1024
MOE_TILE = 256


def _mm_kernel(a_ref, b_ref, o_ref):
    o_ref[...] = jnp.dot(a_ref[...].astype(jnp.bfloat16), b_ref[...].astype(jnp.bfloat16),
                         preferred_element_type=jnp.float32).astype(o_ref.dtype)


def _pick_tile(n, cands):
    for c in cands:
        if n % c == 0:
            return c
    return n


def matmul(a, b, out_dtype=jnp.float32):
    m, k = a.shape
    _, n = b.shape
    tm = _pick_tile(m, (1024, 512, 256, 128, 64, 32, 16, 8))
    tn = _pick_tile(n, (1664, 1024, 512, 256, 128))
    return pl.pallas_call(
        _mm_kernel,
        grid=(m // tm, n // tn),
        in_specs=[pl.BlockSpec((tm, k), lambda i, j: (i, 0)),
                  pl.BlockSpec((k, tn), lambda i, j: (0, j))],
        out_specs=pl.BlockSpec((tm, tn), lambda i, j: (i, j)),
        out_shape=jax.ShapeDtypeStruct((m, n), out_dtype),
        compiler_params=pltpu.CompilerParams(
            dimension_semantics=("parallel", "parallel"), vmem_limit_bytes=VMEM_LIMIT_BYTES),
        name="dense_matmul",
    )(a, b)


def _expert_kernel(src_ref, dst_ref, tile_expert_ref, n_valid_ref, h_hbm, wg_ref, wu_ref, wd_ref, y_hbm,
                   xbuf, obuf, zbuf, gsem, ssem, zsem):
    i = pl.program_id(0)
    n_valid = n_valid_ref[0]
    slot = i % 2

    def gather(tile, s):
        def row(r, carry):
            pltpu.make_async_copy(h_hbm.at[pl.ds(src_ref[tile * MOE_TILE + r], 1)],
                                  xbuf.at[s, pl.ds(r, 1)], gsem.at[s]).start()
            return carry
        lax.fori_loop(0, MOE_TILE, row, 0, unroll=8)

    def wait_rows(buf, sem, s):
        pltpu.make_async_copy(buf.at[s], buf.at[s], sem.at[s]).wait()

    @pl.when((i == 0) & (n_valid > 0))
    def _():
        gather(0, 0)

    @pl.when(i + 1 < n_valid)
    def _():
        gather(i + 1, 1 - slot)

    @pl.when(i < n_valid)
    def _():
        wait_rows(xbuf, gsem, slot)

        @pl.when(i >= 2)
        def _():
            wait_rows(obuf, ssem, slot)

        x = xbuf[slot].astype(jnp.bfloat16)
        g = jnp.dot(x, wg_ref[...].astype(jnp.bfloat16), preferred_element_type=jnp.float32)
        u = jnp.dot(x, wu_ref[...].astype(jnp.bfloat16), preferred_element_type=jnp.float32)
        hid = (g * jax.nn.sigmoid(g)) * u
        obuf[slot] = jnp.dot(hid.astype(jnp.bfloat16), wd_ref[...].astype(jnp.bfloat16),
                             preferred_element_type=jnp.float32)

        def row(r, carry):
            pltpu.make_async_copy(obuf.at[slot, pl.ds(r, 1)],
                                  y_hbm.at[pl.ds(dst_ref[i * MOE_TILE + r], 1)], ssem.at[slot]).start()
            return carry
        lax.fori_loop(0, MOE_TILE, row, 0, unroll=8)

    @pl.when(i >= n_valid)
    def _():
        zbuf[...] = jnp.zeros_like(zbuf)

        def row(r, carry):
            pltpu.make_async_copy(zbuf.at[pl.ds(r, 1)], y_hbm.at[pl.ds(dst_ref[i * MOE_TILE + r], 1)],
                                  zsem.at[0]).start()
            return carry
        lax.fori_loop(0, MOE_TILE, row, 0, unroll=8)
        pltpu.make_async_copy(zbuf, zbuf, zsem.at[0]).wait()

    @pl.when(i == pl.num_programs(0) - 1)
    def _():
        @pl.when(n_valid >= 1)
        def _():
            wait_rows(obuf, ssem, (n_valid - 1) % 2)

        @pl.when(n_valid >= 2)
        def _():
            wait_rows(obuf, ssem, n_valid % 2)


def expert_ffn(h, src_row, dst_row, tile_expert, n_valid, w_gate, w_up, w_down):
    _, d = h.shape
    p = src_row.shape[0]
    n_tiles = p // MOE_TILE
    grid_spec = pltpu.PrefetchScalarGridSpec(
        num_scalar_prefetch=4,
        grid=(n_tiles,),
        in_specs=[
            pl.BlockSpec(memory_space=pl.ANY),
            pl.BlockSpec((None, d, EXPERT_FF), lambda i, s, t, te, nv: (te[i], 0, 0)),
            pl.BlockSpec((None, d, EXPERT_FF), lambda i, s, t, te, nv: (te[i], 0, 0)),
            pl.BlockSpec((None, EXPERT_FF, d), lambda i, s, t, te, nv: (te[i], 0, 0)),
        ],
        out_specs=pl.BlockSpec(memory_space=pl.ANY),
        scratch_shapes=[pltpu.VMEM((2, MOE_TILE, d), jnp.float32), pltpu.VMEM((2, MOE_TILE, d), jnp.float32),
                        pltpu.VMEM((MOE_TILE, d), jnp.float32),
                        pltpu.SemaphoreType.DMA((2,)), pltpu.SemaphoreType.DMA((2,)), pltpu.SemaphoreType.DMA((1,))],
    )
    return pl.pallas_call(
        _expert_kernel,
        grid_spec=grid_spec,
        out_shape=jax.ShapeDtypeStruct((p, d), jnp.float32),
        compiler_params=pltpu.CompilerParams(
            dimension_semantics=("arbitrary",), vmem_limit_bytes=VMEM_LIMIT_BYTES),
        name="expert_ffn",
    )(src_row, dst_row, tile_expert, n_valid, h, w_gate, w_up, w_down)


def grouped_moe(h, logits, router_bias, w_gate, w_up, w_down):
    t, d = h.shape
    aff = jax.nn.sigmoid(logits)
    biased = (aff + router_bias.astype(jnp.float32)).reshape(-1, N_GROUPS, EXPERTS_PER_GROUP)
    group_score = jnp.sum(lax.top_k(biased, TOP_K)[0], axis=-1)
    group = jnp.argmax(group_score, axis=-1)
    group_hot = group[:, None] == jnp.arange(N_GROUPS)[None, :]
    in_group = jnp.sum(jnp.where(group_hot[:, :, None], biased, 0.0), axis=1)
    local = lax.top_k(in_group, TOP_K)[1]
    expert = (group[:, None] * EXPERTS_PER_GROUP + local).astype(jnp.int32)
    expert_hot = expert[:, :, None] == jnp.arange(N_EXPERTS, dtype=jnp.int32)[None, None, :]
    wsel = jnp.sum(jnp.where(expert_hot, aff[:, None, :], 0.0), axis=-1)
    wsel = wsel / jnp.sum(wsel, axis=-1, keepdims=True)

    flat_e = expert.reshape(-1)
    onehot = (flat_e[:, None] == jnp.arange(N_EXPERTS, dtype=jnp.int32)[None, :]).astype(jnp.int32)
    csum = jnp.cumsum(onehot, axis=0)
    rank = jnp.sum(csum * onehot, axis=1) - 1
    counts = csum[-1]
    tiles_per = (counts + MOE_TILE - 1) // MOE_TILE
    tile_end = jnp.cumsum(tiles_per)
    tile_start = tile_end - tiles_per
    pos = jnp.sum(onehot * tile_start[None, :], axis=1) * MOE_TILE + rank
    n_tiles = (t * TOP_K) // MOE_TILE + N_EXPERTS
    p = n_tiles * MOE_TILE
    n_assign = t * TOP_K
    assign_of = jnp.full((p,), -1, jnp.int32).at[pos].set(jnp.arange(n_assign, dtype=jnp.int32))
    is_pad = assign_of < 0
    src_row = jnp.where(is_pad, 0, assign_of // TOP_K)
    dst_row = jnp.where(is_pad, n_assign + jnp.cumsum(is_pad.astype(jnp.int32)) - 1, assign_of)
    tile_ids = jnp.arange(n_tiles, dtype=jnp.int32)
    n_valid = tile_end[-1].astype(jnp.int32)
    tile_expert = jnp.minimum(jnp.sum((tile_end[None, :] <= tile_ids[:, None]).astype(jnp.int32), axis=1), N_EXPERTS - 1)
    last_expert = tile_expert[jnp.maximum(n_valid - 1, 0)]
    tile_expert = jnp.where(tile_ids < n_valid, tile_expert, last_expert)

    y = expert_ffn(h, src_row, dst_row, tile_expert, n_valid.reshape(1), w_gate, w_up, w_down)
    return y, wsel


def axial_rope_tables(n_tokens):
    t = jnp.arange(n_tokens)
    row = (t // GRID_W).astype(jnp.float32)
    col = (t % GRID_W).astype(jnp.float32)
    n_freq = HEAD_DIM // 4
    inv = ROPE_THETA ** (-jnp.arange(n_freq, dtype=jnp.float32) / n_freq)
    ang = jnp.concatenate([row[:, None] * inv, col[:, None] * inv], axis=-1)
    cos, sin = jnp.cos(ang), jnp.sin(ang)
    cos_t = jnp.tile(jnp.concatenate([cos, cos], axis=-1), (1, NA_HEADS))
    sin_t = jnp.tile(jnp.concatenate([-sin, sin], axis=-1), (1, NA_HEADS))
    return cos_t, sin_t


PROJ_TILE = 512
PROJ_COLS = 512
REST_WIDTH = GATE_OFF - POOL_OFF
GATE_WIDTH = N_BRANCH * D_MODEL
PROJ_VMEM_LIMIT = 56 * 1024 * 1024


def _swap_head_halves(x):
    w = x.shape[-1]
    lane = lax.broadcasted_iota(jnp.int32, (1, w), 1)
    lower = (lane % HEAD_DIM) < HEAD_DIM // 2
    return jnp.where(lower, pltpu.roll(x, w - HEAD_DIM // 2, axis=1), pltpu.roll(x, HEAD_DIM // 2, axis=1))


def _proj_kernel(x_ref, g_ref, shift_ref, scale_ref, w_ref, *refs, rope):
    if rope:
        cos_ref, sin_ref, qr_ref, qp_ref, k_ref, v_ref, rest_ref, gate_ref = refs
    else:
        qp_ref, k_ref, v_ref, rest_ref, gate_ref = refs
    x = x_ref[...]
    y = x * lax.rsqrt(jnp.mean(x * x, axis=-1, keepdims=True) + RMS_EPS) * g_ref[...]
    h = (y * (1.0 + scale_ref[...]) + shift_ref[...]).astype(jnp.bfloat16)

    def cols(c0, width=PROJ_COLS):
        return jnp.dot(h, w_ref[:, c0:c0 + width], preferred_element_type=jnp.float32)

    q = cols(0) * (HEAD_DIM ** -0.5)
    k = cols(BRANCH_WIDTH)
    qp_ref[...] = q.astype(jnp.bfloat16)
    if rope:
        cos_t, sin_t = cos_ref[...], sin_ref[...]
        qr_ref[...] = (q * cos_t + _swap_head_halves(q) * sin_t).astype(jnp.bfloat16)
        k = k * cos_t + _swap_head_halves(k) * sin_t
    k_ref[...] = k.astype(jnp.bfloat16)
    v_ref[...] = cols(2 * BRANCH_WIDTH).astype(jnp.bfloat16)
    for c0 in range(0, REST_WIDTH, PROJ_COLS):
        rest_ref[:, c0:c0 + PROJ_COLS] = cols(POOL_OFF + c0)
    for c0 in range(0, GATE_WIDTH, PROJ_COLS):
        gate_ref[:, c0:c0 + PROJ_COLS] = jax.nn.sigmoid(cols(GATE_OFF + c0)).astype(jnp.bfloat16)


def input_projection(x, g, shift, scale, w, rope_tables=None):
    b, n, d = x.shape
    tm = min(PROJ_TILE, n)
    rope = rope_tables is not None
    per_batch = shift.shape[0] > 1
    mod_spec = pl.BlockSpec((None, 1, d), (lambda i, j: (i, 0, 0)) if per_batch else (lambda i, j: (0, 0, 0)))
    tok = lambda width: pl.BlockSpec((None, tm, width), lambda i, j: (i, j, 0))
    in_specs = [tok(d), pl.BlockSpec((1, d), lambda i, j: (0, 0)), mod_spec, mod_spec,
                pl.BlockSpec((d, IN_WIDTH), lambda i, j: (0, 0), pipeline_mode=pl.Buffered(1))]
    args = [x, g.reshape(1, d), shift[:, None, :], scale[:, None, :], w]
    n_qkv = 3
    if rope:
        in_specs += [pl.BlockSpec((tm, BRANCH_WIDTH), lambda i, j: (j, 0))] * 2
        args += list(rope_tables)
        n_qkv = 4
    bf = jnp.bfloat16
    out_shape = ([jax.ShapeDtypeStruct((b, n, BRANCH_WIDTH), bf)] * n_qkv
                 + [jax.ShapeDtypeStruct((b, n, REST_WIDTH), jnp.float32), jax.ShapeDtypeStruct((b, n, GATE_WIDTH), bf)])
    out_specs = [tok(BRANCH_WIDTH)] * n_qkv + [tok(REST_WIDTH), tok(GATE_WIDTH)]
    return pl.pallas_call(
        functools.partial(_proj_kernel, rope=rope),
        grid=(b, n // tm),
        in_specs=in_specs,
        out_specs=out_specs,
        out_shape=out_shape,
        compiler_params=pltpu.CompilerParams(
            dimension_semantics=("parallel", "parallel"), vmem_limit_bytes=PROJ_VMEM_LIMIT),
        name="input_projection",
    )(*args)


NA_GROUP = 4
NA_SLAB = NA_GROUP + NA_KH_MAX
NA_HEADS_PER_STEP = 128 // HEAD_DIM


def _natten_bias_patterns(rpb, rows):
    j = jnp.arange(GRID_W)
    col_start = jnp.clip(j - NA_KW // 2, 0, GRID_W - NA_KW)
    col_mask = (j[None, :] >= col_start[:, None]) & (j[None, :] < col_start[:, None] + NA_KW)
    dc = jnp.clip(j[None, :] - j[:, None], -(NA_KW - 1), NA_KW - 1) + NA_KW - 1
    dc_hot = (dc[None, :, :] == jnp.arange(2 * NA_KW - 1)[:, None, None]).astype(jnp.float32)
    by_col = jnp.einsum('hrc,cqk->hrqk', rpb.astype(jnp.float32), dc_hot, precision=lax.Precision.HIGHEST)
    by_col = jnp.where(col_mask[None, None], by_col, NEG_INF)
    masked = jnp.full((NA_HEADS, GRID_W, GRID_W), NEG_INF, jnp.float32)
    n_groups = rows // NA_GROUP
    patterns = []
    for g in (0, 1, n_groups - 1):
        slab0 = min(max(g * NA_GROUP - NA_KH_MAX // 2, 0), rows - NA_SLAB)
        q_rows = []
        for r in range(g * NA_GROUP, (g + 1) * NA_GROUP):
            win0 = min(max(r - NA_KH_MAX // 2, 0), rows - NA_KH_MAX)
            blocks = [by_col[:, kr - r + NA_KH_MAX - 1] if win0 <= kr < win0 + NA_KH_MAX else masked
                      for kr in range(slab0, slab0 + NA_SLAB)]
            q_rows.append(jnp.concatenate(blocks, axis=-1))
        patterns.append(jnp.concatenate(q_rows, axis=1))
    return jnp.stack(patterns, axis=1)


def _softmax_pv(s_lat, s_ctx, v_lat, v_ctx):
    m = jnp.max(s_lat, axis=-1, keepdims=True)
    if s_ctx is not None:
        m = jnp.maximum(m, jnp.max(s_ctx, axis=-1, keepdims=True))
    p_lat = jnp.exp(s_lat - m)
    den = jnp.sum(p_lat, axis=-1, keepdims=True)
    o = jnp.dot(p_lat.astype(jnp.bfloat16), v_lat, preferred_element_type=jnp.float32)
    if s_ctx is not None:
        p_ctx = jnp.exp(s_ctx - m)
        den = den + jnp.sum(p_ctx, axis=-1, keepdims=True)
        o = o + jnp.dot(p_ctx.astype(jnp.bfloat16), v_ctx, preferred_element_type=jnp.float32)
    return o / den


def _head_masks():
    lane = lax.broadcasted_iota(jnp.int32, (1, NA_HEADS_PER_STEP * HEAD_DIM), 1)
    return [(lane >= h * HEAD_DIM) & (lane < (h + 1) * HEAD_DIM) for h in range(NA_HEADS_PER_STEP)]


def _natten_kernel(qr_ref, qp_ref, kr_ref, v_ref, kc_ref, vc_ref, bias_ref, o_ref):
    rows = qr_ref.shape[0] // GRID_W
    n_groups = rows // NA_GROUP
    nq = NA_GROUP * GRID_W
    nk = NA_SLAB * GRID_W
    kc = kc_ref[...]
    vc = vc_ref[...]
    nt = (((1,), (1,)), ((), ()))
    masks = _head_masks()

    def body(g, carry):
        slab0 = jnp.clip(g * NA_GROUP - NA_KH_MAX // 2, 0, rows - NA_SLAB)
        pattern = jnp.where(g == 0, 0, jnp.where(g == n_groups - 1, 2, 1))
        q0 = pl.multiple_of(g * nq, nq)
        k0 = pl.multiple_of(slab0 * GRID_W, GRID_W)
        qr = qr_ref[pl.ds(q0, nq), :]
        qp = qp_ref[pl.ds(q0, nq), :]
        ks = kr_ref[pl.ds(k0, nk), :]
        vs = v_ref[pl.ds(k0, nk), :]
        out = None
        for h, mask in enumerate(masks):
            s_lat = lax.dot_general(jnp.where(mask, qr, 0), ks, nt, preferred_element_type=jnp.float32)
            s_lat = s_lat + bias_ref[h, pattern]
            s_ctx = lax.dot_general(jnp.where(mask, qp, 0), kc, nt, preferred_element_type=jnp.float32)
            o = _softmax_pv(s_lat, s_ctx, vs, vc)
            out = o if out is None else jnp.where(mask, o, out)
        o_ref[pl.ds(q0, nq), :] = out.astype(o_ref.dtype)
        return carry

    lax.fori_loop(0, n_groups, body, 0)


def neighbourhood_attention(q_rot, q_plain, k_rot, v, k_ctx, v_ctx, rpb):
    b, s, width = q_rot.shape
    n_ctx = k_ctx.shape[1]
    lanes = NA_HEADS_PER_STEP * HEAD_DIM
    bias = _natten_bias_patterns(rpb, s // GRID_W)
    lat = pl.BlockSpec((None, s, lanes), lambda i, j: (i, 0, j))
    cx = pl.BlockSpec((None, n_ctx, lanes), lambda i, j: (i, 0, j))
    return pl.pallas_call(
        _natten_kernel,
        grid=(b, width // lanes),
        in_specs=[lat, lat, lat, lat, cx, cx,
                  pl.BlockSpec((NA_HEADS_PER_STEP, 3, NA_GROUP * GRID_W, NA_SLAB * GRID_W), lambda i, j: (j, 0, 0, 0))],
        out_specs=lat,
        out_shape=jax.ShapeDtypeStruct((b, s, width), jnp.float32),
        compiler_params=pltpu.CompilerParams(
            dimension_semantics=("parallel", "parallel"), vmem_limit_bytes=VMEM_LIMIT_BYTES),
        name="natten",
    )(q_rot, q_plain, k_rot, v, k_ctx, v_ctx, bias)


def _ctx_attn_kernel(q_ref, k_ref, v_ref, o_ref):
    nt = (((1,), (1,)), ((), ()))
    q, k, v = q_ref[...], k_ref[...], v_ref[...]
    out = None
    for mask in _head_masks():
        s = lax.dot_general(jnp.where(mask, q, 0), k, nt, preferred_element_type=jnp.float32)
        o = _softmax_pv(s, None, v, None)
        out = o if out is None else jnp.where(mask, o, out)
    o_ref[...] = out.astype(o_ref.dtype)


def context_attention(q, k, v):
    b, n, width = q.shape
    lanes = NA_HEADS_PER_STEP * HEAD_DIM
    spec = pl.BlockSpec((None, n, lanes), lambda i, j: (i, 0, j))
    return pl.pallas_call(
        _ctx_attn_kernel,
        grid=(b, width // lanes),
        in_specs=[spec, spec, spec],
        out_specs=spec,
        out_shape=jax.ShapeDtypeStruct((b, n, width), jnp.float32),
        compiler_params=pltpu.CompilerParams(dimension_semantics=("parallel", "parallel")),
        name="ctx_attention",
    )(q, k, v)


POOL_PAD = 16


def _pool_kernel(u_ref, w_ref, scale_ref, o_ref, pad_ref):
    n = u_ref.shape[0]
    u = u_ref[...]
    pad_ref[pl.ds(0, POOL_PAD), :] = jnp.zeros((POOL_PAD, POOL_GROUP_DIM), jnp.float32)
    pad_ref[pl.ds(POOL_PAD + n, POOL_PAD), :] = jnp.zeros((POOL_PAD, POOL_GROUP_DIM), jnp.float32)
    pad_ref[pl.ds(POOL_PAD, n), :] = u
    t = lax.broadcasted_iota(jnp.int32, (n, 1), 0)
    for g, w in enumerate(POOL_WINDOWS):
        @pl.when(pl.program_id(1) == g)
        def _(w=w):
            first = POOL_PAD - w // 2
            acc = pad_ref[pl.ds(first, n), :]
            for k in range(1, w):
                acc = acc + pad_ref[pl.ds(first + k, n), :]
            cnt = jnp.minimum(t + (w - w // 2), n) - jnp.maximum(t - w // 2, 0)
            y = acc / cnt.astype(jnp.float32) - u
            y = jnp.dot(y.astype(jnp.bfloat16), w_ref[...].astype(jnp.bfloat16), preferred_element_type=jnp.float32)
            o_ref[...] = y * scale_ref[...]


def multiscale_pool(rest, pool_w, pool_scale):
    b, n, _ = rest.shape
    col0 = 0
    return pl.pallas_call(
        _pool_kernel,
        grid=(b, POOL_GROUPS),
        in_specs=[pl.BlockSpec((None, n, POOL_GROUP_DIM), lambda i, g: (i, 0, col0 + g)),
                  pl.BlockSpec((None, POOL_GROUP_DIM, POOL_GROUP_DIM), lambda i, g: (g, 0, 0)),
                  pl.BlockSpec((1, POOL_GROUP_DIM), lambda i, g: (0, g))],
        out_specs=pl.BlockSpec((None, n, POOL_GROUP_DIM), lambda i, g: (i, 0, g)),
        out_shape=jax.ShapeDtypeStruct((b, n, BRANCH_WIDTH), jnp.float32),
        scratch_shapes=[pltpu.VMEM((n + 2 * POOL_PAD, POOL_GROUP_DIM), jnp.float32)],
        compiler_params=pltpu.CompilerParams(
            dimension_semantics=("parallel", "parallel"), vmem_limit_bytes=VMEM_LIMIT_BYTES),
        name="multiscale_pool",
    )(rest, pool_w, pool_scale.reshape(1, BRANCH_WIDTH))


HY_CHUNK = 64
HY_PITCH = 72
HY_LANES = 128
HY_FEAT = 128
HY_VMEM_LIMIT = 56 * 1024 * 1024


def _cis_tables(num, den):
    ang = (-2.0 * math.pi / den) * (num % den).astype(jnp.float32)
    return jnp.cos(ang), jnp.sin(ang)


def _stack_complex(mr, mi):
    return jnp.concatenate([jnp.concatenate([mr, -mi], axis=-1), jnp.concatenate([mi, mr], axis=-1)], axis=-2)


def _hy_dft_tables(n):
    big = 2 * n
    n1_full = big // HY_CHUNK
    k1 = jnp.arange(n1_full, dtype=jnp.int32)
    n1 = jnp.arange(n1_full, dtype=jnp.int32)
    n2 = jnp.arange(HY_CHUNK, dtype=jnp.int32)
    mr, mi = _cis_tables(k1[None, :, None] * (HY_CHUNK * n1[None, None, :] + n2[:, None, None]), big)
    half = n1_full // 2
    fwd_a = _stack_complex(mr[:, :, :half], mi[:, :, :half])
    fwd_a_real = jnp.concatenate([mr, mi], axis=-2)
    mrt = jnp.transpose(mr[:, :, :half], (0, 2, 1))
    mit = -jnp.transpose(mi[:, :, :half], (0, 2, 1))
    inv_a = _stack_complex(mrt, mit)
    er, ei = _cis_tables(n2[:, None] * n2[None, :], HY_CHUNK)
    fwd_b = _stack_complex(er, ei)
    inv_b = _stack_complex(er, -ei)
    bf = jnp.bfloat16
    return fwd_a.astype(bf), fwd_a_real.astype(bf), inv_a.astype(bf), fwd_b.astype(bf), inv_b.astype(bf)


def _hy_positions(n):
    f32 = jnp.float32
    t = jnp.linspace(0.0, 1.0, n, dtype=f32)[:, None]
    bands = (HY_EMB_DIM - 1) // 2
    ang = (2.0 * math.pi / n) * jnp.arange(n, dtype=f32)[:, None]
    f = jnp.linspace(1e-4, bands - 1, bands, dtype=f32)[None, :]
    z = jnp.concatenate([t, jnp.cos(f * ang), -jnp.sin(f * ang)], axis=-1)
    z = jnp.pad(z, ((0, 0), (0, HY_FEAT - HY_EMB_DIM)))
    zb = jnp.concatenate([z[:1], z[:0:-1]], axis=0)
    return z, zb


def _filter_mlp(z, w1, b1, fr, w2, b2):
    hp = lax.Precision.HIGHEST
    h = jnp.sin(fr * (jnp.dot(z, w1, precision=hp, preferred_element_type=jnp.float32) + b1))
    return jnp.sin(fr * (jnp.dot(h, w2, precision=hp, preferred_element_type=jnp.float32) + b2))


def _filter_taps(zf_ref, zb_ref, w1_ref, b1_ref, fr_ref, w2_ref, b2_ref, w3f_ref, w3b_ref, delta_ref, hid_ref):
    n = zf_ref.shape[0]
    hp = lax.Precision.HIGHEST

    @pl.when((pl.program_id(1) == 0) & (pl.program_id(2) == 0))
    def _():
        args = (w1_ref[...], b1_ref[...], fr_ref[...], w2_ref[...], b2_ref[...])
        hid_ref[0] = _filter_mlp(zf_ref[...], *args)
        hid_ref[1] = _filter_mlp(zb_ref[...], *args)

    delta = delta_ref[...]
    hf = jnp.dot(hid_ref[0], w3f_ref[...], precision=hp, preferred_element_type=jnp.float32)
    hf = hf * jnp.exp(-zf_ref[:, 0:1] * delta)
    hb = jnp.dot(hid_ref[1], w3b_ref[...], precision=hp, preferred_element_type=jnp.float32)
    hb = hb * jnp.exp(-zb_ref[:, 0:1] * delta)
    row = lax.broadcasted_iota(jnp.int32, (n, 1), 0)
    hb = jnp.where(row > 0, hb, 0.0)
    norm = jnp.sum(jnp.abs(hf), axis=0, keepdims=True) + jnp.sum(jnp.abs(hb), axis=0, keepdims=True)
    return hf / norm, hb / norm


def _filter_spec_kernel(zf_ref, zb_ref, w1_ref, b1_ref, fr_ref, w2_ref, b2_ref, w3f_ref, w3b_ref, delta_ref,
                        fa_ref, fb_ref, hr_ref, hi_ref, hid_ref, tbuf, sbuf):
    n = zf_ref.shape[0]
    chunks = n // HY_CHUNK
    n1_full = 2 * chunks
    hf, hb = _filter_taps(zf_ref, zb_ref, w1_ref, b1_ref, fr_ref, w2_ref, b2_ref, w3f_ref, w3b_ref, delta_ref, hid_ref)
    for c in range(chunks):
        tbuf[pl.ds(c * HY_PITCH, HY_CHUNK), :] = hf[c * HY_CHUNK:(c + 1) * HY_CHUNK]
        tbuf[pl.ds((chunks + c) * HY_PITCH, HY_CHUNK), :] = hb[c * HY_CHUNK:(c + 1) * HY_CHUNK]

    def stage_a(n2, carry):
        rows = tbuf[pl.ds(n2, n1_full, stride=HY_PITCH), :].astype(jnp.bfloat16)
        a = jnp.dot(fa_ref[n2], rows, preferred_element_type=jnp.float32)
        sbuf[0, pl.ds(n2, n1_full, stride=HY_PITCH), :] = a[:n1_full]
        sbuf[1, pl.ds(n2, n1_full, stride=HY_PITCH), :] = a[n1_full:]
        return carry

    lax.fori_loop(0, HY_CHUNK, stage_a, 0, unroll=2)
    inv_len = 1.0 / (2 * n)

    def stage_b(k1, carry):
        r0 = pl.multiple_of(k1 * HY_PITCH, 8)
        x = jnp.concatenate([sbuf[0, pl.ds(r0, HY_CHUNK), :], sbuf[1, pl.ds(r0, HY_CHUNK), :]], axis=0)
        y = jnp.dot(fb_ref[...], x.astype(jnp.bfloat16), preferred_element_type=jnp.float32) * inv_len
        o0 = pl.multiple_of(k1 * HY_CHUNK, HY_CHUNK)
        hr_ref[pl.ds(o0, HY_CHUNK), :] = y[:HY_CHUNK]
        hi_ref[pl.ds(o0, HY_CHUNK), :] = y[HY_CHUNK:]
        return carry

    lax.fori_loop(0, n1_full, stage_b, 0, unroll=4)


def hyena_filter_spectra(n, hy_w1, hy_b1, hy_freq, hy_w2, hy_b2, hy_w3, tables):
    depth = hy_w1.shape[0]
    zf, zb = _hy_positions(n)
    tiles = BRANCH_WIDTH // HY_LANES
    w1 = jnp.pad(hy_w1, ((0, 0), (0, HY_FEAT - HY_EMB_DIM), (0, 0)))
    deltas = jnp.abs(jnp.linspace(HY_MIN_DECAY, HY_MAX_DECAY, BRANCH_WIDTH, dtype=jnp.float32)).reshape(1, BRANCH_WIDTH)
    fwd_a_real, fwd_b = tables[1], tables[3]
    n1_full = 2 * n // HY_CHUNK
    one = pl.Buffered(1)
    full = lambda shape: pl.BlockSpec(shape, lambda l, o, j: (0,) * len(shape), pipeline_mode=one)
    per_layer = lambda shape: pl.BlockSpec((None,) + shape, lambda l, o, j: (l,) + (0,) * len(shape))
    out_spec = pl.BlockSpec((None, None, 2 * n, HY_LANES), lambda l, o, j: (l, o, 0, j), pipeline_mode=one)
    out_sds = jax.ShapeDtypeStruct((depth, HY_ORDER, 2 * n, BRANCH_WIDTH), jnp.float32)
    return pl.pallas_call(
        _filter_spec_kernel,
        grid=(depth, HY_ORDER, tiles),
        in_specs=[full((n, HY_FEAT)), full((n, HY_FEAT)),
                  per_layer((HY_FEAT, HY_FILTER_HIDDEN)), per_layer((1, HY_FILTER_HIDDEN)),
                  per_layer((1, HY_FILTER_HIDDEN)),
                  per_layer((HY_FILTER_HIDDEN, HY_FILTER_HIDDEN)), per_layer((1, HY_FILTER_HIDDEN)),
                  pl.BlockSpec((None, HY_FILTER_HIDDEN, HY_LANES), lambda l, o, j: (l, 0, o * 2 * tiles + j)),
                  pl.BlockSpec((None, HY_FILTER_HIDDEN, HY_LANES), lambda l, o, j: (l, 0, o * 2 * tiles + tiles + j)),
                  pl.BlockSpec((1, HY_LANES), lambda l, o, j: (0, j)),
                  full((HY_CHUNK, 2 * n1_full, n1_full)), full((2 * HY_CHUNK, 2 * HY_CHUNK))],
        out_specs=[out_spec, out_spec],
        out_shape=[out_sds, out_sds],
        scratch_shapes=[pltpu.VMEM((2, n, HY_FILTER_HIDDEN), jnp.float32),
                        pltpu.VMEM((n1_full * HY_PITCH, HY_LANES), jnp.float32),
                        pltpu.VMEM((2, n1_full * HY_PITCH, HY_LANES), jnp.float32)],
        compiler_params=pltpu.CompilerParams(
            dimension_semantics=("arbitrary", "arbitrary", "arbitrary"), vmem_limit_bytes=HY_VMEM_LIMIT),
        name="hyena_filter_spectra",
    )(zf, zb, w1, hy_b1[:, None, :], hy_freq[:, None, :], hy_w2, hy_b2[:, None, :], hy_w3, hy_w3, deltas,
      fwd_a_real, fwd_b)


def _short_conv(u_ref, b, w_ref, bias_ref, pad_ref):
    n = u_ref.shape[1]
    x = u_ref[b]
    pad_ref[pl.ds(0, 8), :] = jnp.zeros((8, HY_LANES), jnp.float32)
    pad_ref[pl.ds(8 + n, 8), :] = jnp.zeros((8, HY_LANES), jnp.float32)
    pad_ref[pl.ds(8, n), :] = x
    return (pad_ref[pl.ds(7, n), :] * w_ref[0:1, :] + x * w_ref[1:2, :] + pad_ref[pl.ds(9, n), :] * w_ref[2:3, :]
            + bias_ref[...])


def _hyena_conv_kernel(z_ref, g_ref, wz_ref, bz_ref, wg_ref, bg_ref, skip_ref, hr_ref, hi_ref,
                       fa_ref, ia_ref, fb_ref, ib_ref, o_ref, pad_ref, zbuf, sbuf, *, conv_z):
    n = z_ref.shape[1]
    chunks = n // HY_CHUNK
    n1_full = 2 * chunks
    for b in range(2):
        z = _short_conv(z_ref, b, wz_ref, bz_ref, pad_ref) if conv_z else z_ref[b]
        for c in range(chunks):
            zbuf[b, pl.ds(c * HY_PITCH, HY_CHUNK), :] = z[c * HY_CHUNK:(c + 1) * HY_CHUNK]

    def stage_a(n2, carry):
        x = jnp.concatenate([zbuf[0, pl.ds(n2, chunks, stride=HY_PITCH), :],
                             zbuf[1, pl.ds(n2, chunks, stride=HY_PITCH), :]], axis=0)
        a = jnp.dot(fa_ref[n2], x.astype(jnp.bfloat16), preferred_element_type=jnp.float32)
        sbuf[0, pl.ds(n2, n1_full, stride=HY_PITCH), :] = a[:n1_full]
        sbuf[1, pl.ds(n2, n1_full, stride=HY_PITCH), :] = a[n1_full:]
        return carry

    lax.fori_loop(0, HY_CHUNK, stage_a, 0, unroll=2)

    def stage_b(k1, carry):
        r0 = pl.multiple_of(k1 * HY_PITCH, 8)
        x = jnp.concatenate([sbuf[0, pl.ds(r0, HY_CHUNK), :], sbuf[1, pl.ds(r0, HY_CHUNK), :]], axis=0)
        s = jnp.dot(fb_ref[...], x.astype(jnp.bfloat16), preferred_element_type=jnp.float32)
        h0 = pl.multiple_of(k1 * HY_CHUNK, HY_CHUNK)
        hr = hr_ref[pl.ds(h0, HY_CHUNK), :]
        hi = hi_ref[pl.ds(h0, HY_CHUNK), :]
        sr, si = s[:HY_CHUNK], s[HY_CHUNK:]
        y = jnp.concatenate([sr * hr - si * hi, sr * hi + si * hr], axis=0)
        c = jnp.dot(ib_ref[...], y.astype(jnp.bfloat16), preferred_element_type=jnp.float32)
        sbuf[0, pl.ds(r0, HY_CHUNK), :] = c[:HY_CHUNK]
        sbuf[1, pl.ds(r0, HY_CHUNK), :] = c[HY_CHUNK:]
        return carry

    lax.fori_loop(0, n1_full, stage_b, 0, unroll=8)
    skip = skip_ref[...]

    def stage_c(n2, carry):
        x = jnp.concatenate([sbuf[0, pl.ds(n2, n1_full, stride=HY_PITCH), :],
                             sbuf[1, pl.ds(n2, n1_full, stride=HY_PITCH), :]], axis=0)
        y = jnp.dot(ia_ref[n2], x.astype(jnp.bfloat16), preferred_element_type=jnp.float32)
        for b in range(2):
            zb = zbuf[b, pl.ds(n2, chunks, stride=HY_PITCH), :]
            zbuf[b, pl.ds(n2, chunks, stride=HY_PITCH), :] = y[b * chunks:(b + 1) * chunks] + zb * skip
        return carry

    lax.fori_loop(0, HY_CHUNK, stage_c, 0, unroll=4)
    for b in range(2):
        g = _short_conv(g_ref, b, wg_ref, bg_ref, pad_ref)
        for c in range(chunks):
            o_ref[b, pl.ds(c * HY_CHUNK, HY_CHUNK), :] = (
                g[c * HY_CHUNK:(c + 1) * HY_CHUNK] * zbuf[b, pl.ds(c * HY_PITCH, HY_CHUNK), :])


def hyena_long_conv(z, z_col0, g, g_col0, conv_z, short_w, short_b, w_off_z, w_off_g, skip, hr, hi, tables):
    b, n, _ = g.shape
    tiles = BRANCH_WIDTH // HY_LANES
    fwd_a, _, inv_a, fwd_b, inv_b = tables
    n1_full = 2 * n // HY_CHUNK
    one = pl.Buffered(1)
    zc, gc = z_col0 // HY_LANES, g_col0 // HY_LANES
    wz, wg = w_off_z // HY_LANES, w_off_g // HY_LANES
    const = lambda shape: pl.BlockSpec(shape, lambda j, p: (0,) * len(shape), pipeline_mode=one)
    return pl.pallas_call(
        functools.partial(_hyena_conv_kernel, conv_z=conv_z),
        grid=(tiles, b // 2),
        in_specs=[pl.BlockSpec((2, n, HY_LANES), lambda j, p: (p, 0, zc + j), pipeline_mode=one),
                  pl.BlockSpec((2, n, HY_LANES), lambda j, p: (p, 0, gc + j), pipeline_mode=one),
                  pl.BlockSpec((HY_SHORT, HY_LANES), lambda j, p: (0, wz + j)),
                  pl.BlockSpec((1, HY_LANES), lambda j, p: (0, wz + j)),
                  pl.BlockSpec((HY_SHORT, HY_LANES), lambda j, p: (0, wg + j)),
                  pl.BlockSpec((1, HY_LANES), lambda j, p: (0, wg + j)),
                  pl.BlockSpec((1, HY_LANES), lambda j, p: (0, j)),
                  pl.BlockSpec((2 * n, HY_LANES), lambda j, p: (0, j), pipeline_mode=one),
                  pl.BlockSpec((2 * n, HY_LANES), lambda j, p: (0, j), pipeline_mode=one),
                  const((HY_CHUNK, 2 * n1_full, n1_full)), const((HY_CHUNK, n1_full, 2 * n1_full)),
                  const((2 * HY_CHUNK, 2 * HY_CHUNK)), const((2 * HY_CHUNK, 2 * HY_CHUNK))],
        out_specs=pl.BlockSpec((2, n, HY_LANES), lambda j, p: (p, 0, j), pipeline_mode=one),
        out_shape=jax.ShapeDtypeStruct((b, n, BRANCH_WIDTH), jnp.float32),
        scratch_shapes=[pltpu.VMEM((n + 16, HY_LANES), jnp.float32),
                        pltpu.VMEM((2, (n // HY_CHUNK) * HY_PITCH, HY_LANES), jnp.float32),
                        pltpu.VMEM((2, n1_full * HY_PITCH, HY_LANES), jnp.float32)],
        compiler_params=pltpu.CompilerParams(
            dimension_semantics=("parallel", "parallel"), vmem_limit_bytes=HY_VMEM_LIMIT),
        name="hyena_long_conv",
    )(z, g, short_w, short_b.reshape(1, -1), short_w, short_b.reshape(1, -1), skip.reshape(1, -1), hr, hi,
      fwd_a, inv_a, fwd_b, inv_b)


def hyena_mixer(rest, short_w, short_b, skip, hr, hi, tables):
    off = HY_OFF - POOL_OFF
    z1 = hyena_long_conv(rest, off, rest, off + BRANCH_WIDTH, True, short_w, short_b, 0, BRANCH_WIDTH,
                         skip[0], hr[0], hi[0], tables)
    return hyena_long_conv(z1, 0, rest, off + 2 * BRANCH_WIDTH, False, short_w, short_b, 0, 2 * BRANCH_WIDTH,
                           skip[1], hr[1], hi[1], tables)


MERGE_TILE = 512


def _merge_kernel(ya_ref, yp_ref, yh_ref, gate_ref, x_ref, res_gate_ref, wb_ref, wo_ref, g2_ref, shift_ref, scale_ref,
                  wr_ref, xo_ref, h_ref, logit_ref):
    d = x_ref.shape[-1]
    merged = None
    for i, y_ref in enumerate((ya_ref, yp_ref, yh_ref)):
        br = jnp.dot(y_ref[...].astype(jnp.bfloat16), wb_ref[i], preferred_element_type=jnp.float32)
        term = gate_ref[:, i * d:(i + 1) * d].astype(jnp.float32) * br
        merged = term if merged is None else merged + term
    out = jnp.dot(merged.astype(jnp.bfloat16), wo_ref[...], preferred_element_type=jnp.float32)
    x = x_ref[...] + res_gate_ref[...] * out
    xo_ref[...] = x
    y = x * lax.rsqrt(jnp.mean(x * x, axis=-1, keepdims=True) + RMS_EPS) * g2_ref[...]
    h = y * (1.0 + scale_ref[...]) + shift_ref[...]
    h_ref[...] = h
    logit_ref[...] = jnp.dot(h, wr_ref[...], precision=lax.Precision.HIGHEST, preferred_element_type=jnp.float32)


def merge_and_norm(y_attn, y_pool, y_hy, gates, x, mod, w_branch, w_out, g2, w_router):
    b, n, d = x.shape
    tm = min(MERGE_TILE, n)
    per_batch = mod.shape[0] > 1
    mod_spec = lambda k: pl.BlockSpec((None, None, 1, d),
                                      (lambda i, j: (i, k, 0, 0)) if per_batch else (lambda i, j: (0, k, 0, 0)))
    tok = lambda width: pl.BlockSpec((None, tm, width), lambda i, j: (i, j, 0))
    const = lambda shape: pl.BlockSpec(shape, lambda i, j: (0,) * len(shape))
    call = pl.pallas_call(
        _merge_kernel,
        grid=(b, n // tm),
        in_specs=[tok(BRANCH_WIDTH), tok(BRANCH_WIDTH), tok(BRANCH_WIDTH), tok(GATE_WIDTH), tok(d), mod_spec(2),
                  const((N_BRANCH, BRANCH_WIDTH, d)), const((d, d)), const((1, d)), mod_spec(3), mod_spec(4),
                  const((d, N_EXPERTS))],
        out_specs=[tok(d), tok(d), tok(N_EXPERTS)],
        out_shape=[jax.ShapeDtypeStruct((b, n, d), jnp.float32), jax.ShapeDtypeStruct((b, n, d), jnp.float32),
                   jax.ShapeDtypeStruct((b, n, N_EXPERTS), jnp.float32)],
        compiler_params=pltpu.CompilerParams(
            dimension_semantics=("parallel", "parallel"), vmem_limit_bytes=VMEM_LIMIT_BYTES),
        name="merge_and_norm",
    )
    mod4 = mod[:, :, None, :]
    return call(y_attn, y_pool, y_hy, gates, x, mod4, w_branch, w_out, g2.reshape(1, d), mod4, mod4, w_router)


def _combine_kernel(y_ref, w_ref, x_ref, gate_ref, g_ref, o_ref, *, final_norm):
    d = x_ref.shape[-1]
    w = w_ref[...]
    f = y_ref[:, :d] * w[:, 0:1] + y_ref[:, d:] * w[:, 1:2]
    x = x_ref[...] + gate_ref[...] * f
    if final_norm:
        x = x * lax.rsqrt(jnp.mean(x * x, axis=-1, keepdims=True) + RMS_EPS) * g_ref[...]
    o_ref[...] = x


def moe_combine(y_pairs, wsel, row0, x, mod, final_g=None):
    b, n, d = x.shape
    tm = min(MERGE_TILE, n)
    tiles = n // tm
    blk0 = row0 // tm
    per_batch = mod.shape[0] > 1
    tok = lambda width: pl.BlockSpec((None, tm, width), lambda i, j: (i, j, 0))
    flat = lambda width: pl.BlockSpec((tm, width), lambda i, j: (blk0 + i * tiles + j, 0))
    k = N_MOD - 1
    gate_spec = pl.BlockSpec((None, None, 1, d), (lambda i, j: (i, k, 0, 0)) if per_batch else (lambda i, j: (0, k, 0, 0)))
    g = jnp.ones((1, d), jnp.float32) if final_g is None else final_g.reshape(1, d)
    return pl.pallas_call(
        functools.partial(_combine_kernel, final_norm=final_g is not None),
        grid=(b, n // tm),
        in_specs=[flat(TOP_K * d), flat(TOP_K), tok(d), gate_spec, pl.BlockSpec((1, d), lambda i, j: (0, 0))],
        out_specs=tok(d),
        out_shape=jax.ShapeDtypeStruct((b, n, d), jnp.float32),
        compiler_params=pltpu.CompilerParams(
            dimension_semantics=("parallel", "parallel"), vmem_limit_bytes=VMEM_LIMIT_BYTES),
        name="moe_combine",
    )(y_pairs, wsel, x, mod[:, :, None, :], g)


def latent_mixers(x, g1, mod, w_in, rope_tables, k_ctx, v_ctx, rpb, pool_w, pool_scale, hy, w_branch, w_out, g2, w_router):
    q_rot, q_plain, k, v, rest, gates = input_projection(x, g1, mod[:, 0], mod[:, 1], w_in, rope_tables)
    y_attn = neighbourhood_attention(q_rot, q_plain, k, v, k_ctx, v_ctx, rpb)
    y_pool = multiscale_pool(rest, pool_w, pool_scale)
    y_hy = hyena_mixer(rest, *hy)
    return merge_and_norm(y_attn, y_pool, y_hy, gates, x, mod, w_branch, w_out, g2, w_router)


def kernel(x, c, ctx, c_ctx, norm1_g, norm2_g, w_mod, b_mod, w_in, rpb, pool_w, pool_scale, hy_short_w, hy_short_b, hy_w1, hy_b1, hy_freq, hy_w2, hy_b2, hy_w3, hy_skip, w_branch, w_out, w_router, router_bias, w_gate_e, w_up_e, w_down_e, final_g):
    b, s, d = x.shape
    n_ctx = ctx.shape[1]
    bf = jnp.bfloat16
    rope_tables = axial_rope_tables(s)
    tables_l = _hy_dft_tables(s)
    tables_c = _hy_dft_tables(n_ctx)
    spec_l = hyena_filter_spectra(s, hy_w1, hy_b1, hy_freq, hy_w2, hy_b2, hy_w3, tables_l)
    spec_c = hyena_filter_spectra(n_ctx, hy_w1[:DEPTH - 1], hy_b1[:DEPTH - 1], hy_freq[:DEPTH - 1], hy_w2[:DEPTH - 1],
                                  hy_b2[:DEPTH - 1], hy_w3[:DEPTH - 1], tables_c)
    cond = jnp.concatenate([c, c_ctx[None, :], jnp.zeros((8 - (b + 1) % 8, d), c.dtype)], axis=0)
    cond = jax.nn.silu(cond)
    xl, xc = x, ctx
    for l in range(DEPTH):
        need_ctx = l < DEPTH - 1
        mod = (matmul(cond, w_mod[l]) + b_mod[l]).reshape(-1, N_MOD, d)
        ml, mc = mod[:b], mod[b:b + 1]
        w_in_l, w_branch_l, w_out_l = w_in[l].astype(bf), w_branch[l].astype(bf), w_out[l].astype(bf)
        common = (pool_w[l], pool_scale[l])
        hy_c = (hy_short_w[l], hy_short_b[l], hy_skip[l])
        qc, kc, vc, rest_c, gates_c = input_projection(xc, norm1_g[l], mc[:, 0], mc[:, 1], w_in_l)
        xl, hl2, logit_l = latent_mixers(xl, norm1_g[l], ml, w_in_l, rope_tables, kc, vc, rpb[l], *common,
                                         hy_c + (spec_l[0][l], spec_l[1][l], tables_l), w_branch_l, w_out_l,
                                         norm2_g[l], w_router)
        tokens, logits = hl2.reshape(b * s, d), logit_l.reshape(b * s, N_EXPERTS)
        if need_ctx:
            ya_c = context_attention(qc, kc, vc)
            yp_c = multiscale_pool(rest_c, *common)
            yh_c = hyena_mixer(rest_c, *hy_c, spec_c[0][l], spec_c[1][l], tables_c)
            xc, hc2, logit_c = merge_and_norm(ya_c, yp_c, yh_c, gates_c, xc, mc, w_branch_l, w_out_l, norm2_g[l], w_router)
            tokens = jnp.concatenate([tokens, hc2.reshape(b * n_ctx, d)], axis=0)
            logits = jnp.concatenate([logits, logit_c.reshape(b * n_ctx, N_EXPERTS)], axis=0)
        y, wsel = grouped_moe(tokens, logits, router_bias, w_gate_e[l], w_up_e[l], w_down_e[l])
        y_pairs = y.reshape(-1, TOP_K * d)
        xl = moe_combine(y_pairs, wsel, 0, xl, ml, final_g if l == DEPTH - 1 else None)
        if need_ctx:
            xc = moe_combine(y_pairs, wsel, b * s, xc, mc)
    return xl
```

```python
import functools
import math

import jax
import jax.numpy as jnp
from jax import lax
from jax.experimental import pallas as pl
from jax.experimental.pallas import tpu as pltpu

D_MODEL = 1024
DEPTH = 4
GRID_W = 64
BRANCH_WIDTH = D_MODEL // 2
N_BRANCH = 3
HEAD_DIM = 64
NA_HEADS = BRANCH_WIDTH // HEAD_DIM
NA_KH_MAX = 8
NA_KW = 16
ROPE_THETA = 10000.0
NEG_INF = -1e30
POOL_WINDOWS = (2, 4, 8, 16)
POOL_GROUPS = len(POOL_WINDOWS)
POOL_GROUP_DIM = BRANCH_WIDTH // POOL_GROUPS
HY_ORDER = 2
HY_SHORT = 3
HY_EMB_DIM = 33
HY_FILTER_HIDDEN = 64
HY_MIN_DECAY = math.log(1e-2) / 0.3
HY_MAX_DECAY = math.log(1e-2) / 1.5
N_EXPERTS = 16
N_GROUPS = 4
EXPERTS_PER_GROUP = N_EXPERTS // N_GROUPS
TOP_K = 2
EXPERT_FF = D_MODEL // 2
N_MOD = 6
RMS_EPS = 1e-6
POOL_OFF = 3 * BRANCH_WIDTH
HY_OFF = POOL_OFF + BRANCH_WIDTH
GATE_OFF = HY_OFF + (HY_ORDER + 1) * BRANCH_WIDTH
IN_WIDTH = GATE_OFF + N_BRANCH * D_MODEL

VMEM_LIMIT_BYTES = 48 * 1024 * 1024
MOE_TILE = 256
ROWS_PER_ISSUE = 8


def _mm_kernel(a_ref, b_ref, o_ref):
    o_ref[...] = jnp.dot(a_ref[...].astype(jnp.bfloat16), b_ref[...].astype(jnp.bfloat16),
                         preferred_element_type=jnp.float32).astype(o_ref.dtype)


def _pick_tile(n, cands):
    for c in cands:
        if n % c == 0:
            return c
    return n


def matmul(a, b, out_dtype=jnp.float32):
    m, k = a.shape
    _, n = b.shape
    tm = _pick_tile(m, (1024, 512, 256, 128, 64, 32, 16, 8))
    tn = _pick_tile(n, (1664, 1024, 512, 256, 128))
    return pl.pallas_call(
        _mm_kernel,
        grid=(m // tm, n // tn),
        in_specs=[pl.BlockSpec((tm, k), lambda i, j: (i, 0)),
                  pl.BlockSpec((k, tn), lambda i, j: (0, j))],
        out_specs=pl.BlockSpec((tm, tn), lambda i, j: (i, j)),
        out_shape=jax.ShapeDtypeStruct((m, n), out_dtype),
        compiler_params=pltpu.CompilerParams(
            dimension_semantics=("parallel", "parallel"), vmem_limit_bytes=VMEM_LIMIT_BYTES),
        name="dense_matmul",
    )(a, b)


def _expert_kernel(src_ref, dst_ref, tile_expert_ref, n_valid_ref, h_hbm, wg_ref, wu_ref, wd_ref, y_hbm,
                   xbuf, obuf, zbuf, gsem, ssem, zsem):
    i = pl.program_id(0)
    n_valid = n_valid_ref[0]
    slot = i % 2

    def gather(tile, s):
        base = tile * MOE_TILE

        def rows(g, carry):
            r0 = pl.multiple_of(g * ROWS_PER_ISSUE, ROWS_PER_ISSUE)
            for j in range(ROWS_PER_ISSUE):
                pltpu.make_async_copy(h_hbm.at[pl.ds(src_ref[base + r0 + j], 1)],
                                      xbuf.at[s, pl.ds(r0 + j, 1)], gsem.at[s]).start()
            return carry
        lax.fori_loop(0, MOE_TILE // ROWS_PER_ISSUE, rows, 0)

    def wait_rows(buf, sem, s):
        pltpu.make_async_copy(buf.at[s], buf.at[s], sem.at[s]).wait()

    @pl.when((i == 0) & (n_valid > 0))
    def _():
        gather(0, 0)

    @pl.when(i + 1 < n_valid)
    def _():
        gather(i + 1, 1 - slot)

    @pl.when(i < n_valid)
    def _():
        wait_rows(xbuf, gsem, slot)

        @pl.when(i >= 2)
        def _():
            wait_rows(obuf, ssem, slot)

        x = xbuf[slot].astype(jnp.bfloat16)
        g = jnp.dot(x, wg_ref[...].astype(jnp.bfloat16), preferred_element_type=jnp.float32)
        u = jnp.dot(x, wu_ref[...].astype(jnp.bfloat16), preferred_element_type=jnp.float32)
        hid = (g * jax.nn.sigmoid(g)) * u
        obuf[slot] = jnp.dot(hid.astype(jnp.bfloat16), wd_ref[...].astype(jnp.bfloat16),
                             preferred_element_type=jnp.float32)

        base = i * MOE_TILE

        def rows(g, carry):
            r0 = pl.multiple_of(g * ROWS_PER_ISSUE, ROWS_PER_ISSUE)
            for j in range(ROWS_PER_ISSUE):
                pltpu.make_async_copy(obuf.at[slot, pl.ds(r0 + j, 1)],
                                      y_hbm.at[pl.ds(dst_ref[base + r0 + j], 1)], ssem.at[slot]).start()
            return carry
        lax.fori_loop(0, MOE_TILE // ROWS_PER_ISSUE, rows, 0)

    @pl.when(i >= n_valid)
    def _():
        zbuf[...] = jnp.zeros_like(zbuf)

        def row(r, carry):
            pltpu.make_async_copy(zbuf.at[pl.ds(r, 1)], y_hbm.at[pl.ds(dst_ref[i * MOE_TILE + r], 1)],
                                  zsem.at[0]).start()
            return carry
        lax.fori_loop(0, MOE_TILE, row, 0, unroll=8)
        pltpu.make_async_copy(zbuf, zbuf, zsem.at[0]).wait()

    @pl.when(i == pl.num_programs(0) - 1)
    def _():
        @pl.when(n_valid >= 1)
        def _():
            wait_rows(obuf, ssem, (n_valid - 1) % 2)

        @pl.when(n_valid >= 2)
        def _():
            wait_rows(obuf, ssem, n_valid % 2)


def expert_ffn(h, src_row, dst_row, tile_expert, n_valid, layer, w_gate, w_up, w_down):
    _, d = h.shape
    p = src_row.shape[0]
    n_tiles = p // MOE_TILE
    grid_spec = pltpu.PrefetchScalarGridSpec(
        num_scalar_prefetch=4,
        grid=(n_tiles,),
        in_specs=[
            pl.BlockSpec(memory_space=pl.ANY),
            pl.BlockSpec((None, None, d, EXPERT_FF), lambda i, s, t, te, nv: (layer, te[i], 0, 0)),
            pl.BlockSpec((None, None, d, EXPERT_FF), lambda i, s, t, te, nv: (layer, te[i], 0, 0)),
            pl.BlockSpec((None, None, EXPERT_FF, d), lambda i, s, t, te, nv: (layer, te[i], 0, 0)),
        ],
        out_specs=pl.BlockSpec(memory_space=pl.ANY),
        scratch_shapes=[pltpu.VMEM((2, MOE_TILE, d), jnp.float32), pltpu.VMEM((2, MOE_TILE, d), jnp.float32),
                        pltpu.VMEM((MOE_TILE, d), jnp.float32),
                        pltpu.SemaphoreType.DMA((2,)), pltpu.SemaphoreType.DMA((2,)), pltpu.SemaphoreType.DMA((1,))],
    )
    return pl.pallas_call(
        _expert_kernel,
        grid_spec=grid_spec,
        out_shape=jax.ShapeDtypeStruct((p, d), jnp.float32),
        compiler_params=pltpu.CompilerParams(
            dimension_semantics=("arbitrary",), vmem_limit_bytes=VMEM_LIMIT_BYTES),
        name="expert_ffn",
    )(src_row, dst_row, tile_expert, n_valid, h, w_gate, w_up, w_down)


def grouped_moe(h, logits, router_bias, layer, w_gate, w_up, w_down):
    t, d = h.shape
    aff = jax.nn.sigmoid(logits)
    biased = (aff + router_bias.astype(jnp.float32)).reshape(-1, N_GROUPS, EXPERTS_PER_GROUP)
    idx = jnp.arange(EXPERTS_PER_GROUP, dtype=jnp.int32)
    vi, vj = biased[..., :, None], biased[..., None, :]
    ahead = (vj > vi) | ((vj == vi) & (idx[None, :] < idx[:, None]))
    rank_in_group = jnp.sum(ahead.astype(jnp.int32), axis=-1)
    group_score = jnp.sum(jnp.where(rank_in_group < TOP_K, biased, 0.0), axis=-1)
    group = jnp.argmax(group_score, axis=-1)
    group_hot = group[:, None] == jnp.arange(N_GROUPS)[None, :]
    rank_sel = jnp.sum(jnp.where(group_hot[:, :, None], rank_in_group, 0), axis=1)
    local = jnp.sum(jnp.where(rank_sel[:, None, :] == jnp.arange(TOP_K)[None, :, None], idx[None, None, :], 0), axis=-1)
    expert = (group[:, None] * EXPERTS_PER_GROUP + local).astype(jnp.int32)
    expert_hot = expert[:, :, None] == jnp.arange(N_EXPERTS, dtype=jnp.int32)[None, None, :]
    wsel = jnp.sum(jnp.where(expert_hot, aff[:, None, :], 0.0), axis=-1)
    wsel = wsel / jnp.sum(wsel, axis=-1, keepdims=True)

    flat_e = expert.reshape(-1)
    onehot = (flat_e[:, None] == jnp.arange(N_EXPERTS, dtype=jnp.int32)[None, :]).astype(jnp.int32)
    csum = jnp.cumsum(onehot, axis=0)
    rank = jnp.sum(csum * onehot, axis=1) - 1
    counts = csum[-1]
    tiles_per = (counts + MOE_TILE - 1) // MOE_TILE
    tile_end = jnp.cumsum(tiles_per)
    tile_start = tile_end - tiles_per
    pos = jnp.sum(onehot * tile_start[None, :], axis=1) * MOE_TILE + rank
    n_tiles = (t * TOP_K) // MOE_TILE + N_EXPERTS
    p = n_tiles * MOE_TILE
    n_assign = t * TOP_K
    assign_of = jnp.full((p,), -1, jnp.int32).at[pos].set(jnp.arange(n_assign, dtype=jnp.int32))
    is_pad = assign_of < 0
    src_row = jnp.where(is_pad, 0, assign_of // TOP_K)
    dst_row = jnp.where(is_pad, n_assign + jnp.cumsum(is_pad.astype(jnp.int32)) - 1,
                        (assign_of % TOP_K) * t + assign_of // TOP_K)
    tile_ids = jnp.arange(n_tiles, dtype=jnp.int32)
    n_valid = tile_end[-1].astype(jnp.int32)
    tile_expert = jnp.minimum(jnp.sum((tile_end[None, :] <= tile_ids[:, None]).astype(jnp.int32), axis=1), N_EXPERTS - 1)
    last_expert = tile_expert[jnp.maximum(n_valid - 1, 0)]
    tile_expert = jnp.where(tile_ids < n_valid, tile_expert, last_expert)

    y = expert_ffn(h, src_row, dst_row, tile_expert, n_valid.reshape(1), layer, w_gate, w_up, w_down)
    return y, wsel


def axial_rope_tables(n_tokens):
    t = jnp.arange(n_tokens)
    row = (t // GRID_W).astype(jnp.float32)
    col = (t % GRID_W).astype(jnp.float32)
    n_freq = HEAD_DIM // 4
    inv = ROPE_THETA ** (-jnp.arange(n_freq, dtype=jnp.float32) / n_freq)
    ang = jnp.concatenate([row[:, None] * inv, col[:, None] * inv], axis=-1)
    cos, sin = jnp.cos(ang), jnp.sin(ang)
    cos_t = jnp.tile(jnp.concatenate([cos, cos], axis=-1), (1, NA_HEADS))
    sin_t = jnp.tile(jnp.concatenate([-sin, sin], axis=-1), (1, NA_HEADS))
    return cos_t, sin_t


PROJ_TILE = 512
PROJ_COLS = 512
REST_WIDTH = GATE_OFF - POOL_OFF
GATE_WIDTH = N_BRANCH * D_MODEL
PROJ_VMEM_LIMIT = 56 * 1024 * 1024


def _swap_head_halves(x):
    w = x.shape[-1]
    lane = lax.broadcasted_iota(jnp.int32, (1, w), 1)
    lower = (lane % HEAD_DIM) < HEAD_DIM // 2
    return jnp.where(lower, pltpu.roll(x, w - HEAD_DIM // 2, axis=1), pltpu.roll(x, HEAD_DIM // 2, axis=1))


def _proj_kernel(x_ref, g_ref, shift_ref, scale_ref, w_ref, *refs, rope):
    if rope:
        cos_ref, sin_ref, qr_ref, qp_ref, k_ref, v_ref, rest_ref, gate_ref = refs
    else:
        qp_ref, k_ref, v_ref, rest_ref, gate_ref = refs
    x = x_ref[...]
    y = x * lax.rsqrt(jnp.mean(x * x, axis=-1, keepdims=True) + RMS_EPS) * g_ref[...]
    h = (y * (1.0 + scale_ref[...]) + shift_ref[...]).astype(jnp.bfloat16)

    def cols(c0, width=PROJ_COLS):
        return jnp.dot(h, w_ref[:, c0:c0 + width], preferred_element_type=jnp.float32)

    q = cols(0) * (HEAD_DIM ** -0.5)
    k = cols(BRANCH_WIDTH)
    qp_ref[...] = q.astype(jnp.bfloat16)
    if rope:
        cos_t, sin_t = cos_ref[...], sin_ref[...]
        qr_ref[...] = (q * cos_t + _swap_head_halves(q) * sin_t).astype(jnp.bfloat16)
        k = k * cos_t + _swap_head_halves(k) * sin_t
    k_ref[...] = k.astype(jnp.bfloat16)
    v_ref[...] = cols(2 * BRANCH_WIDTH).astype(jnp.bfloat16)
    for c0 in range(0, REST_WIDTH, PROJ_COLS):
        rest_ref[:, c0:c0 + PROJ_COLS] = cols(POOL_OFF + c0)
    for c0 in range(0, GATE_WIDTH, PROJ_COLS):
        gate_ref[:, c0:c0 + PROJ_COLS] = jax.nn.sigmoid(cols(GATE_OFF + c0)).astype(jnp.bfloat16)


def input_projection(x, g, shift, scale, w, rope_tables=None):
    b, n, d = x.shape
    tm = min(PROJ_TILE, n)
    rope = rope_tables is not None
    per_batch = shift.shape[0] > 1
    mod_spec = pl.BlockSpec((None, 1, d), (lambda i, j: (i, 0, 0)) if per_batch else (lambda i, j: (0, 0, 0)))
    tok = lambda width: pl.BlockSpec((None, tm, width), lambda i, j: (i, j, 0))
    in_specs = [tok(d), pl.BlockSpec((1, d), lambda i, j: (0, 0)), mod_spec, mod_spec,
                pl.BlockSpec((d, IN_WIDTH), lambda i, j: (0, 0), pipeline_mode=pl.Buffered(1))]
    args = [x, g.reshape(1, d), shift[:, None, :], scale[:, None, :], w]
    n_qkv = 3
    if rope:
        in_specs += [pl.BlockSpec((tm, BRANCH_WIDTH), lambda i, j: (j, 0))] * 2
        args += list(rope_tables)
        n_qkv = 4
    bf = jnp.bfloat16
    out_shape = ([jax.ShapeDtypeStruct((b, n, BRANCH_WIDTH), bf)] * n_qkv
                 + [jax.ShapeDtypeStruct((b, n, REST_WIDTH), jnp.float32), jax.ShapeDtypeStruct((b, n, GATE_WIDTH), bf)])
    out_specs = [tok(BRANCH_WIDTH)] * n_qkv + [tok(REST_WIDTH), tok(GATE_WIDTH)]
    return pl.pallas_call(
        functools.partial(_proj_kernel, rope=rope),
        grid=(b, n // tm),
        in_specs=in_specs,
        out_specs=out_specs,
        out_shape=out_shape,
        compiler_params=pltpu.CompilerParams(
            dimension_semantics=("parallel", "parallel"), vmem_limit_bytes=PROJ_VMEM_LIMIT),
        name="input_projection",
    )(*args)


NA_GROUP = 4
NA_SLAB = NA_GROUP + NA_KH_MAX
NA_HEADS_PER_STEP = 128 // HEAD_DIM


def _natten_bias_patterns(rpb, rows):
    j = jnp.arange(GRID_W)
    col_start = jnp.clip(j - NA_KW // 2, 0, GRID_W - NA_KW)
    col_mask = (j[None, :] >= col_start[:, None]) & (j[None, :] < col_start[:, None] + NA_KW)
    dc = jnp.clip(j[None, :] - j[:, None], -(NA_KW - 1), NA_KW - 1) + NA_KW - 1
    dc_hot = (dc[None, :, :] == jnp.arange(2 * NA_KW - 1)[:, None, None]).astype(jnp.float32)
    by_col = jnp.einsum('hrc,cqk->hrqk', rpb.astype(jnp.float32), dc_hot, precision=lax.Precision.HIGHEST)
    by_col = jnp.where(col_mask[None, None], by_col, NEG_INF)
    masked = jnp.full((NA_HEADS, GRID_W, GRID_W), NEG_INF, jnp.float32)
    n_groups = rows // NA_GROUP
    patterns = []
    for g in (0, 1, n_groups - 1):
        slab0 = min(max(g * NA_GROUP - NA_KH_MAX // 2, 0), rows - NA_SLAB)
        q_rows = []
        for r in range(g * NA_GROUP, (g + 1) * NA_GROUP):
            win0 = min(max(r - NA_KH_MAX // 2, 0), rows - NA_KH_MAX)
            blocks = [by_col[:, kr - r + NA_KH_MAX - 1] if win0 <= kr < win0 + NA_KH_MAX else masked
                      for kr in range(slab0, slab0 + NA_SLAB)]
            q_rows.append(jnp.concatenate(blocks, axis=-1))
        patterns.append(jnp.concatenate(q_rows, axis=1))
    return jnp.stack(patterns, axis=1)


def _softmax_pv(s_lat, s_ctx, v_lat, v_ctx):
    m = jnp.max(s_lat, axis=-1, keepdims=True)
    if s_ctx is not None:
        m = jnp.maximum(m, jnp.max(s_ctx, axis=-1, keepdims=True))
    p_lat = jnp.exp(s_lat - m)
    den = jnp.sum(p_lat, axis=-1, keepdims=True)
    o = jnp.dot(p_lat.astype(jnp.bfloat16), v_lat, preferred_element_type=jnp.float32)
    if s_ctx is not None:
        p_ctx = jnp.exp(s_ctx - m)
        den = den + jnp.sum(p_ctx, axis=-1, keepdims=True)
        o = o + jnp.dot(p_ctx.astype(jnp.bfloat16), v_ctx, preferred_element_type=jnp.float32)
    return o / den


def _head_masks():
    lane = lax.broadcasted_iota(jnp.int32, (1, NA_HEADS_PER_STEP * HEAD_DIM), 1)
    return [(lane >= h * HEAD_DIM) & (lane < (h + 1) * HEAD_DIM) for h in range(NA_HEADS_PER_STEP)]


def _natten_kernel(qr_ref, qp_ref, kr_ref, v_ref, kc_ref, vc_ref, bias_ref, o_ref):
    rows = qr_ref.shape[0] // GRID_W
    n_groups = rows // NA_GROUP
    nq = NA_GROUP * GRID_W
    nk = NA_SLAB * GRID_W
    kc = kc_ref[...]
    vc = vc_ref[...]
    nt = (((1,), (1,)), ((), ()))
    masks = _head_masks()

    def body(g, carry):
        slab0 = jnp.clip(g * NA_GROUP - NA_KH_MAX // 2, 0, rows - NA_SLAB)
        pattern = jnp.where(g == 0, 0, jnp.where(g == n_groups - 1, 2, 1))
        q0 = pl.multiple_of(g * nq, nq)
        k0 = pl.multiple_of(slab0 * GRID_W, GRID_W)
        qr = qr_ref[pl.ds(q0, nq), :]
        qp = qp_ref[pl.ds(q0, nq), :]
        ks = kr_ref[pl.ds(k0, nk), :]
        vs = v_ref[pl.ds(k0, nk), :]
        out = None
        for h, mask in enumerate(masks):
            s_lat = lax.dot_general(jnp.where(mask, qr, 0), ks, nt, preferred_element_type=jnp.float32)
            s_lat = s_lat + bias_ref[h, pattern]
            s_ctx = lax.dot_general(jnp.where(mask, qp, 0), kc, nt, preferred_element_type=jnp.float32)
            o = _softmax_pv(s_lat, s_ctx, vs, vc)
            out = o if out is None else jnp.where(mask, o, out)
        o_ref[pl.ds(q0, nq), :] = out.astype(o_ref.dtype)
        return carry

    lax.fori_loop(0, n_groups, body, 0)


def neighbourhood_attention(q_rot, q_plain, k_rot, v, k_ctx, v_ctx, rpb):
    b, s, width = q_rot.shape
    n_ctx = k_ctx.shape[1]
    lanes = NA_HEADS_PER_STEP * HEAD_DIM
    bias = _natten_bias_patterns(rpb, s // GRID_W)
    lat = pl.BlockSpec((None, s, lanes), lambda i, j: (i, 0, j))
    cx = pl.BlockSpec((None, n_ctx, lanes), lambda i, j: (i, 0, j))
    return pl.pallas_call(
        _natten_kernel,
        grid=(b, width // lanes),
        in_specs=[lat, lat, lat, lat, cx, cx,
                  pl.BlockSpec((NA_HEADS_PER_STEP, 3, NA_GROUP * GRID_W, NA_SLAB * GRID_W), lambda i, j: (j, 0, 0, 0))],
        out_specs=lat,
        out_shape=jax.ShapeDtypeStruct((b, s, width), jnp.float32),
        compiler_params=pltpu.CompilerParams(
            dimension_semantics=("parallel", "parallel"), vmem_limit_bytes=VMEM_LIMIT_BYTES),
        name="natten",
    )(q_rot, q_plain, k_rot, v, k_ctx, v_ctx, bias)


def _ctx_attn_kernel(q_ref, k_ref, v_ref, o_ref):
    nt = (((1,), (1,)), ((), ()))
    q, k, v = q_ref[...], k_ref[...], v_ref[...]
    out = None
    for mask in _head_masks():
        s = lax.dot_general(jnp.where(mask, q, 0), k, nt, preferred_element_type=jnp.float32)
        o = _softmax_pv(s, None, v, None)
        out = o if out is None else jnp.where(mask, o, out)
    o_ref[...] = out.astype(o_ref.dtype)


def context_attention(q, k, v):
    b, n, width = q.shape
    lanes = NA_HEADS_PER_STEP * HEAD_DIM
    spec = pl.BlockSpec((None, n, lanes), lambda i, j: (i, 0, j))
    return pl.pallas_call(
        _ctx_attn_kernel,
        grid=(b, width // lanes),
        in_specs=[spec, spec, spec],
        out_specs=spec,
        out_shape=jax.ShapeDtypeStruct((b, n, width), jnp.float32),
        compiler_params=pltpu.CompilerParams(dimension_semantics=("parallel", "parallel")),
        name="ctx_attention",
    )(q, k, v)


POOL_PAD = 16


def _pool_kernel(u_ref, w_ref, scale_ref, o_ref, pad_ref):
    n = u_ref.shape[0]
    u = u_ref[...]
    pad_ref[pl.ds(0, POOL_PAD), :] = jnp.zeros((POOL_PAD, POOL_GROUP_DIM), jnp.float32)
    pad_ref[pl.ds(POOL_PAD + n, POOL_PAD), :] = jnp.zeros((POOL_PAD, POOL_GROUP_DIM), jnp.float32)
    pad_ref[pl.ds(POOL_PAD, n), :] = u
    t = lax.broadcasted_iota(jnp.int32, (n, 1), 0)
    for g, w in enumerate(POOL_WINDOWS):
        @pl.when(pl.program_id(1) == g)
        def _(w=w):
            first = POOL_PAD - w // 2
            acc = pad_ref[pl.ds(first, n), :]
            for k in range(1, w):
                acc = acc + pad_ref[pl.ds(first + k, n), :]
            cnt = jnp.minimum(t + (w - w // 2), n) - jnp.maximum(t - w // 2, 0)
            y = acc / cnt.astype(jnp.float32) - u
            y = jnp.dot(y.astype(jnp.bfloat16), w_ref[...].astype(jnp.bfloat16), preferred_element_type=jnp.float32)
            o_ref[...] = y * scale_ref[...]


def multiscale_pool(rest, pool_w, pool_scale):
    b, n, _ = rest.shape
    col0 = 0
    return pl.pallas_call(
        _pool_kernel,
        grid=(b, POOL_GROUPS),
        in_specs=[pl.BlockSpec((None, n, POOL_GROUP_DIM), lambda i, g: (i, 0, col0 + g)),
                  pl.BlockSpec((None, POOL_GROUP_DIM, POOL_GROUP_DIM), lambda i, g: (g, 0, 0)),
                  pl.BlockSpec((1, POOL_GROUP_DIM), lambda i, g: (0, g))],
        out_specs=pl.BlockSpec((None, n, POOL_GROUP_DIM), lambda i, g: (i, 0, g)),
        out_shape=jax.ShapeDtypeStruct((b, n, BRANCH_WIDTH), jnp.float32),
        scratch_shapes=[pltpu.VMEM((n + 2 * POOL_PAD, POOL_GROUP_DIM), jnp.float32)],
        compiler_params=pltpu.CompilerParams(
            dimension_semantics=("parallel", "parallel"), vmem_limit_bytes=VMEM_LIMIT_BYTES),
        name="multiscale_pool",
    )(rest, pool_w, pool_scale.reshape(1, BRANCH_WIDTH))


HY_CHUNK = 64
HY_PITCH = 72
HY_LANES = 128
HY_FEAT = 128
HY_VMEM_LIMIT = 56 * 1024 * 1024


def _cis_tables(num, den):
    ang = (-2.0 * math.pi / den) * (num % den).astype(jnp.float32)
    return jnp.cos(ang), jnp.sin(ang)


def _stack_complex(mr, mi):
    return jnp.concatenate([jnp.concatenate([mr, -mi], axis=-1), jnp.concatenate([mi, mr], axis=-1)], axis=-2)


def _hy_dft_tables(n):
    big = 2 * n
    n1_full = big // HY_CHUNK
    k1 = jnp.arange(n1_full, dtype=jnp.int32)
    n1 = jnp.arange(n1_full, dtype=jnp.int32)
    n2 = jnp.arange(HY_CHUNK, dtype=jnp.int32)
    mr, mi = _cis_tables(k1[None, :, None] * (HY_CHUNK * n1[None, None, :] + n2[:, None, None]), big)
    half = n1_full // 2
    fwd_a = _stack_complex(mr[:, :, :half], mi[:, :, :half])
    fwd_a_real = jnp.concatenate([mr, mi], axis=-2)
    mrt = jnp.transpose(mr[:, :, :half], (0, 2, 1))
    mit = -jnp.transpose(mi[:, :, :half], (0, 2, 1))
    inv_a = _stack_complex(mrt, mit)
    er, ei = _cis_tables(n2[:, None] * n2[None, :], HY_CHUNK)
    fwd_b = _stack_complex(er, ei)
    inv_b = _stack_complex(er, -ei)
    bf = jnp.bfloat16
    return fwd_a.astype(bf), fwd_a_real.astype(bf), inv_a.astype(bf), fwd_b.astype(bf), inv_b.astype(bf)


def _hy_positions(n):
    f32 = jnp.float32
    t = jnp.linspace(0.0, 1.0, n, dtype=f32)[:, None]
    bands = (HY_EMB_DIM - 1) // 2
    ang = (2.0 * math.pi / n) * jnp.arange(n, dtype=f32)[:, None]
    f = jnp.linspace(1e-4, bands - 1, bands, dtype=f32)[None, :]
    z = jnp.concatenate([t, jnp.cos(f * ang), -jnp.sin(f * ang)], axis=-1)
    z = jnp.pad(z, ((0, 0), (0, HY_FEAT - HY_EMB_DIM)))
    zb = jnp.concatenate([z[:1], z[:0:-1]], axis=0)
    return z, zb


def _filter_mlp(z, w1, b1, fr, w2, b2):
    hp = lax.Precision.HIGHEST
    h = jnp.sin(fr * (jnp.dot(z, w1, precision=hp, preferred_element_type=jnp.float32) + b1))
    return jnp.sin(fr * (jnp.dot(h, w2, precision=hp, preferred_element_type=jnp.float32) + b2))


def _filter_taps(zf_ref, zb_ref, w1_ref, b1_ref, fr_ref, w2_ref, b2_ref, w3f_ref, w3b_ref, delta_ref, hid_ref):
    n = zf_ref.shape[0]
    hp = lax.Precision.HIGHEST

    @pl.when((pl.program_id(1) == 0) & (pl.program_id(2) == 0))
    def _():
        args = (w1_ref[...], b1_ref[...], fr_ref[...], w2_ref[...], b2_ref[...])
        hid_ref[0] = _filter_mlp(zf_ref[...], *args)
        hid_ref[1] = _filter_mlp(zb_ref[...], *args)

    delta = delta_ref[...]
    hf = jnp.dot(hid_ref[0], w3f_ref[...], precision=hp, preferred_element_type=jnp.float32)
    hf = hf * jnp.exp(-zf_ref[:, 0:1] * delta)
    hb = jnp.dot(hid_ref[1], w3b_ref[...], precision=hp, preferred_element_type=jnp.float32)
    hb = hb * jnp.exp(-zb_ref[:, 0:1] * delta)
    row = lax.broadcasted_iota(jnp.int32, (n, 1), 0)
    hb = jnp.where(row > 0, hb, 0.0)
    norm = jnp.sum(jnp.abs(hf), axis=0, keepdims=True) + jnp.sum(jnp.abs(hb), axis=0, keepdims=True)
    return hf / norm, hb / norm


def _filter_spec_kernel(zf_ref, zb_ref, w1_ref, b1_ref, fr_ref, w2_ref, b2_ref, w3f_ref, w3b_ref, delta_ref,
                        fa_ref, fb_ref, hr_ref, hi_ref, hid_ref, tbuf, sbuf):
    n = zf_ref.shape[0]
    chunks = n // HY_CHUNK
    n1_full = 2 * chunks
    hf, hb = _filter_taps(zf_ref, zb_ref, w1_ref, b1_ref, fr_ref, w2_ref, b2_ref, w3f_ref, w3b_ref, delta_ref, hid_ref)
    for c in range(chunks):
        tbuf[pl.ds(c * HY_PITCH, HY_CHUNK), :] = hf[c * HY_CHUNK:(c + 1) * HY_CHUNK]
        tbuf[pl.ds((chunks + c) * HY_PITCH, HY_CHUNK), :] = hb[c * HY_CHUNK:(c + 1) * HY_CHUNK]

    def stage_a(n2, carry):
        rows = tbuf[pl.ds(n2, n1_full, stride=HY_PITCH), :].astype(jnp.bfloat16)
        a = jnp.dot(fa_ref[n2], rows, preferred_element_type=jnp.float32)
        sbuf[0, pl.ds(n2, n1_full, stride=HY_PITCH), :] = a[:n1_full]
        sbuf[1, pl.ds(n2, n1_full, stride=HY_PITCH), :] = a[n1_full:]
        return carry

    lax.fori_loop(0, HY_CHUNK, stage_a, 0, unroll=2)
    inv_len = 1.0 / (2 * n)

    def stage_b(k1, carry):
        r0 = pl.multiple_of(k1 * HY_PITCH, 8)
        x = jnp.concatenate([sbuf[0, pl.ds(r0, HY_CHUNK), :], sbuf[1, pl.ds(r0, HY_CHUNK), :]], axis=0)
        y = jnp.dot(fb_ref[...], x.astype(jnp.bfloat16), preferred_element_type=jnp.float32) * inv_len
        o0 = pl.multiple_of(k1 * HY_CHUNK, HY_CHUNK)
        hr_ref[pl.ds(o0, HY_CHUNK), :] = y[:HY_CHUNK]
        hi_ref[pl.ds(o0, HY_CHUNK), :] = y[HY_CHUNK:]
        return carry

    lax.fori_loop(0, n1_full, stage_b, 0, unroll=4)


def hyena_filter_spectra(n, hy_w1, hy_b1, hy_freq, hy_w2, hy_b2, hy_w3, tables):
    depth = hy_w1.shape[0]
    zf, zb = _hy_positions(n)
    tiles = BRANCH_WIDTH // HY_LANES
    w1 = jnp.pad(hy_w1, ((0, 0), (0, HY_FEAT - HY_EMB_DIM), (0, 0)))
    deltas = jnp.abs(jnp.linspace(HY_MIN_DECAY, HY_MAX_DECAY, BRANCH_WIDTH, dtype=jnp.float32)).reshape(1, BRANCH_WIDTH)
    fwd_a_real, fwd_b = tables[1], tables[3]
    n1_full = 2 * n // HY_CHUNK
    one = pl.Buffered(1)
    full = lambda shape: pl.BlockSpec(shape, lambda l, o, j: (0,) * len(shape), pipeline_mode=one)
    per_layer = lambda shape: pl.BlockSpec((None,) + shape, lambda l, o, j: (l,) + (0,) * len(shape))
    out_spec = pl.BlockSpec((None, None, 2 * n, HY_LANES), lambda l, o, j: (l, o, 0, j), pipeline_mode=one)
    out_sds = jax.ShapeDtypeStruct((depth, HY_ORDER, 2 * n, BRANCH_WIDTH), jnp.float32)
    return pl.pallas_call(
        _filter_spec_kernel,
        grid=(depth, HY_ORDER, tiles),
        in_specs=[full((n, HY_FEAT)), full((n, HY_FEAT)),
                  per_layer((HY_FEAT, HY_FILTER_HIDDEN)), per_layer((1, HY_FILTER_HIDDEN)),
                  per_layer((1, HY_FILTER_HIDDEN)),
                  per_layer((HY_FILTER_HIDDEN, HY_FILTER_HIDDEN)), per_layer((1, HY_FILTER_HIDDEN)),
                  pl.BlockSpec((None, HY_FILTER_HIDDEN, HY_LANES), lambda l, o, j: (l, 0, o * 2 * tiles + j)),
                  pl.BlockSpec((None, HY_FILTER_HIDDEN, HY_LANES), lambda l, o, j: (l, 0, o * 2 * tiles + tiles + j)),
                  pl.BlockSpec((1, HY_LANES), lambda l, o, j: (0, j)),
                  full((HY_CHUNK, 2 * n1_full, n1_full)), full((2 * HY_CHUNK, 2 * HY_CHUNK))],
        out_specs=[out_spec, out_spec],
        out_shape=[out_sds, out_sds],
        scratch_shapes=[pltpu.VMEM((2, n, HY_FILTER_HIDDEN), jnp.float32),
                        pltpu.VMEM((n1_full * HY_PITCH, HY_LANES), jnp.float32),
                        pltpu.VMEM((2, n1_full * HY_PITCH, HY_LANES), jnp.float32)],
        compiler_params=pltpu.CompilerParams(
            dimension_semantics=("arbitrary", "arbitrary", "arbitrary"), vmem_limit_bytes=HY_VMEM_LIMIT),
        name="hyena_filter_spectra",
    )(zf, zb, w1, hy_b1[:, None, :], hy_freq[:, None, :], hy_w2, hy_b2[:, None, :], hy_w3, hy_w3, deltas,
      fwd_a_real, fwd_b)


def _short_conv(u_ref, b, w_ref, bias_ref, pad_ref):
    n = u_ref.shape[1]
    x = u_ref[b]
    pad_ref[pl.ds(0, 8), :] = jnp.zeros((8, HY_LANES), jnp.float32)
    pad_ref[pl.ds(8 + n, 8), :] = jnp.zeros((8, HY_LANES), jnp.float32)
    pad_ref[pl.ds(8, n), :] = x
    return (pad_ref[pl.ds(7, n), :] * w_ref[0:1, :] + x * w_ref[1:2, :] + pad_ref[pl.ds(9, n), :] * w_ref[2:3, :]
            + bias_ref[...])


def _hyena_conv_kernel(z_ref, g_ref, wz_ref, bz_ref, wg_ref, bg_ref, skip_ref, hr_ref, hi_ref,
                       fa_ref, ia_ref, fb_ref, ib_ref, o_ref, pad_ref, zbuf, sbuf, *, conv_z):
    n = z_ref.shape[1]
    chunks = n // HY_CHUNK
    n1_full = 2 * chunks
    for b in range(2):
        z = _short_conv(z_ref, b, wz_ref, bz_ref, pad_ref) if conv_z else z_ref[b]
        for c in range(chunks):
            zbuf[b, pl.ds(c * HY_PITCH, HY_CHUNK), :] = z[c * HY_CHUNK:(c + 1) * HY_CHUNK]

    def stage_a(n2, carry):
        x = jnp.concatenate([zbuf[0, pl.ds(n2, chunks, stride=HY_PITCH), :],
                             zbuf[1, pl.ds(n2, chunks, stride=HY_PITCH), :]], axis=0)
        a = jnp.dot(fa_ref[n2], x.astype(jnp.bfloat16), preferred_element_type=jnp.float32)
        sbuf[0, pl.ds(n2, n1_full, stride=HY_PITCH), :] = a[:n1_full]
        sbuf[1, pl.ds(n2, n1_full, stride=HY_PITCH), :] = a[n1_full:]
        return carry

    lax.fori_loop(0, HY_CHUNK, stage_a, 0, unroll=2)

    def stage_b(k1, carry):
        r0 = pl.multiple_of(k1 * HY_PITCH, 8)
        x = jnp.concatenate([sbuf[0, pl.ds(r0, HY_CHUNK), :], sbuf[1, pl.ds(r0, HY_CHUNK), :]], axis=0)
        s = jnp.dot(fb_ref[...], x.astype(jnp.bfloat16), preferred_element_type=jnp.float32)
        h0 = pl.multiple_of(k1 * HY_CHUNK, HY_CHUNK)
        hr = hr_ref[pl.ds(h0, HY_CHUNK), :]
        hi = hi_ref[pl.ds(h0, HY_CHUNK), :]
        sr, si = s[:HY_CHUNK], s[HY_CHUNK:]
        y = jnp.concatenate([sr * hr - si * hi, sr * hi + si * hr], axis=0)
        c = jnp.dot(ib_ref[...], y.astype(jnp.bfloat16), preferred_element_type=jnp.float32)
        sbuf[0, pl.ds(r0, HY_CHUNK), :] = c[:HY_CHUNK]
        sbuf[1, pl.ds(r0, HY_CHUNK), :] = c[HY_CHUNK:]
        return carry

    lax.fori_loop(0, n1_full, stage_b, 0, unroll=8)
    skip = skip_ref[...]

    def stage_c(n2, carry):
        x = jnp.concatenate([sbuf[0, pl.ds(n2, n1_full, stride=HY_PITCH), :],
                             sbuf[1, pl.ds(n2, n1_full, stride=HY_PITCH), :]], axis=0)
        y = jnp.dot(ia_ref[n2], x.astype(jnp.bfloat16), preferred_element_type=jnp.float32)
        for b in range(2):
            zb = zbuf[b, pl.ds(n2, chunks, stride=HY_PITCH), :]
            zbuf[b, pl.ds(n2, chunks, stride=HY_PITCH), :] = y[b * chunks:(b + 1) * chunks] + zb * skip
        return carry

    lax.fori_loop(0, HY_CHUNK, stage_c, 0, unroll=4)
    for b in range(2):
        g = _short_conv(g_ref, b, wg_ref, bg_ref, pad_ref)
        for c in range(chunks):
            o_ref[b, pl.ds(c * HY_CHUNK, HY_CHUNK), :] = (
                g[c * HY_CHUNK:(c + 1) * HY_CHUNK] * zbuf[b, pl.ds(c * HY_PITCH, HY_CHUNK), :])


def hyena_long_conv(z, z_col0, g, g_col0, conv_z, short_w, short_b, w_off_z, w_off_g, skip, hr, hi, layer, order,
                    tables):
    b, n, _ = g.shape
    tiles = BRANCH_WIDTH // HY_LANES
    fwd_a, _, inv_a, fwd_b, inv_b = tables
    n1_full = 2 * n // HY_CHUNK
    one = pl.Buffered(1)
    zc, gc = z_col0 // HY_LANES, g_col0 // HY_LANES
    wz, wg = w_off_z // HY_LANES, w_off_g // HY_LANES
    const = lambda shape: pl.BlockSpec(shape, lambda j, p: (0,) * len(shape), pipeline_mode=one)
    return pl.pallas_call(
        functools.partial(_hyena_conv_kernel, conv_z=conv_z),
        grid=(tiles, b // 2),
        in_specs=[pl.BlockSpec((2, n, HY_LANES), lambda j, p: (p, 0, zc + j), pipeline_mode=one),
                  pl.BlockSpec((2, n, HY_LANES), lambda j, p: (p, 0, gc + j), pipeline_mode=one),
                  pl.BlockSpec((HY_SHORT, HY_LANES), lambda j, p: (0, wz + j)),
                  pl.BlockSpec((1, HY_LANES), lambda j, p: (0, wz + j)),
                  pl.BlockSpec((HY_SHORT, HY_LANES), lambda j, p: (0, wg + j)),
                  pl.BlockSpec((1, HY_LANES), lambda j, p: (0, wg + j)),
                  pl.BlockSpec((1, HY_LANES), lambda j, p: (0, j)),
                  pl.BlockSpec((None, None, 2 * n, HY_LANES), lambda j, p: (layer, order, 0, j), pipeline_mode=one),
                  pl.BlockSpec((None, None, 2 * n, HY_LANES), lambda j, p: (layer, order, 0, j), pipeline_mode=one),
                  const((HY_CHUNK, 2 * n1_full, n1_full)), const((HY_CHUNK, n1_full, 2 * n1_full)),
                  const((2 * HY_CHUNK, 2 * HY_CHUNK)), const((2 * HY_CHUNK, 2 * HY_CHUNK))],
        out_specs=pl.BlockSpec((2, n, HY_LANES), lambda j, p: (p, 0, j), pipeline_mode=one),
        out_shape=jax.ShapeDtypeStruct((b, n, BRANCH_WIDTH), jnp.float32),
        scratch_shapes=[pltpu.VMEM((n + 16, HY_LANES), jnp.float32),
                        pltpu.VMEM((2, (n // HY_CHUNK) * HY_PITCH, HY_LANES), jnp.float32),
                        pltpu.VMEM((2, n1_full * HY_PITCH, HY_LANES), jnp.float32)],
        compiler_params=pltpu.CompilerParams(
            dimension_semantics=("parallel", "parallel"), vmem_limit_bytes=HY_VMEM_LIMIT),
        name="hyena_long_conv",
    )(z, g, short_w, short_b.reshape(1, -1), short_w, short_b.reshape(1, -1), skip.reshape(1, -1), hr, hi,
      fwd_a, inv_a, fwd_b, inv_b)


def hyena_mixer(rest, short_w, short_b, skip, hr, hi, layer, tables):
    off = HY_OFF - POOL_OFF
    z1 = hyena_long_conv(rest, off, rest, off + BRANCH_WIDTH, True, short_w, short_b, 0, BRANCH_WIDTH,
                         skip[0], hr, hi, layer, 0, tables)
    return hyena_long_conv(z1, 0, rest, off + 2 * BRANCH_WIDTH, False, short_w, short_b, 0, 2 * BRANCH_WIDTH,
                           skip[1], hr, hi, layer, 1, tables)


MERGE_TILE = 512


def _merge_kernel(ya_ref, yp_ref, yh_ref, gate_ref, x_ref, res_gate_ref, wb_ref, wo_ref, g2_ref, shift_ref, scale_ref,
                  wr_ref, xo_ref, h_ref, logit_ref):
    d = x_ref.shape[-1]
    merged = None
    for i, y_ref in enumerate((ya_ref, yp_ref, yh_ref)):
        br = jnp.dot(y_ref[...].astype(jnp.bfloat16), wb_ref[i], preferred_element_type=jnp.float32)
        term = gate_ref[:, i * d:(i + 1) * d].astype(jnp.float32) * br
        merged = term if merged is None else merged + term
    out = jnp.dot(merged.astype(jnp.bfloat16), wo_ref[...], preferred_element_type=jnp.float32)
    x = x_ref[...] + res_gate_ref[...] * out
    xo_ref[...] = x
    y = x * lax.rsqrt(jnp.mean(x * x, axis=-1, keepdims=True) + RMS_EPS) * g2_ref[...]
    h = y * (1.0 + scale_ref[...]) + shift_ref[...]
    h_ref[...] = h
    logit_ref[...] = jnp.dot(h, wr_ref[...], precision=lax.Precision.HIGHEST, preferred_element_type=jnp.float32)


def merge_and_norm(y_attn, y_pool, y_hy, gates, x, mod, w_branch, w_out, g2, w_router):
    b, n, d = x.shape
    tm = min(MERGE_TILE, n)
    per_batch = mod.shape[0] > 1
    mod_spec = lambda k: pl.BlockSpec((None, None, 1, d),
                                      (lambda i, j: (i, k, 0, 0)) if per_batch else (lambda i, j: (0, k, 0, 0)))
    tok = lambda width: pl.BlockSpec((None, tm, width), lambda i, j: (i, j, 0))
    const = lambda shape: pl.BlockSpec(shape, lambda i, j: (0,) * len(shape))
    call = pl.pallas_call(
        _merge_kernel,
        grid=(b, n // tm),
        in_specs=[tok(BRANCH_WIDTH), tok(BRANCH_WIDTH), tok(BRANCH_WIDTH), tok(GATE_WIDTH), tok(d), mod_spec(2),
                  const((N_BRANCH, BRANCH_WIDTH, d)), const((d, d)), const((1, d)), mod_spec(3), mod_spec(4),
                  const((d, N_EXPERTS))],
        out_specs=[tok(d), tok(d), tok(N_EXPERTS)],
        out_shape=[jax.ShapeDtypeStruct((b, n, d), jnp.float32), jax.ShapeDtypeStruct((b, n, d), jnp.float32),
                   jax.ShapeDtypeStruct((b, n, N_EXPERTS), jnp.float32)],
        compiler_params=pltpu.CompilerParams(
            dimension_semantics=("parallel", "parallel"), vmem_limit_bytes=VMEM_LIMIT_BYTES),
        name="merge_and_norm",
    )
    mod4 = mod[:, :, None, :]
    return call(y_attn, y_pool, y_hy, gates, x, mod4, w_branch, w_out, g2.reshape(1, d), mod4, mod4, w_router)


def _combine_kernel(y0_ref, y1_ref, w_ref, x_ref, gate_ref, g_ref, o_ref, *, final_norm):
    w = w_ref[...]
    f = y0_ref[...] * w[:, 0:1] + y1_ref[...] * w[:, 1:2]
    x = x_ref[...] + gate_ref[...] * f
    if final_norm:
        x = x * lax.rsqrt(jnp.mean(x * x, axis=-1, keepdims=True) + RMS_EPS) * g_ref[...]
    o_ref[...] = x


def moe_combine(y, wsel, row0, x, mod, final_g=None):
    b, n, d = x.shape
    tm = min(MERGE_TILE, n)
    tiles = n // tm
    blk0 = row0 // tm
    second = wsel.shape[0] // tm
    per_batch = mod.shape[0] > 1
    tok = lambda width: pl.BlockSpec((None, tm, width), lambda i, j: (i, j, 0))
    flat = lambda width, off=0: pl.BlockSpec((tm, width), lambda i, j: (off + blk0 + i * tiles + j, 0))
    k = N_MOD - 1
    gate_spec = pl.BlockSpec((None, None, 1, d), (lambda i, j: (i, k, 0, 0)) if per_batch else (lambda i, j: (0, k, 0, 0)))
    g = jnp.ones((1, d), jnp.float32) if final_g is None else final_g.reshape(1, d)
    return pl.pallas_call(
        functools.partial(_combine_kernel, final_norm=final_g is not None),
        grid=(b, n // tm),
        in_specs=[flat(d), flat(d, second), flat(TOP_K), tok(d), gate_spec, pl.BlockSpec((1, d), lambda i, j: (0, 0))],
        out_specs=tok(d),
        out_shape=jax.ShapeDtypeStruct((b, n, d), jnp.float32),
        compiler_params=pltpu.CompilerParams(
            dimension_semantics=("parallel", "parallel"), vmem_limit_bytes=VMEM_LIMIT_BYTES),
        name="moe_combine",
    )(y, y, wsel, x, mod[:, :, None, :], g)


def latent_mixers(x, g1, mod, w_in, rope_tables, k_ctx, v_ctx, rpb, pool_w, pool_scale, hy, w_branch, w_out, g2, w_router):
    q_rot, q_plain, k, v, rest, gates = input_projection(x, g1, mod[:, 0], mod[:, 1], w_in, rope_tables)
    y_attn = neighbourhood_attention(q_rot, q_plain, k, v, k_ctx, v_ctx, rpb)
    y_pool = multiscale_pool(rest, pool_w, pool_scale)
    y_hy = hyena_mixer(rest, *hy)
    return merge_and_norm(y_attn, y_pool, y_hy, gates, x, mod, w_branch, w_out, g2, w_router)


def kernel(x, c, ctx, c_ctx, norm1_g, norm2_g, w_mod, b_mod, w_in, rpb, pool_w, pool_scale, hy_short_w, hy_short_b, hy_w1, hy_b1, hy_freq, hy_w2, hy_b2, hy_w3, hy_skip, w_branch, w_out, w_router, router_bias, w_gate_e, w_up_e, w_down_e, final_g):
    b, s, d = x.shape
    n_ctx = ctx.shape[1]
    bf = jnp.bfloat16
    rope_tables = axial_rope_tables(s)
    tables_l = _hy_dft_tables(s)
    tables_c = _hy_dft_tables(n_ctx)
    spec_l = hyena_filter_spectra(s, hy_w1, hy_b1, hy_freq, hy_w2, hy_b2, hy_w3, tables_l)
    spec_c = hyena_filter_spectra(n_ctx, hy_w1[:DEPTH - 1], hy_b1[:DEPTH - 1], hy_freq[:DEPTH - 1], hy_w2[:DEPTH - 1],
                                  hy_b2[:DEPTH - 1], hy_w3[:DEPTH - 1], tables_c)
    cond = jnp.concatenate([c, c_ctx[None, :], jnp.zeros((8 - (b + 1) % 8, d), c.dtype)], axis=0)
    cond = jax.nn.silu(cond)
    xl, xc = x, ctx
    for l in range(DEPTH):
        need_ctx = l < DEPTH - 1
        mod = (matmul(cond, w_mod[l]) + b_mod[l]).reshape(-1, N_MOD, d)
        ml, mc = mod[:b], mod[b:b + 1]
        w_in_l, w_branch_l, w_out_l = w_in[l].astype(bf), w_branch[l].astype(bf), w_out[l].astype(bf)
        common = (pool_w[l], pool_scale[l])
        hy_c = (hy_short_w[l], hy_short_b[l], hy_skip[l])
        qc, kc, vc, rest_c, gates_c = input_projection(xc, norm1_g[l], mc[:, 0], mc[:, 1], w_in_l)
        xl, hl2, logit_l = latent_mixers(xl, norm1_g[l], ml, w_in_l, rope_tables, kc, vc, rpb[l], *common,
                                         hy_c + (spec_l[0], spec_l[1], l, tables_l), w_branch_l, w_out_l,
                                         norm2_g[l], w_router)
        tokens, logits = hl2.reshape(b * s, d), logit_l.reshape(b * s, N_EXPERTS)
        if need_ctx:
            ya_c = context_attention(qc, kc, vc)
            yp_c = multiscale_pool(rest_c, *common)
            yh_c = hyena_mixer(rest_c, *hy_c, spec_c[0], spec_c[1], l, tables_c)
            xc, hc2, logit_c = merge_and_norm(ya_c, yp_c, yh_c, gates_c, xc, mc, w_branch_l, w_out_l, norm2_g[l], w_router)
            tokens = jnp.concatenate([tokens, hc2.reshape(b * n_ctx, d)], axis=0)
            logits = jnp.concatenate([logits, logit_c.reshape(b * n_ctx, N_EXPERTS)], axis=0)
        y, wsel = grouped_moe(tokens, logits, router_bias, l, w_gate_e, w_up_e, w_down_e)
        xl = moe_combine(y, wsel, 0, xl, ml, final_g if l == DEPTH - 1 else None)
        if need_ctx:
            xc = moe_combine(y, wsel, b * s, xc, mc)
    return xl
```

```python
import functools
import math

import jax
import jax.numpy as jnp
from jax import lax
from jax.experimental import pallas as pl
from jax.experimental.pallas import tpu as pltpu

D_MODEL = 1024
DEPTH = 4
GRID_W = 64
BRANCH_WIDTH = D_MODEL // 2
N_BRANCH = 3
HEAD_DIM = 64
NA_HEADS = BRANCH_WIDTH // HEAD_DIM
NA_KH_MAX = 8
NA_KW = 16
ROPE_THETA = 10000.0
NEG_INF = -1e30
POOL_WINDOWS = (2, 4, 8, 16)
POOL_GROUPS = len(POOL_WINDOWS)
POOL_GROUP_DIM = BRANCH_WIDTH // POOL_GROUPS
HY_ORDER = 2
HY_SHORT = 3
HY_EMB_DIM = 33
HY_FILTER_HIDDEN = 64
HY_MIN_DECAY = math.log(1e-2) / 0.3
HY_MAX_DECAY = math.log(1e-2) / 1.5
N_EXPERTS = 16
N_GROUPS = 4
EXPERTS_PER_GROUP = N_EXPERTS // N_GROUPS
TOP_K = 2
EXPERT_FF = D_MODEL // 2
N_MOD = 6
RMS_EPS = 1e-6
POOL_OFF = 3 * BRANCH_WIDTH
HY_OFF = POOL_OFF + BRANCH_WIDTH
GATE_OFF = HY_OFF + (HY_ORDER + 1) * BRANCH_WIDTH
IN_WIDTH = GATE_OFF + N_BRANCH * D_MODEL

VMEM_LIMIT_BYTES = 48 * 1024 * 1024
MOE_TILE = 256
ROWS_PER_ISSUE = 8


def _mm_kernel(a_ref, b_ref, o_ref):
    o_ref[...] = jnp.dot(a_ref[...].astype(jnp.bfloat16), b_ref[...].astype(jnp.bfloat16),
                         preferred_element_type=jnp.float32).astype(o_ref.dtype)


def _pick_tile(n, cands):
    for c in cands:
        if n % c == 0:
            return c
    return n


def matmul(a, b, out_dtype=jnp.float32):
    m, k = a.shape
    _, n = b.shape
    tm = _pick_tile(m, (1024, 512, 256, 128, 64, 32, 16, 8))
    tn = _pick_tile(n, (1664, 1024, 512, 256, 128))
    return pl.pallas_call(
        _mm_kernel,
        grid=(m // tm, n // tn),
        in_specs=[pl.BlockSpec((tm, k), lambda i, j: (i, 0)),
                  pl.BlockSpec((k, tn), lambda i, j: (0, j))],
        out_specs=pl.BlockSpec((tm, tn), lambda i, j: (i, j)),
        out_shape=jax.ShapeDtypeStruct((m, n), out_dtype),
        compiler_params=pltpu.CompilerParams(
            dimension_semantics=("parallel", "parallel"), vmem_limit_bytes=VMEM_LIMIT_BYTES),
        name="dense_matmul",
    )(a, b)


ROW_TILE = 8


def _rows_to_tiles(ref, idx, x):
    m = x.shape[0]
    for c in range(ROW_TILE):
        ref[idx + (pl.ds(c, m, stride=ROW_TILE), slice(None))] = x[:, c * 128:(c + 1) * 128]


def _tiles_to_rows(ref, idx, m):
    return jnp.concatenate([ref[idx + (pl.ds(c, m, stride=ROW_TILE), slice(None))] for c in range(ROW_TILE)], axis=1)


def _expert_kernel(src_ref, dst_ref, tile_expert_ref, n_valid_ref, h_hbm, wg_ref, wu_ref, wd_ref, y_hbm,
                   xbuf, obuf, zbuf, gsem, ssem, zsem):
    i = pl.program_id(0)
    n_valid = n_valid_ref[0]
    slot = i % 2

    def token_copies(index_ref, base, make):
        def tokens(g, carry):
            r0 = g * ROWS_PER_ISSUE
            for j in range(ROWS_PER_ISSUE):
                hbm_row = pl.multiple_of(index_ref[base + r0 + j], ROW_TILE)
                vmem_row = pl.multiple_of((r0 + j) * ROW_TILE, ROW_TILE)
                make(pl.ds(hbm_row, ROW_TILE), pl.ds(vmem_row, ROW_TILE)).start()
            return carry
        lax.fori_loop(0, MOE_TILE // ROWS_PER_ISSUE, tokens, 0)

    def gather(tile, s):
        token_copies(src_ref, tile * MOE_TILE,
                     lambda hbm, vmem: pltpu.make_async_copy(h_hbm.at[hbm], xbuf.at[s, vmem], gsem.at[s]))

    def wait_all(buf, sem, s):
        pltpu.make_async_copy(buf.at[s], buf.at[s], sem.at[s]).wait()

    @pl.when((i == 0) & (n_valid > 0))
    def _():
        gather(0, 0)

    @pl.when(i + 1 < n_valid)
    def _():
        gather(i + 1, 1 - slot)

    @pl.when(i < n_valid)
    def _():
        wait_all(xbuf, gsem, slot)

        @pl.when(i >= 2)
        def _():
            wait_all(obuf, ssem, slot)

        x = _tiles_to_rows(xbuf, (slot,), MOE_TILE).astype(jnp.bfloat16)
        g = jnp.dot(x, wg_ref[...].astype(jnp.bfloat16), preferred_element_type=jnp.float32)
        u = jnp.dot(x, wu_ref[...].astype(jnp.bfloat16), preferred_element_type=jnp.float32)
        hid = (g * jax.nn.sigmoid(g)) * u
        y = jnp.dot(hid.astype(jnp.bfloat16), wd_ref[...].astype(jnp.bfloat16), preferred_element_type=jnp.float32)
        _rows_to_tiles(obuf, (slot,), y)
        token_copies(dst_ref, i * MOE_TILE,
                     lambda hbm, vmem: pltpu.make_async_copy(obuf.at[slot, vmem], y_hbm.at[hbm], ssem.at[slot]))

    @pl.when(i >= n_valid)
    def _():
        zbuf[...] = jnp.zeros_like(zbuf)
        token_copies(dst_ref, i * MOE_TILE,
                     lambda hbm, vmem: pltpu.make_async_copy(zbuf.at[vmem], y_hbm.at[hbm], zsem.at[0]))
        pltpu.make_async_copy(zbuf, zbuf, zsem.at[0]).wait()

    @pl.when(i == pl.num_programs(0) - 1)
    def _():
        @pl.when(n_valid >= 1)
        def _():
            wait_all(obuf, ssem, (n_valid - 1) % 2)

        @pl.when(n_valid >= 2)
        def _():
            wait_all(obuf, ssem, n_valid % 2)


def expert_ffn(h, src_row, dst_row, tile_expert, n_valid, layer, w_gate, w_up, w_down):
    d = ROW_TILE * h.shape[1]
    p = src_row.shape[0]
    n_tiles = p // MOE_TILE
    buf = (MOE_TILE * ROW_TILE, h.shape[1])
    grid_spec = pltpu.PrefetchScalarGridSpec(
        num_scalar_prefetch=4,
        grid=(n_tiles,),
        in_specs=[
            pl.BlockSpec(memory_space=pl.ANY),
            pl.BlockSpec((None, None, d, EXPERT_FF), lambda i, s, t, te, nv: (layer, te[i], 0, 0)),
            pl.BlockSpec((None, None, d, EXPERT_FF), lambda i, s, t, te, nv: (layer, te[i], 0, 0)),
            pl.BlockSpec((None, None, EXPERT_FF, d), lambda i, s, t, te, nv: (layer, te[i], 0, 0)),
        ],
        out_specs=pl.BlockSpec(memory_space=pl.ANY),
        scratch_shapes=[pltpu.VMEM((2,) + buf, jnp.float32), pltpu.VMEM((2,) + buf, jnp.float32),
                        pltpu.VMEM(buf, jnp.float32),
                        pltpu.SemaphoreType.DMA((2,)), pltpu.SemaphoreType.DMA((2,)), pltpu.SemaphoreType.DMA((1,))],
    )
    return pl.pallas_call(
        _expert_kernel,
        grid_spec=grid_spec,
        out_shape=jax.ShapeDtypeStruct((p * ROW_TILE, h.shape[1]), jnp.float32),
        compiler_params=pltpu.CompilerParams(
            dimension_semantics=("arbitrary",), vmem_limit_bytes=VMEM_LIMIT_BYTES),
        name="expert_ffn",
    )(src_row * ROW_TILE, dst_row * ROW_TILE, tile_expert, n_valid, h, w_gate, w_up, w_down)


def grouped_moe(h, logits, router_bias, layer, w_gate, w_up, w_down):
    t = logits.shape[0]
    aff = jax.nn.sigmoid(logits)
    biased = (aff + router_bias.astype(jnp.float32)).reshape(-1, N_GROUPS, EXPERTS_PER_GROUP)
    idx = jnp.arange(EXPERTS_PER_GROUP, dtype=jnp.int32)
    vi, vj = biased[..., :, None], biased[..., None, :]
    ahead = (vj > vi) | ((vj == vi) & (idx[None, :] < idx[:, None]))
    rank_in_group = jnp.sum(ahead.astype(jnp.int32), axis=-1)
    group_score = jnp.sum(jnp.where(rank_in_group < TOP_K, biased, 0.0), axis=-1)
    group = jnp.argmax(group_score, axis=-1)
    group_hot = group[:, None] == jnp.arange(N_GROUPS)[None, :]
    rank_sel = jnp.sum(jnp.where(group_hot[:, :, None], rank_in_group, 0), axis=1)
    local = jnp.sum(jnp.where(rank_sel[:, None, :] == jnp.arange(TOP_K)[None, :, None], idx[None, None, :], 0), axis=-1)
    expert = (group[:, None] * EXPERTS_PER_GROUP + local).astype(jnp.int32)
    expert_hot = expert[:, :, None] == jnp.arange(N_EXPERTS, dtype=jnp.int32)[None, None, :]
    wsel = jnp.sum(jnp.where(expert_hot, aff[:, None, :], 0.0), axis=-1)
    wsel = wsel / jnp.sum(wsel, axis=-1, keepdims=True)

    flat_e = expert.reshape(-1)
    onehot = (flat_e[:, None] == jnp.arange(N_EXPERTS, dtype=jnp.int32)[None, :]).astype(jnp.int32)
    csum = jnp.cumsum(onehot, axis=0)
    rank = jnp.sum(csum * onehot, axis=1) - 1
    counts = csum[-1]
    tiles_per = (counts + MOE_TILE - 1) // MOE_TILE
    tile_end = jnp.cumsum(tiles_per)
    tile_start = tile_end - tiles_per
    pos = jnp.sum(onehot * tile_start[None, :], axis=1) * MOE_TILE + rank
    n_tiles = (t * TOP_K) // MOE_TILE + N_EXPERTS
    p = n_tiles * MOE_TILE
    n_assign = t * TOP_K
    assign_of = jnp.full((p,), -1, jnp.int32).at[pos].set(jnp.arange(n_assign, dtype=jnp.int32))
    is_pad = assign_of < 0
    src_row = jnp.where(is_pad, 0, assign_of // TOP_K)
    dst_row = jnp.where(is_pad, n_assign + jnp.cumsum(is_pad.astype(jnp.int32)) - 1,
                        (assign_of % TOP_K) * t + assign_of // TOP_K)
    tile_ids = jnp.arange(n_tiles, dtype=jnp.int32)
    n_valid = tile_end[-1].astype(jnp.int32)
    tile_expert = jnp.minimum(jnp.sum((tile_end[None, :] <= tile_ids[:, None]).astype(jnp.int32), axis=1), N_EXPERTS - 1)
    last_expert = tile_expert[jnp.maximum(n_valid - 1, 0)]
    tile_expert = jnp.where(tile_ids < n_valid, tile_expert, last_expert)

    y = expert_ffn(h, src_row, dst_row, tile_expert, n_valid.reshape(1), layer, w_gate, w_up, w_down)
    return y, wsel


def axial_rope_tables(n_tokens):
    t = jnp.arange(n_tokens)
    row = (t // GRID_W).astype(jnp.float32)
    col = (t % GRID_W).astype(jnp.float32)
    n_freq = HEAD_DIM // 4
    inv = ROPE_THETA ** (-jnp.arange(n_freq, dtype=jnp.float32) / n_freq)
    ang = jnp.concatenate([row[:, None] * inv, col[:, None] * inv], axis=-1)
    cos, sin = jnp.cos(ang), jnp.sin(ang)
    cos_t = jnp.tile(jnp.concatenate([cos, cos], axis=-1), (1, NA_HEADS))
    sin_t = jnp.tile(jnp.concatenate([-sin, sin], axis=-1), (1, NA_HEADS))
    return cos_t, sin_t


PROJ_TILE = 512
PROJ_COLS = 512
REST_WIDTH = GATE_OFF - POOL_OFF
GATE_WIDTH = N_BRANCH * D_MODEL
PROJ_VMEM_LIMIT = 56 * 1024 * 1024


def _swap_head_halves(x):
    w = x.shape[-1]
    lane = lax.broadcasted_iota(jnp.int32, (1, w), 1)
    lower = (lane % HEAD_DIM) < HEAD_DIM // 2
    return jnp.where(lower, pltpu.roll(x, w - HEAD_DIM // 2, axis=1), pltpu.roll(x, HEAD_DIM // 2, axis=1))


def _proj_kernel(x_ref, g_ref, shift_ref, scale_ref, w_ref, *refs, rope):
    if rope:
        cos_ref, sin_ref, qr_ref, qp_ref, k_ref, v_ref, rest_ref, gate_ref = refs
    else:
        qp_ref, k_ref, v_ref, rest_ref, gate_ref = refs
    x = x_ref[...]
    y = x * lax.rsqrt(jnp.mean(x * x, axis=-1, keepdims=True) + RMS_EPS) * g_ref[...]
    h = (y * (1.0 + scale_ref[...]) + shift_ref[...]).astype(jnp.bfloat16)

    def cols(c0, width=PROJ_COLS):
        return jnp.dot(h, w_ref[:, c0:c0 + width], preferred_element_type=jnp.float32)

    q = cols(0) * (HEAD_DIM ** -0.5)
    k = cols(BRANCH_WIDTH)
    qp_ref[...] = q.astype(jnp.bfloat16)
    if rope:
        cos_t, sin_t = cos_ref[...], sin_ref[...]
        qr_ref[...] = (q * cos_t + _swap_head_halves(q) * sin_t).astype(jnp.bfloat16)
        k = k * cos_t + _swap_head_halves(k) * sin_t
    k_ref[...] = k.astype(jnp.bfloat16)
    v_ref[...] = cols(2 * BRANCH_WIDTH).astype(jnp.bfloat16)
    for c0 in range(0, REST_WIDTH, PROJ_COLS):
        rest_ref[:, c0:c0 + PROJ_COLS] = cols(POOL_OFF + c0)
    for c0 in range(0, GATE_WIDTH, PROJ_COLS):
        gate_ref[:, c0:c0 + PROJ_COLS] = jax.nn.sigmoid(cols(GATE_OFF + c0)).astype(jnp.bfloat16)


def input_projection(x, g, shift, scale, w, rope_tables=None):
    b, n, d = x.shape
    tm = min(PROJ_TILE, n)
    rope = rope_tables is not None
    per_batch = shift.shape[0] > 1
    mod_spec = pl.BlockSpec((None, 1, d), (lambda i, j: (i, 0, 0)) if per_batch else (lambda i, j: (0, 0, 0)))
    tok = lambda width: pl.BlockSpec((None, tm, width), lambda i, j: (i, j, 0))
    in_specs = [tok(d), pl.BlockSpec((1, d), lambda i, j: (0, 0)), mod_spec, mod_spec,
                pl.BlockSpec((d, IN_WIDTH), lambda i, j: (0, 0), pipeline_mode=pl.Buffered(1))]
    args = [x, g.reshape(1, d), shift[:, None, :], scale[:, None, :], w]
    n_qkv = 3
    if rope:
        in_specs += [pl.BlockSpec((tm, BRANCH_WIDTH), lambda i, j: (j, 0))] * 2
        args += list(rope_tables)
        n_qkv = 4
    bf = jnp.bfloat16
    out_shape = ([jax.ShapeDtypeStruct((b, n, BRANCH_WIDTH), bf)] * n_qkv
                 + [jax.ShapeDtypeStruct((b, n, REST_WIDTH), jnp.float32), jax.ShapeDtypeStruct((b, n, GATE_WIDTH), bf)])
    out_specs = [tok(BRANCH_WIDTH)] * n_qkv + [tok(REST_WIDTH), tok(GATE_WIDTH)]
    return pl.pallas_call(
        functools.partial(_proj_kernel, rope=rope),
        grid=(b, n // tm),
        in_specs=in_specs,
        out_specs=out_specs,
        out_shape=out_shape,
        compiler_params=pltpu.CompilerParams(
            dimension_semantics=("parallel", "parallel"), vmem_limit_bytes=PROJ_VMEM_LIMIT),
        name="input_projection",
    )(*args)


NA_GROUP = 4
NA_SLAB = NA_GROUP + NA_KH_MAX
NA_HEADS_PER_STEP = 128 // HEAD_DIM


def _natten_bias_patterns(rpb, rows):
    j = jnp.arange(GRID_W)
    col_start = jnp.clip(j - NA_KW // 2, 0, GRID_W - NA_KW)
    col_mask = (j[None, :] >= col_start[:, None]) & (j[None, :] < col_start[:, None] + NA_KW)
    dc = jnp.clip(j[None, :] - j[:, None], -(NA_KW - 1), NA_KW - 1) + NA_KW - 1
    dc_hot = (dc[None, :, :] == jnp.arange(2 * NA_KW - 1)[:, None, None]).astype(jnp.float32)
    by_col = jnp.einsum('hrc,cqk->hrqk', rpb.astype(jnp.float32), dc_hot, precision=lax.Precision.HIGHEST)
    by_col = jnp.where(col_mask[None, None], by_col, NEG_INF)
    masked = jnp.full((NA_HEADS, GRID_W, GRID_W), NEG_INF, jnp.float32)
    n_groups = rows // NA_GROUP
    patterns = []
    for g in (0, 1, n_groups - 1):
        slab0 = min(max(g * NA_GROUP - NA_KH_MAX // 2, 0), rows - NA_SLAB)
        q_rows = []
        for r in range(g * NA_GROUP, (g + 1) * NA_GROUP):
            win0 = min(max(r - NA_KH_MAX // 2, 0), rows - NA_KH_MAX)
            blocks = [by_col[:, kr - r + NA_KH_MAX - 1] if win0 <= kr < win0 + NA_KH_MAX else masked
                      for kr in range(slab0, slab0 + NA_SLAB)]
            q_rows.append(jnp.concatenate(blocks, axis=-1))
        patterns.append(jnp.concatenate(q_rows, axis=1))
    return jnp.stack(patterns, axis=1)


def _softmax_pv(s_lat, s_ctx, v_lat, v_ctx):
    m = jnp.max(s_lat, axis=-1, keepdims=True)
    if s_ctx is not None:
        m = jnp.maximum(m, jnp.max(s_ctx, axis=-1, keepdims=True))
    p_lat = jnp.exp(s_lat - m)
    den = jnp.sum(p_lat, axis=-1, keepdims=True)
    o = jnp.dot(p_lat.astype(jnp.bfloat16), v_lat, preferred_element_type=jnp.float32)
    if s_ctx is not None:
        p_ctx = jnp.exp(s_ctx - m)
        den = den + jnp.sum(p_ctx, axis=-1, keepdims=True)
        o = o + jnp.dot(p_ctx.astype(jnp.bfloat16), v_ctx, preferred_element_type=jnp.float32)
    return o / den


def _head_masks():
    lane = lax.broadcasted_iota(jnp.int32, (1, NA_HEADS_PER_STEP * HEAD_DIM), 1)
    return [(lane >= h * HEAD_DIM) & (lane < (h + 1) * HEAD_DIM) for h in range(NA_HEADS_PER_STEP)]


def _natten_kernel(qr_ref, qp_ref, kr_ref, v_ref, kc_ref, vc_ref, bias_ref, o_ref):
    rows = qr_ref.shape[0] // GRID_W
    n_groups = rows // NA_GROUP
    nq = NA_GROUP * GRID_W
    nk = NA_SLAB * GRID_W
    kc = kc_ref[...]
    vc = vc_ref[...]
    nt = (((1,), (1,)), ((), ()))
    masks = _head_masks()

    def body(g, carry):
        slab0 = jnp.clip(g * NA_GROUP - NA_KH_MAX // 2, 0, rows - NA_SLAB)
        pattern = jnp.where(g == 0, 0, jnp.where(g == n_groups - 1, 2, 1))
        q0 = pl.multiple_of(g * nq, nq)
        k0 = pl.multiple_of(slab0 * GRID_W, GRID_W)
        qr = qr_ref[pl.ds(q0, nq), :]
        qp = qp_ref[pl.ds(q0, nq), :]
        ks = kr_ref[pl.ds(k0, nk), :]
        vs = v_ref[pl.ds(k0, nk), :]
        out = None
        for h, mask in enumerate(masks):
            s_lat = lax.dot_general(jnp.where(mask, qr, 0), ks, nt, preferred_element_type=jnp.float32)
            s_lat = s_lat + bias_ref[h, pattern]
            s_ctx = lax.dot_general(jnp.where(mask, qp, 0), kc, nt, preferred_element_type=jnp.float32)
            o = _softmax_pv(s_lat, s_ctx, vs, vc)
            out = o if out is None else jnp.where(mask, o, out)
        o_ref[pl.ds(q0, nq), :] = out.astype(o_ref.dtype)
        return carry

    lax.fori_loop(0, n_groups, body, 0)


def neighbourhood_attention(q_rot, q_plain, k_rot, v, k_ctx, v_ctx, rpb):
    b, s, width = q_rot.shape
    n_ctx = k_ctx.shape[1]
    lanes = NA_HEADS_PER_STEP * HEAD_DIM
    bias = _natten_bias_patterns(rpb, s // GRID_W)
    lat = pl.BlockSpec((None, s, lanes), lambda i, j: (i, 0, j))
    cx = pl.BlockSpec((None, n_ctx, lanes), lambda i, j: (i, 0, j))
    return pl.pallas_call(
        _natten_kernel,
        grid=(b, width // lanes),
        in_specs=[lat, lat, lat, lat, cx, cx,
                  pl.BlockSpec((NA_HEADS_PER_STEP, 3, NA_GROUP * GRID_W, NA_SLAB * GRID_W), lambda i, j: (j, 0, 0, 0))],
        out_specs=lat,
        out_shape=jax.ShapeDtypeStruct((b, s, width), jnp.float32),
        compiler_params=pltpu.CompilerParams(
            dimension_semantics=("parallel", "parallel"), vmem_limit_bytes=VMEM_LIMIT_BYTES),
        name="natten",
    )(q_rot, q_plain, k_rot, v, k_ctx, v_ctx, bias)


def _ctx_attn_kernel(q_ref, k_ref, v_ref, o_ref):
    nt = (((1,), (1,)), ((), ()))
    q, k, v = q_ref[...], k_ref[...], v_ref[...]
    out = None
    for mask in _head_masks():
        s = lax.dot_general(jnp.where(mask, q, 0), k, nt, preferred_element_type=jnp.float32)
        o = _softmax_pv(s, None, v, None)
        out = o if out is None else jnp.where(mask, o, out)
    o_ref[...] = out.astype(o_ref.dtype)


def context_attention(q, k, v):
    b, n, width = q.shape
    lanes = NA_HEADS_PER_STEP * HEAD_DIM
    spec = pl.BlockSpec((None, n, lanes), lambda i, j: (i, 0, j))
    return pl.pallas_call(
        _ctx_attn_kernel,
        grid=(b, width // lanes),
        in_specs=[spec, spec, spec],
        out_specs=spec,
        out_shape=jax.ShapeDtypeStruct((b, n, width), jnp.float32),
        compiler_params=pltpu.CompilerParams(dimension_semantics=("parallel", "parallel")),
        name="ctx_attention",
    )(q, k, v)


POOL_PAD = 16


def _pool_kernel(u_ref, w_ref, scale_ref, o_ref, pad_ref):
    n = u_ref.shape[0]
    u = u_ref[...]
    pad_ref[pl.ds(0, POOL_PAD), :] = jnp.zeros((POOL_PAD, POOL_GROUP_DIM), jnp.float32)
    pad_ref[pl.ds(POOL_PAD + n, POOL_PAD), :] = jnp.zeros((POOL_PAD, POOL_GROUP_DIM), jnp.float32)
    pad_ref[pl.ds(POOL_PAD, n), :] = u
    t = lax.broadcasted_iota(jnp.int32, (n, 1), 0)
    for g, w in enumerate(POOL_WINDOWS):
        @pl.when(pl.program_id(1) == g)
        def _(w=w):
            first = POOL_PAD - w // 2
            acc = pad_ref[pl.ds(first, n), :]
            for k in range(1, w):
                acc = acc + pad_ref[pl.ds(first + k, n), :]
            cnt = jnp.minimum(t + (w - w // 2), n) - jnp.maximum(t - w // 2, 0)
            y = acc / cnt.astype(jnp.float32) - u
            y = jnp.dot(y.astype(jnp.bfloat16), w_ref[...].astype(jnp.bfloat16), preferred_element_type=jnp.float32)
            o_ref[...] = y * scale_ref[...]


def multiscale_pool(rest, pool_w, pool_scale):
    b, n, _ = rest.shape
    col0 = 0
    return pl.pallas_call(
        _pool_kernel,
        grid=(b, POOL_GROUPS),
        in_specs=[pl.BlockSpec((None, n, POOL_GROUP_DIM), lambda i, g: (i, 0, col0 + g)),
                  pl.BlockSpec((None, POOL_GROUP_DIM, POOL_GROUP_DIM), lambda i, g: (g, 0, 0)),
                  pl.BlockSpec((1, POOL_GROUP_DIM), lambda i, g: (0, g))],
        out_specs=pl.BlockSpec((None, n, POOL_GROUP_DIM), lambda i, g: (i, 0, g)),
        out_shape=jax.ShapeDtypeStruct((b, n, BRANCH_WIDTH), jnp.float32),
        scratch_shapes=[pltpu.VMEM((n + 2 * POOL_PAD, POOL_GROUP_DIM), jnp.float32)],
        compiler_params=pltpu.CompilerParams(
            dimension_semantics=("parallel", "parallel"), vmem_limit_bytes=VMEM_LIMIT_BYTES),
        name="multiscale_pool",
    )(rest, pool_w, pool_scale.reshape(1, BRANCH_WIDTH))


HY_CHUNK = 64
HY_PITCH = 72
HY_LANES = 128
HY_FEAT = 128
HY_VMEM_LIMIT = 56 * 1024 * 1024


def _cis_tables(num, den):
    ang = (-2.0 * math.pi / den) * (num % den).astype(jnp.float32)
    return jnp.cos(ang), jnp.sin(ang)


def _stack_complex(mr, mi):
    return jnp.concatenate([jnp.concatenate([mr, -mi], axis=-1), jnp.concatenate([mi, mr], axis=-1)], axis=-2)


def _hy_dft_tables(n):
    big = 2 * n
    n1_full = big // HY_CHUNK
    k1 = jnp.arange(n1_full, dtype=jnp.int32)
    n1 = jnp.arange(n1_full, dtype=jnp.int32)
    n2 = jnp.arange(HY_CHUNK, dtype=jnp.int32)
    mr, mi = _cis_tables(k1[None, :, None] * (HY_CHUNK * n1[None, None, :] + n2[:, None, None]), big)
    half = n1_full // 2
    fwd_a = _stack_complex(mr[:, :, :half], mi[:, :, :half])
    fwd_a_real = jnp.concatenate([mr, mi], axis=-2)
    mrt = jnp.transpose(mr[:, :, :half], (0, 2, 1))
    mit = -jnp.transpose(mi[:, :, :half], (0, 2, 1))
    inv_a = _stack_complex(mrt, mit)
    er, ei = _cis_tables(n2[:, None] * n2[None, :], HY_CHUNK)
    fwd_b = _stack_complex(er, ei)
    inv_b = _stack_complex(er, -ei)
    bf = jnp.bfloat16
    return fwd_a.astype(bf), fwd_a_real.astype(bf), inv_a.astype(bf), fwd_b.astype(bf), inv_b.astype(bf)


def _hy_positions(n):
    f32 = jnp.float32
    t = jnp.linspace(0.0, 1.0, n, dtype=f32)[:, None]
    bands = (HY_EMB_DIM - 1) // 2
    ang = (2.0 * math.pi / n) * jnp.arange(n, dtype=f32)[:, None]
    f = jnp.linspace(1e-4, bands - 1, bands, dtype=f32)[None, :]
    z = jnp.concatenate([t, jnp.cos(f * ang), -jnp.sin(f * ang)], axis=-1)
    z = jnp.pad(z, ((0, 0), (0, HY_FEAT - HY_EMB_DIM)))
    zb = jnp.concatenate([z[:1], z[:0:-1]], axis=0)
    return z, zb


def _filter_mlp(z, w1, b1, fr, w2, b2):
    hp = lax.Precision.HIGHEST
    h = jnp.sin(fr * (jnp.dot(z, w1, precision=hp, preferred_element_type=jnp.float32) + b1))
    return jnp.sin(fr * (jnp.dot(h, w2, precision=hp, preferred_element_type=jnp.float32) + b2))


def _filter_taps(zf_ref, zb_ref, w1_ref, b1_ref, fr_ref, w2_ref, b2_ref, w3f_ref, w3b_ref, delta_ref, hid_ref):
    n = zf_ref.shape[0]
    hp = lax.Precision.HIGHEST

    @pl.when((pl.program_id(1) == 0) & (pl.program_id(2) == 0))
    def _():
        args = (w1_ref[...], b1_ref[...], fr_ref[...], w2_ref[...], b2_ref[...])
        hid_ref[0] = _filter_mlp(zf_ref[...], *args)
        hid_ref[1] = _filter_mlp(zb_ref[...], *args)

    delta = delta_ref[...]
    hf = jnp.dot(hid_ref[0], w3f_ref[...], precision=hp, preferred_element_type=jnp.float32)
    hf = hf * jnp.exp(-zf_ref[:, 0:1] * delta)
    hb = jnp.dot(hid_ref[1], w3b_ref[...], precision=hp, preferred_element_type=jnp.float32)
    hb = hb * jnp.exp(-zb_ref[:, 0:1] * delta)
    row = lax.broadcasted_iota(jnp.int32, (n, 1), 0)
    hb = jnp.where(row > 0, hb, 0.0)
    norm = jnp.sum(jnp.abs(hf), axis=0, keepdims=True) + jnp.sum(jnp.abs(hb), axis=0, keepdims=True)
    return hf / norm, hb / norm


def _filter_spec_kernel(zf_ref, zb_ref, w1_ref, b1_ref, fr_ref, w2_ref, b2_ref, w3f_ref, w3b_ref, delta_ref,
                        fa_ref, fb_ref, hr_ref, hi_ref, hid_ref, tbuf, sbuf):
    n = zf_ref.shape[0]
    chunks = n // HY_CHUNK
    n1_full = 2 * chunks
    hf, hb = _filter_taps(zf_ref, zb_ref, w1_ref, b1_ref, fr_ref, w2_ref, b2_ref, w3f_ref, w3b_ref, delta_ref, hid_ref)
    for c in range(chunks):
        tbuf[pl.ds(c * HY_PITCH, HY_CHUNK), :] = hf[c * HY_CHUNK:(c + 1) * HY_CHUNK]
        tbuf[pl.ds((chunks + c) * HY_PITCH, HY_CHUNK), :] = hb[c * HY_CHUNK:(c + 1) * HY_CHUNK]

    def stage_a(n2, carry):
        rows = tbuf[pl.ds(n2, n1_full, stride=HY_PITCH), :].astype(jnp.bfloat16)
        a = jnp.dot(fa_ref[n2], rows, preferred_element_type=jnp.float32)
        sbuf[0, pl.ds(n2, n1_full, stride=HY_PITCH), :] = a[:n1_full]
        sbuf[1, pl.ds(n2, n1_full, stride=HY_PITCH), :] = a[n1_full:]
        return carry

    lax.fori_loop(0, HY_CHUNK, stage_a, 0, unroll=2)
    inv_len = 1.0 / (2 * n)

    def stage_b(k1, carry):
        r0 = pl.multiple_of(k1 * HY_PITCH, 8)
        x = jnp.concatenate([sbuf[0, pl.ds(r0, HY_CHUNK), :], sbuf[1, pl.ds(r0, HY_CHUNK), :]], axis=0)
        y = jnp.dot(fb_ref[...], x.astype(jnp.bfloat16), preferred_element_type=jnp.float32) * inv_len
        o0 = pl.multiple_of(k1 * HY_CHUNK, HY_CHUNK)
        hr_ref[pl.ds(o0, HY_CHUNK), :] = y[:HY_CHUNK]
        hi_ref[pl.ds(o0, HY_CHUNK), :] = y[HY_CHUNK:]
        return carry

    lax.fori_loop(0, n1_full, stage_b, 0, unroll=4)


def hyena_filter_spectra(n, hy_w1, hy_b1, hy_freq, hy_w2, hy_b2, hy_w3, tables):
    depth = hy_w1.shape[0]
    zf, zb = _hy_positions(n)
    tiles = BRANCH_WIDTH // HY_LANES
    w1 = jnp.pad(hy_w1, ((0, 0), (0, HY_FEAT - HY_EMB_DIM), (0, 0)))
    deltas = jnp.abs(jnp.linspace(HY_MIN_DECAY, HY_MAX_DECAY, BRANCH_WIDTH, dtype=jnp.float32)).reshape(1, BRANCH_WIDTH)
    fwd_a_real, fwd_b = tables[1], tables[3]
    n1_full = 2 * n // HY_CHUNK
    one = pl.Buffered(1)
    full = lambda shape: pl.BlockSpec(shape, lambda l, o, j: (0,) * len(shape), pipeline_mode=one)
    per_layer = lambda shape: pl.BlockSpec((None,) + shape, lambda l, o, j: (l,) + (0,) * len(shape))
    out_spec = pl.BlockSpec((None, None, 2 * n, HY_LANES), lambda l, o, j: (l, o, 0, j), pipeline_mode=one)
    out_sds = jax.ShapeDtypeStruct((depth, HY_ORDER, 2 * n, BRANCH_WIDTH), jnp.float32)
    return pl.pallas_call(
        _filter_spec_kernel,
        grid=(depth, HY_ORDER, tiles),
        in_specs=[full((n, HY_FEAT)), full((n, HY_FEAT)),
                  per_layer((HY_FEAT, HY_FILTER_HIDDEN)), per_layer((1, HY_FILTER_HIDDEN)),
                  per_layer((1, HY_FILTER_HIDDEN)),
                  per_layer((HY_FILTER_HIDDEN, HY_FILTER_HIDDEN)), per_layer((1, HY_FILTER_HIDDEN)),
                  pl.BlockSpec((None, HY_FILTER_HIDDEN, HY_LANES), lambda l, o, j: (l, 0, o * 2 * tiles + j)),
                  pl.BlockSpec((None, HY_FILTER_HIDDEN, HY_LANES), lambda l, o, j: (l, 0, o * 2 * tiles + tiles + j)),
                  pl.BlockSpec((1, HY_LANES), lambda l, o, j: (0, j)),
                  full((HY_CHUNK, 2 * n1_full, n1_full)), full((2 * HY_CHUNK, 2 * HY_CHUNK))],
        out_specs=[out_spec, out_spec],
        out_shape=[out_sds, out_sds],
        scratch_shapes=[pltpu.VMEM((2, n, HY_FILTER_HIDDEN), jnp.float32),
                        pltpu.VMEM((n1_full * HY_PITCH, HY_LANES), jnp.float32),
                        pltpu.VMEM((2, n1_full * HY_PITCH, HY_LANES), jnp.float32)],
        compiler_params=pltpu.CompilerParams(
            dimension_semantics=("arbitrary", "arbitrary", "arbitrary"), vmem_limit_bytes=HY_VMEM_LIMIT),
        name="hyena_filter_spectra",
    )(zf, zb, w1, hy_b1[:, None, :], hy_freq[:, None, :], hy_w2, hy_b2[:, None, :], hy_w3, hy_w3, deltas,
      fwd_a_real, fwd_b)


def _short_conv(u_ref, b, w_ref, bias_ref, pad_ref):
    n = u_ref.shape[1]
    x = u_ref[b]
    pad_ref[pl.ds(0, 8), :] = jnp.zeros((8, HY_LANES), jnp.float32)
    pad_ref[pl.ds(8 + n, 8), :] = jnp.zeros((8, HY_LANES), jnp.float32)
    pad_ref[pl.ds(8, n), :] = x
    return (pad_ref[pl.ds(7, n), :] * w_ref[0:1, :] + x * w_ref[1:2, :] + pad_ref[pl.ds(9, n), :] * w_ref[2:3, :]
            + bias_ref[...])


def _hyena_conv_kernel(z_ref, g_ref, wz_ref, bz_ref, wg_ref, bg_ref, skip_ref, hr_ref, hi_ref,
                       fa_ref, ia_ref, fb_ref, ib_ref, o_ref, pad_ref, zbuf, sbuf, *, conv_z):
    n = z_ref.shape[1]
    chunks = n // HY_CHUNK
    n1_full = 2 * chunks
    for b in range(2):
        z = _short_conv(z_ref, b, wz_ref, bz_ref, pad_ref) if conv_z else z_ref[b]
        for c in range(chunks):
            zbuf[b, pl.ds(c * HY_PITCH, HY_CHUNK), :] = z[c * HY_CHUNK:(c + 1) * HY_CHUNK]

    def stage_a(n2, carry):
        x = jnp.concatenate([zbuf[0, pl.ds(n2, chunks, stride=HY_PITCH), :],
                             zbuf[1, pl.ds(n2, chunks, stride=HY_PITCH), :]], axis=0)
        a = jnp.dot(fa_ref[n2], x.astype(jnp.bfloat16), preferred_element_type=jnp.float32)
        sbuf[0, pl.ds(n2, n1_full, stride=HY_PITCH), :] = a[:n1_full]
        sbuf[1, pl.ds(n2, n1_full, stride=HY_PITCH), :] = a[n1_full:]
        return carry

    lax.fori_loop(0, HY_CHUNK, stage_a, 0, unroll=2)

    def stage_b(k1, carry):
        r0 = pl.multiple_of(k1 * HY_PITCH, 8)
        x = jnp.concatenate([sbuf[0, pl.ds(r0, HY_CHUNK), :], sbuf[1, pl.ds(r0, HY_CHUNK), :]], axis=0)
        s = jnp.dot(fb_ref[...], x.astype(jnp.bfloat16), preferred_element_type=jnp.float32)
        h0 = pl.multiple_of(k1 * HY_CHUNK, HY_CHUNK)
        hr = hr_ref[pl.ds(h0, HY_CHUNK), :]
        hi = hi_ref[pl.ds(h0, HY_CHUNK), :]
        sr, si = s[:HY_CHUNK], s[HY_CHUNK:]
        y = jnp.concatenate([sr * hr - si * hi, sr * hi + si * hr], axis=0)
        c = jnp.dot(ib_ref[...], y.astype(jnp.bfloat16), preferred_element_type=jnp.float32)
        sbuf[0, pl.ds(r0, HY_CHUNK), :] = c[:HY_CHUNK]
        sbuf[1, pl.ds(r0, HY_CHUNK), :] = c[HY_CHUNK:]
        return carry

    lax.fori_loop(0, n1_full, stage_b, 0, unroll=8)
    skip = skip_ref[...]

    def stage_c(n2, carry):
        x = jnp.concatenate([sbuf[0, pl.ds(n2, n1_full, stride=HY_PITCH), :],
                             sbuf[1, pl.ds(n2, n1_full, stride=HY_PITCH), :]], axis=0)
        y = jnp.dot(ia_ref[n2], x.astype(jnp.bfloat16), preferred_element_type=jnp.float32)
        for b in range(2):
            zb = zbuf[b, pl.ds(n2, chunks, stride=HY_PITCH), :]
            zbuf[b, pl.ds(n2, chunks, stride=HY_PITCH), :] = y[b * chunks:(b + 1) * chunks] + zb * skip
        return carry

    lax.fori_loop(0, HY_CHUNK, stage_c, 0, unroll=4)
    for b in range(2):
        g = _short_conv(g_ref, b, wg_ref, bg_ref, pad_ref)
        for c in range(chunks):
            o_ref[b, pl.ds(c * HY_CHUNK, HY_CHUNK), :] = (
                g[c * HY_CHUNK:(c + 1) * HY_CHUNK] * zbuf[b, pl.ds(c * HY_PITCH, HY_CHUNK), :])


def hyena_long_conv(z, z_col0, g, g_col0, conv_z, short_w, short_b, w_off_z, w_off_g, skip, hr, hi, layer, order,
                    tables):
    b, n, _ = g.shape
    tiles = BRANCH_WIDTH // HY_LANES
    fwd_a, _, inv_a, fwd_b, inv_b = tables
    n1_full = 2 * n // HY_CHUNK
    one = pl.Buffered(1)
    zc, gc = z_col0 // HY_LANES, g_col0 // HY_LANES
    wz, wg = w_off_z // HY_LANES, w_off_g // HY_LANES
    const = lambda shape: pl.BlockSpec(shape, lambda j, p: (0,) * len(shape), pipeline_mode=one)
    return pl.pallas_call(
        functools.partial(_hyena_conv_kernel, conv_z=conv_z),
        grid=(tiles, b // 2),
        in_specs=[pl.BlockSpec((2, n, HY_LANES), lambda j, p: (p, 0, zc + j), pipeline_mode=one),
                  pl.BlockSpec((2, n, HY_LANES), lambda j, p: (p, 0, gc + j), pipeline_mode=one),
                  pl.BlockSpec((HY_SHORT, HY_LANES), lambda j, p: (0, wz + j)),
                  pl.BlockSpec((1, HY_LANES), lambda j, p: (0, wz + j)),
                  pl.BlockSpec((HY_SHORT, HY_LANES), lambda j, p: (0, wg + j)),
                  pl.BlockSpec((1, HY_LANES), lambda j, p: (0, wg + j)),
                  pl.BlockSpec((1, HY_LANES), lambda j, p: (0, j)),
                  pl.BlockSpec((None, None, 2 * n, HY_LANES), lambda j, p: (layer, order, 0, j), pipeline_mode=one),
                  pl.BlockSpec((None, None, 2 * n, HY_LANES), lambda j, p: (layer, order, 0, j), pipeline_mode=one),
                  const((HY_CHUNK, 2 * n1_full, n1_full)), const((HY_CHUNK, n1_full, 2 * n1_full)),
                  const((2 * HY_CHUNK, 2 * HY_CHUNK)), const((2 * HY_CHUNK, 2 * HY_CHUNK))],
        out_specs=pl.BlockSpec((2, n, HY_LANES), lambda j, p: (p, 0, j), pipeline_mode=one),
        out_shape=jax.ShapeDtypeStruct((b, n, BRANCH_WIDTH), jnp.float32),
        scratch_shapes=[pltpu.VMEM((n + 16, HY_LANES), jnp.float32),
                        pltpu.VMEM((2, (n // HY_CHUNK) * HY_PITCH, HY_LANES), jnp.float32),
                        pltpu.VMEM((2, n1_full * HY_PITCH, HY_LANES), jnp.float32)],
        compiler_params=pltpu.CompilerParams(
            dimension_semantics=("parallel", "parallel"), vmem_limit_bytes=HY_VMEM_LIMIT),
        name="hyena_long_conv",
    )(z, g, short_w, short_b.reshape(1, -1), short_w, short_b.reshape(1, -1), skip.reshape(1, -1), hr, hi,
      fwd_a, inv_a, fwd_b, inv_b)


def hyena_mixer(rest, short_w, short_b, skip, hr, hi, layer, tables):
    off = HY_OFF - POOL_OFF
    z1 = hyena_long_conv(rest, off, rest, off + BRANCH_WIDTH, True, short_w, short_b, 0, BRANCH_WIDTH,
                         skip[0], hr, hi, layer, 0, tables)
    return hyena_long_conv(z1, 0, rest, off + 2 * BRANCH_WIDTH, False, short_w, short_b, 0, 2 * BRANCH_WIDTH,
                           skip[1], hr, hi, layer, 1, tables)


MERGE_TILE = 512


def _merge_kernel(ya_ref, yp_ref, yh_ref, gate_ref, x_ref, res_gate_ref, wb_ref, wo_ref, g2_ref, shift_ref, scale_ref,
                  wr_ref, xo_ref, h_ref, logit_ref):
    d = x_ref.shape[-1]
    merged = None
    for i, y_ref in enumerate((ya_ref, yp_ref, yh_ref)):
        br = jnp.dot(y_ref[...].astype(jnp.bfloat16), wb_ref[i], preferred_element_type=jnp.float32)
        term = gate_ref[:, i * d:(i + 1) * d].astype(jnp.float32) * br
        merged = term if merged is None else merged + term
    out = jnp.dot(merged.astype(jnp.bfloat16), wo_ref[...], preferred_element_type=jnp.float32)
    x = x_ref[...] + res_gate_ref[...] * out
    xo_ref[...] = x
    y = x * lax.rsqrt(jnp.mean(x * x, axis=-1, keepdims=True) + RMS_EPS) * g2_ref[...]
    h = y * (1.0 + scale_ref[...]) + shift_ref[...]
    _rows_to_tiles(h_ref, (), h)
    logit_ref[...] = jnp.dot(h, wr_ref[...], precision=lax.Precision.HIGHEST, preferred_element_type=jnp.float32)


def merge_and_norm(y_attn, y_pool, y_hy, gates, x, mod, w_branch, w_out, g2, w_router):
    b, n, d = x.shape
    tm = min(MERGE_TILE, n)
    per_batch = mod.shape[0] > 1
    mod_spec = lambda k: pl.BlockSpec((None, None, 1, d),
                                      (lambda i, j: (i, k, 0, 0)) if per_batch else (lambda i, j: (0, k, 0, 0)))
    tok = lambda width: pl.BlockSpec((None, tm, width), lambda i, j: (i, j, 0))
    const = lambda shape: pl.BlockSpec(shape, lambda i, j: (0,) * len(shape))
    call = pl.pallas_call(
        _merge_kernel,
        grid=(b, n // tm),
        in_specs=[tok(BRANCH_WIDTH), tok(BRANCH_WIDTH), tok(BRANCH_WIDTH), tok(GATE_WIDTH), tok(d), mod_spec(2),
                  const((N_BRANCH, BRANCH_WIDTH, d)), const((d, d)), const((1, d)), mod_spec(3), mod_spec(4),
                  const((d, N_EXPERTS))],
        out_specs=[tok(d), pl.BlockSpec((None, tm * ROW_TILE, d // ROW_TILE), lambda i, j: (i, j, 0)), tok(N_EXPERTS)],
        out_shape=[jax.ShapeDtypeStruct((b, n, d), jnp.float32),
                   jax.ShapeDtypeStruct((b, n * ROW_TILE, d // ROW_TILE), jnp.float32),
                   jax.ShapeDtypeStruct((b, n, N_EXPERTS), jnp.float32)],
        compiler_params=pltpu.CompilerParams(
            dimension_semantics=("parallel", "parallel"), vmem_limit_bytes=VMEM_LIMIT_BYTES),
        name="merge_and_norm",
    )
    mod4 = mod[:, :, None, :]
    return call(y_attn, y_pool, y_hy, gates, x, mod4, w_branch, w_out, g2.reshape(1, d), mod4, mod4, w_router)


def _combine_kernel(y0_ref, y1_ref, w_ref, x_ref, gate_ref, g_ref, o_ref, *, final_norm):
    w = w_ref[...]
    m = x_ref.shape[0]
    f = _tiles_to_rows(y0_ref, (), m) * w[:, 0:1] + _tiles_to_rows(y1_ref, (), m) * w[:, 1:2]
    x = x_ref[...] + gate_ref[...] * f
    if final_norm:
        x = x * lax.rsqrt(jnp.mean(x * x, axis=-1, keepdims=True) + RMS_EPS) * g_ref[...]
    o_ref[...] = x


def moe_combine(y, wsel, row0, x, mod, final_g=None):
    b, n, d = x.shape
    tm = min(MERGE_TILE, n)
    tiles = n // tm
    blk0 = row0 // tm
    second = wsel.shape[0] // tm
    per_batch = mod.shape[0] > 1
    tok = lambda width: pl.BlockSpec((None, tm, width), lambda i, j: (i, j, 0))
    flat = lambda width, off=0: pl.BlockSpec((tm, width), lambda i, j: (off + blk0 + i * tiles + j, 0))
    tiled = lambda off: pl.BlockSpec((tm * ROW_TILE, d // ROW_TILE), lambda i, j: (off + blk0 + i * tiles + j, 0))
    k = N_MOD - 1
    gate_spec = pl.BlockSpec((None, None, 1, d), (lambda i, j: (i, k, 0, 0)) if per_batch else (lambda i, j: (0, k, 0, 0)))
    g = jnp.ones((1, d), jnp.float32) if final_g is None else final_g.reshape(1, d)
    return pl.pallas_call(
        functools.partial(_combine_kernel, final_norm=final_g is not None),
        grid=(b, n // tm),
        in_specs=[tiled(0), tiled(second), flat(TOP_K), tok(d), gate_spec, pl.BlockSpec((1, d), lambda i, j: (0, 0))],
        out_specs=tok(d),
        out_shape=jax.ShapeDtypeStruct((b, n, d), jnp.float32),
        compiler_params=pltpu.CompilerParams(
            dimension_semantics=("parallel", "parallel"), vmem_limit_bytes=VMEM_LIMIT_BYTES),
        name="moe_combine",
    )(y, y, wsel, x, mod[:, :, None, :], g)


def latent_mixers(x, g1, mod, w_in, rope_tables, k_ctx, v_ctx, rpb, pool_w, pool_scale, hy, w_branch, w_out, g2, w_router):
    q_rot, q_plain, k, v, rest, gates = input_projection(x, g1, mod[:, 0], mod[:, 1], w_in, rope_tables)
    y_attn = neighbourhood_attention(q_rot, q_plain, k, v, k_ctx, v_ctx, rpb)
    y_pool = multiscale_pool(rest, pool_w, pool_scale)
    y_hy = hyena_mixer(rest, *hy)
    return merge_and_norm(y_attn, y_pool, y_hy, gates, x, mod, w_branch, w_out, g2, w_router)


def kernel(x, c, ctx, c_ctx, norm1_g, norm2_g, w_mod, b_mod, w_in, rpb, pool_w, pool_scale, hy_short_w, hy_short_b, hy_w1, hy_b1, hy_freq, hy_w2, hy_b2, hy_w3, hy_skip, w_branch, w_out, w_router, router_bias, w_gate_e, w_up_e, w_down_e, final_g):
    b, s, d = x.shape
    n_ctx = ctx.shape[1]
    bf = jnp.bfloat16
    rope_tables = axial_rope_tables(s)
    tables_l = _hy_dft_tables(s)
    tables_c = _hy_dft_tables(n_ctx)
    spec_l = hyena_filter_spectra(s, hy_w1, hy_b1, hy_freq, hy_w2, hy_b2, hy_w3, tables_l)
    spec_c = hyena_filter_spectra(n_ctx, hy_w1[:DEPTH - 1], hy_b1[:DEPTH - 1], hy_freq[:DEPTH - 1], hy_w2[:DEPTH - 1],
                                  hy_b2[:DEPTH - 1], hy_w3[:DEPTH - 1], tables_c)
    cond = jnp.concatenate([c, c_ctx[None, :], jnp.zeros((8 - (b + 1) % 8, d), c.dtype)], axis=0)
    cond = jax.nn.silu(cond)
    xl, xc = x, ctx
    for l in range(DEPTH):
        need_ctx = l < DEPTH - 1
        mod = (matmul(cond, w_mod[l]) + b_mod[l]).reshape(-1, N_MOD, d)
        ml, mc = mod[:b], mod[b:b + 1]
        w_in_l, w_branch_l, w_out_l = w_in[l].astype(bf), w_branch[l].astype(bf), w_out[l].astype(bf)
        common = (pool_w[l], pool_scale[l])
        hy_c = (hy_short_w[l], hy_short_b[l], hy_skip[l])
        qc, kc, vc, rest_c, gates_c = input_projection(xc, norm1_g[l], mc[:, 0], mc[:, 1], w_in_l)
        xl, hl2, logit_l = latent_mixers(xl, norm1_g[l], ml, w_in_l, rope_tables, kc, vc, rpb[l], *common,
                                         hy_c + (spec_l[0], spec_l[1], l, tables_l), w_branch_l, w_out_l,
                                         norm2_g[l], w_router)
        tokens, logits = hl2.reshape(-1, d // ROW_TILE), logit_l.reshape(b * s, N_EXPERTS)
        if need_ctx:
            ya_c = context_attention(qc, kc, vc)
            yp_c = multiscale_pool(rest_c, *common)
            yh_c = hyena_mixer(rest_c, *hy_c, spec_c[0], spec_c[1], l, tables_c)
            xc, hc2, logit_c = merge_and_norm(ya_c, yp_c, yh_c, gates_c, xc, mc, w_branch_l, w_out_l, norm2_g[l], w_router)
            tokens = jnp.concatenate([tokens, hc2.reshape(-1, d // ROW_TILE)], axis=0)
            logits = jnp.concatenate([logits, logit_c.reshape(b * n_ctx, N_EXPERTS)], axis=0)
        y, wsel = grouped_moe(tokens, logits, router_bias, l, w_gate_e, w_up_e, w_down_e)
        xl = moe_combine(y, wsel, 0, xl, ml, final_g if l == DEPTH - 1 else None)
        if need_ctx:
            xc = moe_combine(y, wsel, b * s, xc, mc)
    return xl
```

```python
import functools
import math

import jax
import jax.numpy as jnp
from jax import lax
from jax.experimental import pallas as pl
from jax.experimental.pallas import tpu as pltpu

D_MODEL = 1024
DEPTH = 4
GRID_W = 64
BRANCH_WIDTH = D_MODEL // 2
N_BRANCH = 3
HEAD_DIM = 64
NA_HEADS = BRANCH_WIDTH // HEAD_DIM
NA_KH_MAX = 8
NA_KW = 16
ROPE_THETA = 10000.0
NEG_INF = -1e30
POOL_WINDOWS = (2, 4, 8, 16)
POOL_GROUPS = len(POOL_WINDOWS)
POOL_GROUP_DIM = BRANCH_WIDTH // POOL_GROUPS
HY_ORDER = 2
HY_SHORT = 3
HY_EMB_DIM = 33
HY_FILTER_HIDDEN = 64
HY_MIN_DECAY = math.log(1e-2) / 0.3
HY_MAX_DECAY = math.log(1e-2) / 1.5
N_EXPERTS = 16
N_GROUPS = 4
EXPERTS_PER_GROUP = N_EXPERTS // N_GROUPS
TOP_K = 2
EXPERT_FF = D_MODEL // 2
N_MOD = 6
RMS_EPS = 1e-6
POOL_OFF = 3 * BRANCH_WIDTH
HY_OFF = POOL_OFF + BRANCH_WIDTH
GATE_OFF = HY_OFF + (HY_ORDER + 1) * BRANCH_WIDTH
IN_WIDTH = GATE_OFF + N_BRANCH * D_MODEL

VMEM_LIMIT_BYTES = 48 * 1024 * 1024
MOE_TILE = 256
ROWS_PER_ISSUE = 8


def _mm_kernel(a_ref, b_ref, o_ref):
    o_ref[...] = jnp.dot(a_ref[...].astype(jnp.bfloat16), b_ref[...].astype(jnp.bfloat16),
                         preferred_element_type=jnp.float32).astype(o_ref.dtype)


def _pick_tile(n, cands):
    for c in cands:
        if n % c == 0:
            return c
    return n


def matmul(a, b, out_dtype=jnp.float32):
    m, k = a.shape
    _, n = b.shape
    tm = _pick_tile(m, (1024, 512, 256, 128, 64, 32, 16, 8))
    tn = _pick_tile(n, (1664, 1024, 512, 256, 128))
    return pl.pallas_call(
        _mm_kernel,
        grid=(m // tm, n // tn),
        in_specs=[pl.BlockSpec((tm, k), lambda i, j: (i, 0)),
                  pl.BlockSpec((k, tn), lambda i, j: (0, j))],
        out_specs=pl.BlockSpec((tm, tn), lambda i, j: (i, j)),
        out_shape=jax.ShapeDtypeStruct((m, n), out_dtype),
        compiler_params=pltpu.CompilerParams(
            dimension_semantics=("parallel", "parallel"), vmem_limit_bytes=VMEM_LIMIT_BYTES),
        name="dense_matmul",
    )(a, b)


ROW_TILE = 8


def _rows_to_tiles(ref, idx, x):
    m = x.shape[0]
    for c in range(ROW_TILE):
        ref[idx + (pl.ds(c, m, stride=ROW_TILE), slice(None))] = x[:, c * 128:(c + 1) * 128]


def _tiles_to_rows(ref, idx, m):
    return jnp.concatenate([ref[idx + (pl.ds(c, m, stride=ROW_TILE), slice(None))] for c in range(ROW_TILE)], axis=1)


def _expert_kernel(src_ref, dst_ref, tile_expert_ref, n_valid_ref, h_hbm, wg_ref, wu_ref, wd_ref, y_hbm,
                   xbuf, obuf, zbuf, gsem, ssem, zsem):
    i = pl.program_id(0)
    n_valid = n_valid_ref[0]
    slot = i % 2

    def token_copies(index_ref, base, make):
        def tokens(g, carry):
            r0 = g * ROWS_PER_ISSUE
            for j in range(ROWS_PER_ISSUE):
                hbm_row = pl.multiple_of(index_ref[base + r0 + j], ROW_TILE)
                vmem_row = pl.multiple_of((r0 + j) * ROW_TILE, ROW_TILE)
                make(pl.ds(hbm_row, ROW_TILE), pl.ds(vmem_row, ROW_TILE)).start()
            return carry
        lax.fori_loop(0, MOE_TILE // ROWS_PER_ISSUE, tokens, 0)

    def gather(tile, s):
        token_copies(src_ref, tile * MOE_TILE,
                     lambda hbm, vmem: pltpu.make_async_copy(h_hbm.at[hbm], xbuf.at[s, vmem], gsem.at[s]))

    def wait_all(buf, sem, s):
        pltpu.make_async_copy(buf.at[s], buf.at[s], sem.at[s]).wait()

    @pl.when((i == 0) & (n_valid > 0))
    def _():
        gather(0, 0)

    @pl.when(i + 1 < n_valid)
    def _():
        gather(i + 1, 1 - slot)

    @pl.when(i < n_valid)
    def _():
        wait_all(xbuf, gsem, slot)

        @pl.when(i >= 2)
        def _():
            wait_all(obuf, ssem, slot)

        x = _tiles_to_rows(xbuf, (slot,), MOE_TILE).astype(jnp.bfloat16)
        g = jnp.dot(x, wg_ref[...].astype(jnp.bfloat16), preferred_element_type=jnp.float32)
        u = jnp.dot(x, wu_ref[...].astype(jnp.bfloat16), preferred_element_type=jnp.float32)
        hid = (g * jax.nn.sigmoid(g)) * u
        y = jnp.dot(hid.astype(jnp.bfloat16), wd_ref[...].astype(jnp.bfloat16), preferred_element_type=jnp.float32)
        _rows_to_tiles(obuf, (slot,), y)
        token_copies(dst_ref, i * MOE_TILE,
                     lambda hbm, vmem: pltpu.make_async_copy(obuf.at[slot, vmem], y_hbm.at[hbm], ssem.at[slot]))

    @pl.when(i >= n_valid)
    def _():
        zbuf[...] = jnp.zeros_like(zbuf)
        token_copies(dst_ref, i * MOE_TILE,
                     lambda hbm, vmem: pltpu.make_async_copy(zbuf.at[vmem], y_hbm.at[hbm], zsem.at[0]))
        pltpu.make_async_copy(zbuf, zbuf, zsem.at[0]).wait()

    @pl.when(i == pl.num_programs(0) - 1)
    def _():
        @pl.when(n_valid >= 1)
        def _():
            wait_all(obuf, ssem, (n_valid - 1) % 2)

        @pl.when(n_valid >= 2)
        def _():
            wait_all(obuf, ssem, n_valid % 2)


def expert_ffn(h, src_row, dst_row, tile_expert, n_valid, layer, w_gate, w_up, w_down):
    d = ROW_TILE * h.shape[1]
    p = src_row.shape[0]
    n_tiles = p // MOE_TILE
    buf = (MOE_TILE * ROW_TILE, h.shape[1])
    grid_spec = pltpu.PrefetchScalarGridSpec(
        num_scalar_prefetch=4,
        grid=(n_tiles,),
        in_specs=[
            pl.BlockSpec(memory_space=pl.ANY),
            pl.BlockSpec((None, None, d, EXPERT_FF), lambda i, s, t, te, nv: (layer, te[i], 0, 0)),
            pl.BlockSpec((None, None, d, EXPERT_FF), lambda i, s, t, te, nv: (layer, te[i], 0, 0)),
            pl.BlockSpec((None, None, EXPERT_FF, d), lambda i, s, t, te, nv: (layer, te[i], 0, 0)),
        ],
        out_specs=pl.BlockSpec(memory_space=pl.ANY),
        scratch_shapes=[pltpu.VMEM((2,) + buf, jnp.float32), pltpu.VMEM((2,) + buf, jnp.float32),
                        pltpu.VMEM(buf, jnp.float32),
                        pltpu.SemaphoreType.DMA((2,)), pltpu.SemaphoreType.DMA((2,)), pltpu.SemaphoreType.DMA((1,))],
    )
    return pl.pallas_call(
        _expert_kernel,
        grid_spec=grid_spec,
        out_shape=jax.ShapeDtypeStruct((p * ROW_TILE, h.shape[1]), jnp.float32),
        compiler_params=pltpu.CompilerParams(
            dimension_semantics=("arbitrary",), vmem_limit_bytes=VMEM_LIMIT_BYTES),
        name="expert_ffn",
    )(src_row * ROW_TILE, dst_row * ROW_TILE, tile_expert, n_valid, h, w_gate, w_up, w_down)


def grouped_moe(h, logits, router_bias, layer, w_gate, w_up, w_down):
    t = logits.shape[0]
    aff = jax.nn.sigmoid(logits)
    biased = (aff + router_bias.astype(jnp.float32)).reshape(-1, N_GROUPS, EXPERTS_PER_GROUP)
    idx = jnp.arange(EXPERTS_PER_GROUP, dtype=jnp.int32)
    vi, vj = biased[..., :, None], biased[..., None, :]
    ahead = (vj > vi) | ((vj == vi) & (idx[None, :] < idx[:, None]))
    rank_in_group = jnp.sum(ahead.astype(jnp.int32), axis=-1)
    group_score = jnp.sum(jnp.where(rank_in_group < TOP_K, biased, 0.0), axis=-1)
    group = jnp.argmax(group_score, axis=-1)
    group_hot = group[:, None] == jnp.arange(N_GROUPS)[None, :]
    rank_sel = jnp.sum(jnp.where(group_hot[:, :, None], rank_in_group, 0), axis=1)
    local = jnp.sum(jnp.where(rank_sel[:, None, :] == jnp.arange(TOP_K)[None, :, None], idx[None, None, :], 0), axis=-1)
    expert = (group[:, None] * EXPERTS_PER_GROUP + local).astype(jnp.int32)
    expert_hot = expert[:, :, None] == jnp.arange(N_EXPERTS, dtype=jnp.int32)[None, None, :]
    wsel = jnp.sum(jnp.where(expert_hot, aff[:, None, :], 0.0), axis=-1)
    wsel = wsel / jnp.sum(wsel, axis=-1, keepdims=True)

    flat_e = expert.reshape(-1)
    onehot = (flat_e[:, None] == jnp.arange(N_EXPERTS, dtype=jnp.int32)[None, :]).astype(jnp.int32)
    csum = jnp.cumsum(onehot, axis=0)
    rank = jnp.sum(csum * onehot, axis=1) - 1
    counts = csum[-1]
    tiles_per = (counts + MOE_TILE - 1) // MOE_TILE
    tile_end = jnp.cumsum(tiles_per)
    tile_start = tile_end - tiles_per
    pos = jnp.sum(onehot * tile_start[None, :], axis=1) * MOE_TILE + rank
    n_tiles = (t * TOP_K) // MOE_TILE + N_EXPERTS
    p = n_tiles * MOE_TILE
    n_assign = t * TOP_K
    assign_of = jnp.full((p,), -1, jnp.int32).at[pos].set(jnp.arange(n_assign, dtype=jnp.int32))
    is_pad = assign_of < 0
    src_row = jnp.where(is_pad, 0, assign_of // TOP_K)
    dst_row = jnp.where(is_pad, n_assign + jnp.cumsum(is_pad.astype(jnp.int32)) - 1,
                        (assign_of % TOP_K) * t + assign_of // TOP_K)
    tile_ids = jnp.arange(n_tiles, dtype=jnp.int32)
    n_valid = tile_end[-1].astype(jnp.int32)
    tile_expert = jnp.minimum(jnp.sum((tile_end[None, :] <= tile_ids[:, None]).astype(jnp.int32), axis=1), N_EXPERTS - 1)
    last_expert = tile_expert[jnp.maximum(n_valid - 1, 0)]
    tile_expert = jnp.where(tile_ids < n_valid, tile_expert, last_expert)

    y = expert_ffn(h, src_row, dst_row, tile_expert, n_valid.reshape(1), layer, w_gate, w_up, w_down)
    return y, wsel


def axial_rope_tables(n_tokens):
    t = jnp.arange(n_tokens)
    row = (t // GRID_W).astype(jnp.float32)
    col = (t % GRID_W).astype(jnp.float32)
    n_freq = HEAD_DIM // 4
    inv = ROPE_THETA ** (-jnp.arange(n_freq, dtype=jnp.float32) / n_freq)
    ang = jnp.concatenate([row[:, None] * inv, col[:, None] * inv], axis=-1)
    cos, sin = jnp.cos(ang), jnp.sin(ang)
    cos_t = jnp.tile(jnp.concatenate([cos, cos], axis=-1), (1, NA_HEADS))
    sin_t = jnp.tile(jnp.concatenate([-sin, sin], axis=-1), (1, NA_HEADS))
    return cos_t, sin_t


PROJ_TILE = 512
PROJ_COLS = 512
REST_WIDTH = GATE_OFF - POOL_OFF
GATE_WIDTH = N_BRANCH * D_MODEL
PROJ_VMEM_LIMIT = 56 * 1024 * 1024


def _swap_head_halves(x):
    w = x.shape[-1]
    lane = lax.broadcasted_iota(jnp.int32, (1, w), 1)
    lower = (lane % HEAD_DIM) < HEAD_DIM // 2
    return jnp.where(lower, pltpu.roll(x, w - HEAD_DIM // 2, axis=1), pltpu.roll(x, HEAD_DIM // 2, axis=1))


def _proj_kernel(x_ref, g_ref, shift_ref, scale_ref, w_ref, *refs, rope):
    if rope:
        cos_ref, sin_ref, qr_ref, qp_ref, k_ref, v_ref, rest_ref, gate_ref = refs
    else:
        qp_ref, k_ref, v_ref, rest_ref, gate_ref = refs
    x = x_ref[...]
    y = x * lax.rsqrt(jnp.mean(x * x, axis=-1, keepdims=True) + RMS_EPS) * g_ref[...]
    h = (y * (1.0 + scale_ref[...]) + shift_ref[...]).astype(jnp.bfloat16)

    def cols(c0, width=PROJ_COLS):
        return jnp.dot(h, w_ref[:, c0:c0 + width], preferred_element_type=jnp.float32)

    q = cols(0) * (HEAD_DIM ** -0.5)
    k = cols(BRANCH_WIDTH)
    qp_ref[...] = q.astype(jnp.bfloat16)
    if rope:
        cos_t, sin_t = cos_ref[...], sin_ref[...]
        qr_ref[...] = (q * cos_t + _swap_head_halves(q) * sin_t).astype(jnp.bfloat16)
        k = k * cos_t + _swap_head_halves(k) * sin_t
    k_ref[...] = k.astype(jnp.bfloat16)
    v_ref[...] = cols(2 * BRANCH_WIDTH).astype(jnp.bfloat16)
    for c0 in range(0, REST_WIDTH, PROJ_COLS):
        rest_ref[:, c0:c0 + PROJ_COLS] = cols(POOL_OFF + c0)
    for c0 in range(0, GATE_WIDTH, PROJ_COLS):
        gate_ref[:, c0:c0 + PROJ_COLS] = jax.nn.sigmoid(cols(GATE_OFF + c0)).astype(jnp.bfloat16)


def input_projection(x, g, shift, scale, w, rope_tables=None):
    b, n, d = x.shape
    tm = min(PROJ_TILE, n)
    rope = rope_tables is not None
    per_batch = shift.shape[0] > 1
    mod_spec = pl.BlockSpec((None, 1, d), (lambda i, j: (i, 0, 0)) if per_batch else (lambda i, j: (0, 0, 0)))
    tok = lambda width: pl.BlockSpec((None, tm, width), lambda i, j: (i, j, 0))
    in_specs = [tok(d), pl.BlockSpec((1, d), lambda i, j: (0, 0)), mod_spec, mod_spec,
                pl.BlockSpec((d, IN_WIDTH), lambda i, j: (0, 0), pipeline_mode=pl.Buffered(1))]
    args = [x, g.reshape(1, d), shift[:, None, :], scale[:, None, :], w]
    n_qkv = 3
    if rope:
        in_specs += [pl.BlockSpec((tm, BRANCH_WIDTH), lambda i, j: (j, 0))] * 2
        args += list(rope_tables)
        n_qkv = 4
    bf = jnp.bfloat16
    out_shape = ([jax.ShapeDtypeStruct((b, n, BRANCH_WIDTH), bf)] * n_qkv
                 + [jax.ShapeDtypeStruct((b, n, REST_WIDTH), jnp.float32), jax.ShapeDtypeStruct((b, n, GATE_WIDTH), bf)])
    out_specs = [tok(BRANCH_WIDTH)] * n_qkv + [tok(REST_WIDTH), tok(GATE_WIDTH)]
    return pl.pallas_call(
        functools.partial(_proj_kernel, rope=rope),
        grid=(b, n // tm),
        in_specs=in_specs,
        out_specs=out_specs,
        out_shape=out_shape,
        compiler_params=pltpu.CompilerParams(
            dimension_semantics=("parallel", "parallel"), vmem_limit_bytes=PROJ_VMEM_LIMIT),
        name="input_projection",
    )(*args)


NA_GROUP = 4
NA_SLAB = NA_GROUP + NA_KH_MAX
NA_HEADS_PER_STEP = 128 // HEAD_DIM


def _natten_bias_patterns(rpb, rows):
    j = jnp.arange(GRID_W)
    col_start = jnp.clip(j - NA_KW // 2, 0, GRID_W - NA_KW)
    col_mask = (j[None, :] >= col_start[:, None]) & (j[None, :] < col_start[:, None] + NA_KW)
    dc = jnp.clip(j[None, :] - j[:, None], -(NA_KW - 1), NA_KW - 1) + NA_KW - 1
    dc_hot = (dc[None, :, :] == jnp.arange(2 * NA_KW - 1)[:, None, None]).astype(jnp.float32)
    by_col = jnp.einsum('hrc,cqk->hrqk', rpb.astype(jnp.float32), dc_hot, precision=lax.Precision.HIGHEST)
    by_col = jnp.where(col_mask[None, None], by_col, NEG_INF)
    masked = jnp.full((NA_HEADS, GRID_W, GRID_W), NEG_INF, jnp.float32)
    n_groups = rows // NA_GROUP
    patterns = []
    for g in (0, 1, n_groups - 1):
        slab0 = min(max(g * NA_GROUP - NA_KH_MAX // 2, 0), rows - NA_SLAB)
        q_rows = []
        for r in range(g * NA_GROUP, (g + 1) * NA_GROUP):
            win0 = min(max(r - NA_KH_MAX // 2, 0), rows - NA_KH_MAX)
            blocks = [by_col[:, kr - r + NA_KH_MAX - 1] if win0 <= kr < win0 + NA_KH_MAX else masked
                      for kr in range(slab0, slab0 + NA_SLAB)]
            q_rows.append(jnp.concatenate(blocks, axis=-1))
        patterns.append(jnp.concatenate(q_rows, axis=1))
    return jnp.stack(patterns, axis=1)


def _softmax_pv(s_lat, s_ctx, v_lat, v_ctx):
    m = jnp.max(s_lat, axis=-1, keepdims=True)
    if s_ctx is not None:
        m = jnp.maximum(m, jnp.max(s_ctx, axis=-1, keepdims=True))
    p_lat = jnp.exp(s_lat - m)
    den = jnp.sum(p_lat, axis=-1, keepdims=True)
    o = jnp.dot(p_lat.astype(jnp.bfloat16), v_lat, preferred_element_type=jnp.float32)
    if s_ctx is not None:
        p_ctx = jnp.exp(s_ctx - m)
        den = den + jnp.sum(p_ctx, axis=-1, keepdims=True)
        o = o + jnp.dot(p_ctx.astype(jnp.bfloat16), v_ctx, preferred_element_type=jnp.float32)
    return o / den


def _head_masks():
    lane = lax.broadcasted_iota(jnp.int32, (1, NA_HEADS_PER_STEP * HEAD_DIM), 1)
    return [(lane >= h * HEAD_DIM) & (lane < (h + 1) * HEAD_DIM) for h in range(NA_HEADS_PER_STEP)]


def _natten_kernel(qr_ref, qp_ref, kr_ref, v_ref, kc_ref, vc_ref, bias_ref, o_ref):
    rows = qr_ref.shape[0] // GRID_W
    n_groups = rows // NA_GROUP
    nq = NA_GROUP * GRID_W
    nk = NA_SLAB * GRID_W
    kc = kc_ref[...]
    vc = vc_ref[...]
    nt = (((1,), (1,)), ((), ()))
    masks = _head_masks()

    def body(g, carry):
        slab0 = jnp.clip(g * NA_GROUP - NA_KH_MAX // 2, 0, rows - NA_SLAB)
        pattern = jnp.where(g == 0, 0, jnp.where(g == n_groups - 1, 2, 1))
        q0 = pl.multiple_of(g * nq, nq)
        k0 = pl.multiple_of(slab0 * GRID_W, GRID_W)
        qr = qr_ref[pl.ds(q0, nq), :]
        qp = qp_ref[pl.ds(q0, nq), :]
        ks = kr_ref[pl.ds(k0, nk), :]
        vs = v_ref[pl.ds(k0, nk), :]
        out = None
        for h, mask in enumerate(masks):
            s_lat = lax.dot_general(jnp.where(mask, qr, 0), ks, nt, preferred_element_type=jnp.float32)
            s_lat = s_lat + bias_ref[h, pattern]
            s_ctx = lax.dot_general(jnp.where(mask, qp, 0), kc, nt, preferred_element_type=jnp.float32)
            o = _softmax_pv(s_lat, s_ctx, vs, vc)
            out = o if out is None else jnp.where(mask, o, out)
        o_ref[pl.ds(q0, nq), :] = out.astype(o_ref.dtype)
        return carry

    lax.fori_loop(0, n_groups, body, 0)


def neighbourhood_attention(q_rot, q_plain, k_rot, v, k_ctx, v_ctx, rpb):
    b, s, width = q_rot.shape
    n_ctx = k_ctx.shape[1]
    lanes = NA_HEADS_PER_STEP * HEAD_DIM
    bias = _natten_bias_patterns(rpb, s // GRID_W)
    lat = pl.BlockSpec((None, s, lanes), lambda i, j: (i, 0, j))
    cx = pl.BlockSpec((None, n_ctx, lanes), lambda i, j: (i, 0, j))
    return pl.pallas_call(
        _natten_kernel,
        grid=(b, width // lanes),
        in_specs=[lat, lat, lat, lat, cx, cx,
                  pl.BlockSpec((NA_HEADS_PER_STEP, 3, NA_GROUP * GRID_W, NA_SLAB * GRID_W), lambda i, j: (j, 0, 0, 0))],
        out_specs=lat,
        out_shape=jax.ShapeDtypeStruct((b, s, width), jnp.float32),
        compiler_params=pltpu.CompilerParams(
            dimension_semantics=("parallel", "parallel"), vmem_limit_bytes=VMEM_LIMIT_BYTES),
        name="natten",
    )(q_rot, q_plain, k_rot, v, k_ctx, v_ctx, bias)


def _ctx_attn_kernel(q_ref, k_ref, v_ref, o_ref):
    nt = (((1,), (1,)), ((), ()))
    q, k, v = q_ref[...], k_ref[...], v_ref[...]
    out = None
    for mask in _head_masks():
        s = lax.dot_general(jnp.where(mask, q, 0), k, nt, preferred_element_type=jnp.float32)
        o = _softmax_pv(s, None, v, None)
        out = o if out is None else jnp.where(mask, o, out)
    o_ref[...] = out.astype(o_ref.dtype)


def context_attention(q, k, v):
    b, n, width = q.shape
    lanes = NA_HEADS_PER_STEP * HEAD_DIM
    spec = pl.BlockSpec((None, n, lanes), lambda i, j: (i, 0, j))
    return pl.pallas_call(
        _ctx_attn_kernel,
        grid=(b, width // lanes),
        in_specs=[spec, spec, spec],
        out_specs=spec,
        out_shape=jax.ShapeDtypeStruct((b, n, width), jnp.float32),
        compiler_params=pltpu.CompilerParams(dimension_semantics=("parallel", "parallel")),
        name="ctx_attention",
    )(q, k, v)


POOL_PAD = 16


def _pool_kernel(u_ref, w_ref, scale_ref, o_ref, pad_ref):
    n = u_ref.shape[0]
    u = u_ref[...]
    pad_ref[pl.ds(0, POOL_PAD), :] = jnp.zeros((POOL_PAD, POOL_GROUP_DIM), jnp.float32)
    pad_ref[pl.ds(POOL_PAD + n, POOL_PAD), :] = jnp.zeros((POOL_PAD, POOL_GROUP_DIM), jnp.float32)
    pad_ref[pl.ds(POOL_PAD, n), :] = u
    t = lax.broadcasted_iota(jnp.int32, (n, 1), 0)
    for g, w in enumerate(POOL_WINDOWS):
        @pl.when(pl.program_id(1) == g)
        def _(w=w):
            first = POOL_PAD - w // 2
            acc = pad_ref[pl.ds(first, n), :]
            for k in range(1, w):
                acc = acc + pad_ref[pl.ds(first + k, n), :]
            cnt = jnp.minimum(t + (w - w // 2), n) - jnp.maximum(t - w // 2, 0)
            y = acc / cnt.astype(jnp.float32) - u
            y = jnp.dot(y.astype(jnp.bfloat16), w_ref[...].astype(jnp.bfloat16), preferred_element_type=jnp.float32)
            o_ref[...] = y * scale_ref[...]


def multiscale_pool(rest, pool_w, pool_scale):
    b, n, _ = rest.shape
    col0 = 0
    return pl.pallas_call(
        _pool_kernel,
        grid=(b, POOL_GROUPS),
        in_specs=[pl.BlockSpec((None, n, POOL_GROUP_DIM), lambda i, g: (i, 0, col0 + g)),
                  pl.BlockSpec((None, POOL_GROUP_DIM, POOL_GROUP_DIM), lambda i, g: (g, 0, 0)),
                  pl.BlockSpec((1, POOL_GROUP_DIM), lambda i, g: (0, g))],
        out_specs=pl.BlockSpec((None, n, POOL_GROUP_DIM), lambda i, g: (i, 0, g)),
        out_shape=jax.ShapeDtypeStruct((b, n, BRANCH_WIDTH), jnp.float32),
        scratch_shapes=[pltpu.VMEM((n + 2 * POOL_PAD, POOL_GROUP_DIM), jnp.float32)],
        compiler_params=pltpu.CompilerParams(
            dimension_semantics=("parallel", "parallel"), vmem_limit_bytes=VMEM_LIMIT_BYTES),
        name="multiscale_pool",
    )(rest, pool_w, pool_scale.reshape(1, BRANCH_WIDTH))


HY_CHUNK = 64
HY_PITCH = 72
HY_LANES = 128
HY_FEAT = 128
HY_VMEM_LIMIT = 56 * 1024 * 1024


def _cis_tables(num, den):
    ang = (-2.0 * math.pi / den) * (num % den).astype(jnp.float32)
    return jnp.cos(ang), jnp.sin(ang)


def _stack_complex(mr, mi):
    return jnp.concatenate([jnp.concatenate([mr, -mi], axis=-1), jnp.concatenate([mi, mr], axis=-1)], axis=-2)


def _hy_dft_tables(n):
    big = 2 * n
    n1_full = big // HY_CHUNK
    k1 = jnp.arange(n1_full, dtype=jnp.int32)
    n1 = jnp.arange(n1_full, dtype=jnp.int32)
    n2 = jnp.arange(HY_CHUNK, dtype=jnp.int32)
    mr, mi = _cis_tables(k1[None, :, None] * (HY_CHUNK * n1[None, None, :] + n2[:, None, None]), big)
    half = n1_full // 2
    fwd_a = _stack_complex(mr[:, :, :half], mi[:, :, :half])
    fwd_a_real = jnp.concatenate([mr, mi], axis=-2)
    mrt = jnp.transpose(mr[:, :, :half], (0, 2, 1))
    mit = -jnp.transpose(mi[:, :, :half], (0, 2, 1))
    inv_a = _stack_complex(mrt, mit)
    er, ei = _cis_tables(n2[:, None] * n2[None, :], HY_CHUNK)
    fwd_b = _stack_complex(er, ei)
    inv_b = _stack_complex(er, -ei)
    bf = jnp.bfloat16
    return fwd_a.astype(bf), fwd_a_real.astype(bf), inv_a.astype(bf), fwd_b.astype(bf), inv_b.astype(bf)


def _hy_positions(n):
    f32 = jnp.float32
    t = jnp.linspace(0.0, 1.0, n, dtype=f32)[:, None]
    bands = (HY_EMB_DIM - 1) // 2
    ang = (2.0 * math.pi / n) * jnp.arange(n, dtype=f32)[:, None]
    f = jnp.linspace(1e-4, bands - 1, bands, dtype=f32)[None, :]
    z = jnp.concatenate([t, jnp.cos(f * ang), -jnp.sin(f * ang)], axis=-1)
    z = jnp.pad(z, ((0, 0), (0, HY_FEAT - HY_EMB_DIM)))
    zb = jnp.concatenate([z[:1], z[:0:-1]], axis=0)
    return z, zb


def _filter_mlp(z, w1, b1, fr, w2, b2):
    hp = lax.Precision.HIGHEST
    h = jnp.sin(fr * (jnp.dot(z, w1, precision=hp, preferred_element_type=jnp.float32) + b1))
    return jnp.sin(fr * (jnp.dot(h, w2, precision=hp, preferred_element_type=jnp.float32) + b2))


def _filter_taps(zf_ref, zb_ref, w1_ref, b1_ref, fr_ref, w2_ref, b2_ref, w3f_ref, w3b_ref, delta_ref, hid_ref):
    n = zf_ref.shape[0]
    hp = lax.Precision.HIGHEST

    @pl.when((pl.program_id(1) == 0) & (pl.program_id(2) == 0))
    def _():
        args = (w1_ref[...], b1_ref[...], fr_ref[...], w2_ref[...], b2_ref[...])
        hid_ref[0] = _filter_mlp(zf_ref[...], *args)
        hid_ref[1] = _filter_mlp(zb_ref[...], *args)

    delta = delta_ref[...]
    hf = jnp.dot(hid_ref[0], w3f_ref[...], precision=hp, preferred_element_type=jnp.float32)
    hf = hf * jnp.exp(-zf_ref[:, 0:1] * delta)
    hb = jnp.dot(hid_ref[1], w3b_ref[...], precision=hp, preferred_element_type=jnp.float32)
    hb = hb * jnp.exp(-zb_ref[:, 0:1] * delta)
    row = lax.broadcasted_iota(jnp.int32, (n, 1), 0)
    hb = jnp.where(row > 0, hb, 0.0)
    norm = jnp.sum(jnp.abs(hf), axis=0, keepdims=True) + jnp.sum(jnp.abs(hb), axis=0, keepdims=True)
    return hf / norm, hb / norm


def _filter_spec_kernel(zf_ref, zb_ref, w1_ref, b1_ref, fr_ref, w2_ref, b2_ref, w3f_ref, w3b_ref, delta_ref,
                        fa_ref, fb_ref, hr_ref, hi_ref, hid_ref, tbuf, sbuf):
    n = zf_ref.shape[0]
    chunks = n // HY_CHUNK
    n1_full = 2 * chunks
    hf, hb = _filter_taps(zf_ref, zb_ref, w1_ref, b1_ref, fr_ref, w2_ref, b2_ref, w3f_ref, w3b_ref, delta_ref, hid_ref)
    for c in range(chunks):
        tbuf[pl.ds(c * HY_PITCH, HY_CHUNK), :] = hf[c * HY_CHUNK:(c + 1) * HY_CHUNK]
        tbuf[pl.ds((chunks + c) * HY_PITCH, HY_CHUNK), :] = hb[c * HY_CHUNK:(c + 1) * HY_CHUNK]

    def stage_a(n2, carry):
        rows = tbuf[pl.ds(n2, n1_full, stride=HY_PITCH), :].astype(jnp.bfloat16)
        a = jnp.dot(fa_ref[n2], rows, preferred_element_type=jnp.float32)
        sbuf[0, pl.ds(n2, n1_full, stride=HY_PITCH), :] = a[:n1_full]
        sbuf[1, pl.ds(n2, n1_full, stride=HY_PITCH), :] = a[n1_full:]
        return carry

    lax.fori_loop(0, HY_CHUNK, stage_a, 0, unroll=4)
    inv_len = 1.0 / (2 * n)

    def stage_b(k1, carry):
        r0 = pl.multiple_of(k1 * HY_PITCH, 8)
        x = jnp.concatenate([sbuf[0, pl.ds(r0, HY_CHUNK), :], sbuf[1, pl.ds(r0, HY_CHUNK), :]], axis=0)
        y = jnp.dot(fb_ref[...], x.astype(jnp.bfloat16), preferred_element_type=jnp.float32) * inv_len
        o0 = pl.multiple_of(k1 * HY_CHUNK, HY_CHUNK)
        hr_ref[pl.ds(o0, HY_CHUNK), :] = y[:HY_CHUNK]
        hi_ref[pl.ds(o0, HY_CHUNK), :] = y[HY_CHUNK:]
        return carry

    lax.fori_loop(0, n1_full, stage_b, 0, unroll=8)


def hyena_filter_spectra(n, hy_w1, hy_b1, hy_freq, hy_w2, hy_b2, hy_w3, tables):
    depth = hy_w1.shape[0]
    zf, zb = _hy_positions(n)
    tiles = BRANCH_WIDTH // HY_LANES
    w1 = jnp.pad(hy_w1, ((0, 0), (0, HY_FEAT - HY_EMB_DIM), (0, 0)))
    deltas = jnp.abs(jnp.linspace(HY_MIN_DECAY, HY_MAX_DECAY, BRANCH_WIDTH, dtype=jnp.float32)).reshape(1, BRANCH_WIDTH)
    fwd_a_real, fwd_b = tables[1], tables[3]
    n1_full = 2 * n // HY_CHUNK
    one = pl.Buffered(1)
    full = lambda shape: pl.BlockSpec(shape, lambda l, o, j: (0,) * len(shape), pipeline_mode=one)
    per_layer = lambda shape: pl.BlockSpec((None,) + shape, lambda l, o, j: (l,) + (0,) * len(shape))
    out_spec = pl.BlockSpec((None, None, 2 * n, HY_LANES), lambda l, o, j: (l, o, 0, j), pipeline_mode=one)
    out_sds = jax.ShapeDtypeStruct((depth, HY_ORDER, 2 * n, BRANCH_WIDTH), jnp.float32)
    return pl.pallas_call(
        _filter_spec_kernel,
        grid=(depth, HY_ORDER, tiles),
        in_specs=[full((n, HY_FEAT)), full((n, HY_FEAT)),
                  per_layer((HY_FEAT, HY_FILTER_HIDDEN)), per_layer((1, HY_FILTER_HIDDEN)),
                  per_layer((1, HY_FILTER_HIDDEN)),
                  per_layer((HY_FILTER_HIDDEN, HY_FILTER_HIDDEN)), per_layer((1, HY_FILTER_HIDDEN)),
                  pl.BlockSpec((None, HY_FILTER_HIDDEN, HY_LANES), lambda l, o, j: (l, 0, o * 2 * tiles + j)),
                  pl.BlockSpec((None, HY_FILTER_HIDDEN, HY_LANES), lambda l, o, j: (l, 0, o * 2 * tiles + tiles + j)),
                  pl.BlockSpec((1, HY_LANES), lambda l, o, j: (0, j)),
                  full((HY_CHUNK, 2 * n1_full, n1_full)), full((2 * HY_CHUNK, 2 * HY_CHUNK))],
        out_specs=[out_spec, out_spec],
        out_shape=[out_sds, out_sds],
        scratch_shapes=[pltpu.VMEM((2, n, HY_FILTER_HIDDEN), jnp.float32),
                        pltpu.VMEM((n1_full * HY_PITCH, HY_LANES), jnp.float32),
                        pltpu.VMEM((2, n1_full * HY_PITCH, HY_LANES), jnp.float32)],
        compiler_params=pltpu.CompilerParams(
            dimension_semantics=("arbitrary", "arbitrary", "arbitrary"), vmem_limit_bytes=HY_VMEM_LIMIT),
        name="hyena_filter_spectra",
    )(zf, zb, w1, hy_b1[:, None, :], hy_freq[:, None, :], hy_w2, hy_b2[:, None, :], hy_w3, hy_w3, deltas,
      fwd_a_real, fwd_b)


def _short_conv(u_ref, b, w_ref, bias_ref, pad_ref):
    n = u_ref.shape[1]
    x = u_ref[b]
    pad_ref[pl.ds(0, 8), :] = jnp.zeros((8, HY_LANES), jnp.float32)
    pad_ref[pl.ds(8 + n, 8), :] = jnp.zeros((8, HY_LANES), jnp.float32)
    pad_ref[pl.ds(8, n), :] = x
    return (pad_ref[pl.ds(7, n), :] * w_ref[0:1, :] + x * w_ref[1:2, :] + pad_ref[pl.ds(9, n), :] * w_ref[2:3, :]
            + bias_ref[...])


def _hyena_conv_kernel(z_ref, g_ref, wz_ref, bz_ref, wg_ref, bg_ref, skip_ref, hr_ref, hi_ref,
                       fa_ref, ia_ref, fb_ref, ib_ref, o_ref, pad_ref, zbuf, sbuf, *, conv_z):
    n = z_ref.shape[1]
    chunks = n // HY_CHUNK
    n1_full = 2 * chunks
    for b in range(2):
        z = _short_conv(z_ref, b, wz_ref, bz_ref, pad_ref) if conv_z else z_ref[b]
        for c in range(chunks):
            zbuf[b, pl.ds(c * HY_PITCH, HY_CHUNK), :] = z[c * HY_CHUNK:(c + 1) * HY_CHUNK]

    def stage_a(n2, carry):
        x = jnp.concatenate([zbuf[0, pl.ds(n2, chunks, stride=HY_PITCH), :],
                             zbuf[1, pl.ds(n2, chunks, stride=HY_PITCH), :]], axis=0)
        a = jnp.dot(fa_ref[n2], x.astype(jnp.bfloat16), preferred_element_type=jnp.float32)
        sbuf[0, pl.ds(n2, n1_full, stride=HY_PITCH), :] = a[:n1_full]
        sbuf[1, pl.ds(n2, n1_full, stride=HY_PITCH), :] = a[n1_full:]
        return carry

    lax.fori_loop(0, HY_CHUNK, stage_a, 0, unroll=4)

    def stage_b(k1, carry):
        r0 = pl.multiple_of(k1 * HY_PITCH, 8)
        x = jnp.concatenate([sbuf[0, pl.ds(r0, HY_CHUNK), :], sbuf[1, pl.ds(r0, HY_CHUNK), :]], axis=0)
        s = jnp.dot(fb_ref[...], x.astype(jnp.bfloat16), preferred_element_type=jnp.float32)
        h0 = pl.multiple_of(k1 * HY_CHUNK, HY_CHUNK)
        hr = hr_ref[pl.ds(h0, HY_CHUNK), :]
        hi = hi_ref[pl.ds(h0, HY_CHUNK), :]
        sr, si = s[:HY_CHUNK], s[HY_CHUNK:]
        y = jnp.concatenate([sr * hr - si * hi, sr * hi + si * hr], axis=0)
        c = jnp.dot(ib_ref[...], y.astype(jnp.bfloat16), preferred_element_type=jnp.float32)
        sbuf[0, pl.ds(r0, HY_CHUNK), :] = c[:HY_CHUNK]
        sbuf[1, pl.ds(r0, HY_CHUNK), :] = c[HY_CHUNK:]
        return carry

    lax.fori_loop(0, n1_full, stage_b, 0, unroll=8)
    skip = skip_ref[...]

    def stage_c(n2, carry):
        x = jnp.concatenate([sbuf[0, pl.ds(n2, n1_full, stride=HY_PITCH), :],
                             sbuf[1, pl.ds(n2, n1_full, stride=HY_PITCH), :]], axis=0)
        y = jnp.dot(ia_ref[n2], x.astype(jnp.bfloat16), preferred_element_type=jnp.float32)
        for b in range(2):
            zb = zbuf[b, pl.ds(n2, chunks, stride=HY_PITCH), :]
            zbuf[b, pl.ds(n2, chunks, stride=HY_PITCH), :] = y[b * chunks:(b + 1) * chunks] + zb * skip
        return carry

    lax.fori_loop(0, HY_CHUNK, stage_c, 0, unroll=4)
    for b in range(2):
        g = _short_conv(g_ref, b, wg_ref, bg_ref, pad_ref)
        for c in range(chunks):
            o_ref[b, pl.ds(c * HY_CHUNK, HY_CHUNK), :] = (
                g[c * HY_CHUNK:(c + 1) * HY_CHUNK] * zbuf[b, pl.ds(c * HY_PITCH, HY_CHUNK), :])


def hyena_long_conv(z, z_col0, g, g_col0, conv_z, short_w, short_b, w_off_z, w_off_g, skip, hr, hi, layer, order,
                    tables):
    b, n, _ = g.shape
    tiles = BRANCH_WIDTH // HY_LANES
    fwd_a, _, inv_a, fwd_b, inv_b = tables
    n1_full = 2 * n // HY_CHUNK
    one = pl.Buffered(1)
    zc, gc = z_col0 // HY_LANES, g_col0 // HY_LANES
    wz, wg = w_off_z // HY_LANES, w_off_g // HY_LANES
    const = lambda shape: pl.BlockSpec(shape, lambda j, p: (0,) * len(shape), pipeline_mode=one)
    return pl.pallas_call(
        functools.partial(_hyena_conv_kernel, conv_z=conv_z),
        grid=(tiles, b // 2),
        in_specs=[pl.BlockSpec((2, n, HY_LANES), lambda j, p: (p, 0, zc + j), pipeline_mode=one),
                  pl.BlockSpec((2, n, HY_LANES), lambda j, p: (p, 0, gc + j), pipeline_mode=one),
                  pl.BlockSpec((HY_SHORT, HY_LANES), lambda j, p: (0, wz + j)),
                  pl.BlockSpec((1, HY_LANES), lambda j, p: (0, wz + j)),
                  pl.BlockSpec((HY_SHORT, HY_LANES), lambda j, p: (0, wg + j)),
                  pl.BlockSpec((1, HY_LANES), lambda j, p: (0, wg + j)),
                  pl.BlockSpec((1, HY_LANES), lambda j, p: (0, j)),
                  pl.BlockSpec((None, None, 2 * n, HY_LANES), lambda j, p: (layer, order, 0, j), pipeline_mode=one),
                  pl.BlockSpec((None, None, 2 * n, HY_LANES), lambda j, p: (layer, order, 0, j), pipeline_mode=one),
                  const((HY_CHUNK, 2 * n1_full, n1_full)), const((HY_CHUNK, n1_full, 2 * n1_full)),
                  const((2 * HY_CHUNK, 2 * HY_CHUNK)), const((2 * HY_CHUNK, 2 * HY_CHUNK))],
        out_specs=pl.BlockSpec((2, n, HY_LANES), lambda j, p: (p, 0, j), pipeline_mode=one),
        out_shape=jax.ShapeDtypeStruct((b, n, BRANCH_WIDTH), jnp.float32),
        scratch_shapes=[pltpu.VMEM((n + 16, HY_LANES), jnp.float32),
                        pltpu.VMEM((2, (n // HY_CHUNK) * HY_PITCH, HY_LANES), jnp.float32),
                        pltpu.VMEM((2, n1_full * HY_PITCH, HY_LANES), jnp.float32)],
        compiler_params=pltpu.CompilerParams(
            dimension_semantics=("parallel", "parallel"), vmem_limit_bytes=HY_VMEM_LIMIT),
        name="hyena_long_conv",
    )(z, g, short_w, short_b.reshape(1, -1), short_w, short_b.reshape(1, -1), skip.reshape(1, -1), hr, hi,
      fwd_a, inv_a, fwd_b, inv_b)


def hyena_mixer(rest, short_w, short_b, skip, hr, hi, layer, tables):
    off = HY_OFF - POOL_OFF
    z1 = hyena_long_conv(rest, off, rest, off + BRANCH_WIDTH, True, short_w, short_b, 0, BRANCH_WIDTH,
                         skip[0], hr, hi, layer, 0, tables)
    return hyena_long_conv(z1, 0, rest, off + 2 * BRANCH_WIDTH, False, short_w, short_b, 0, 2 * BRANCH_WIDTH,
                           skip[1], hr, hi, layer, 1, tables)


MERGE_TILE = 512


def _merge_kernel(ya_ref, yp_ref, yh_ref, gate_ref, x_ref, res_gate_ref, wb_ref, wo_ref, g2_ref, shift_ref, scale_ref,
                  wr_hi_ref, wr_lo_ref, xo_ref, h_ref, logit_ref):
    d = x_ref.shape[-1]
    merged = None
    for i, y_ref in enumerate((ya_ref, yp_ref, yh_ref)):
        br = jnp.dot(y_ref[...].astype(jnp.bfloat16), wb_ref[i], preferred_element_type=jnp.float32)
        term = gate_ref[:, i * d:(i + 1) * d].astype(jnp.float32) * br
        merged = term if merged is None else merged + term
    out = jnp.dot(merged.astype(jnp.bfloat16), wo_ref[...], preferred_element_type=jnp.float32)
    x = x_ref[...] + res_gate_ref[...] * out
    xo_ref[...] = x
    y = x * lax.rsqrt(jnp.mean(x * x, axis=-1, keepdims=True) + RMS_EPS) * g2_ref[...]
    h = y * (1.0 + scale_ref[...]) + shift_ref[...]
    _rows_to_tiles(h_ref, (), h)
    h_hi = h.astype(jnp.bfloat16)
    h_lo = (h - h_hi.astype(jnp.float32)).astype(jnp.bfloat16)
    wr_hi = wr_hi_ref[...]
    logit_ref[...] = (jnp.dot(h_hi, wr_hi, preferred_element_type=jnp.float32)
                      + (jnp.dot(h_hi, wr_lo_ref[...], preferred_element_type=jnp.float32)
                         + jnp.dot(h_lo, wr_hi, preferred_element_type=jnp.float32)))


def merge_and_norm(y_attn, y_pool, y_hy, gates, x, mod, w_branch, w_out, g2, w_router):
    b, n, d = x.shape
    tm = min(MERGE_TILE, n)
    per_batch = mod.shape[0] > 1
    mod_spec = lambda k: pl.BlockSpec((None, None, 1, d),
                                      (lambda i, j: (i, k, 0, 0)) if per_batch else (lambda i, j: (0, k, 0, 0)))
    tok = lambda width: pl.BlockSpec((None, tm, width), lambda i, j: (i, j, 0))
    const = lambda shape: pl.BlockSpec(shape, lambda i, j: (0,) * len(shape))
    call = pl.pallas_call(
        _merge_kernel,
        grid=(b, n // tm),
        in_specs=[tok(BRANCH_WIDTH), tok(BRANCH_WIDTH), tok(BRANCH_WIDTH), tok(GATE_WIDTH), tok(d), mod_spec(2),
                  const((N_BRANCH, BRANCH_WIDTH, d)), const((d, d)), const((1, d)), mod_spec(3), mod_spec(4),
                  const((d, N_EXPERTS)), const((d, N_EXPERTS))],
        out_specs=[tok(d), pl.BlockSpec((None, tm * ROW_TILE, d // ROW_TILE), lambda i, j: (i, j, 0)), tok(N_EXPERTS)],
        out_shape=[jax.ShapeDtypeStruct((b, n, d), jnp.float32),
                   jax.ShapeDtypeStruct((b, n * ROW_TILE, d // ROW_TILE), jnp.float32),
                   jax.ShapeDtypeStruct((b, n, N_EXPERTS), jnp.float32)],
        compiler_params=pltpu.CompilerParams(
            dimension_semantics=("parallel", "parallel"), vmem_limit_bytes=VMEM_LIMIT_BYTES),
        name="merge_and_norm",
    )
    mod4 = mod[:, :, None, :]
    wr_hi = w_router.astype(jnp.bfloat16)
    wr_lo = (w_router - wr_hi.astype(jnp.float32)).astype(jnp.bfloat16)
    return call(y_attn, y_pool, y_hy, gates, x, mod4, w_branch, w_out, g2.reshape(1, d), mod4, mod4, wr_hi, wr_lo)


def _combine_kernel(y0_ref, y1_ref, w_ref, x_ref, gate_ref, g_ref, o_ref, *, final_norm):
    w = w_ref[...]
    m = x_ref.shape[0]
    f = _tiles_to_rows(y0_ref, (), m) * w[:, 0:1] + _tiles_to_rows(y1_ref, (), m) * w[:, 1:2]
    x = x_ref[...] + gate_ref[...] * f
    if final_norm:
        x = x * lax.rsqrt(jnp.mean(x * x, axis=-1, keepdims=True) + RMS_EPS) * g_ref[...]
    o_ref[...] = x


def moe_combine(y, wsel, row0, x, mod, final_g=None):
    b, n, d = x.shape
    tm = min(MERGE_TILE, n)
    tiles = n // tm
    blk0 = row0 // tm
    second = wsel.shape[0] // tm
    per_batch = mod.shape[0] > 1
    tok = lambda width: pl.BlockSpec((None, tm, width), lambda i, j: (i, j, 0))
    flat = lambda width, off=0: pl.BlockSpec((tm, width), lambda i, j: (off + blk0 + i * tiles + j, 0))
    tiled = lambda off: pl.BlockSpec((tm * ROW_TILE, d // ROW_TILE), lambda i, j: (off + blk0 + i * tiles + j, 0))
    k = N_MOD - 1
    gate_spec = pl.BlockSpec((None, None, 1, d), (lambda i, j: (i, k, 0, 0)) if per_batch else (lambda i, j: (0, k, 0, 0)))
    g = jnp.ones((1, d), jnp.float32) if final_g is None else final_g.reshape(1, d)
    return pl.pallas_call(
        functools.partial(_combine_kernel, final_norm=final_g is not None),
        grid=(b, n // tm),
        in_specs=[tiled(0), tiled(second), flat(TOP_K), tok(d), gate_spec, pl.BlockSpec((1, d), lambda i, j: (0, 0))],
        out_specs=tok(d),
        out_shape=jax.ShapeDtypeStruct((b, n, d), jnp.float32),
        compiler_params=pltpu.CompilerParams(
            dimension_semantics=("parallel", "parallel"), vmem_limit_bytes=VMEM_LIMIT_BYTES),
        name="moe_combine",
    )(y, y, wsel, x, mod[:, :, None, :], g)


def latent_mixers(x, g1, mod, w_in, rope_tables, k_ctx, v_ctx, rpb, pool_w, pool_scale, hy, w_branch, w_out, g2, w_router):
    q_rot, q_plain, k, v, rest, gates = input_projection(x, g1, mod[:, 0], mod[:, 1], w_in, rope_tables)
    y_attn = neighbourhood_attention(q_rot, q_plain, k, v, k_ctx, v_ctx, rpb)
    y_pool = multiscale_pool(rest, pool_w, pool_scale)
    y_hy = hyena_mixer(rest, *hy)
    return merge_and_norm(y_attn, y_pool, y_hy, gates, x, mod, w_branch, w_out, g2, w_router)


def kernel(x, c, ctx, c_ctx, norm1_g, norm2_g, w_mod, b_mod, w_in, rpb, pool_w, pool_scale, hy_short_w, hy_short_b, hy_w1, hy_b1, hy_freq, hy_w2, hy_b2, hy_w3, hy_skip, w_branch, w_out, w_router, router_bias, w_gate_e, w_up_e, w_down_e, final_g):
    b, s, d = x.shape
    n_ctx = ctx.shape[1]
    bf = jnp.bfloat16
    rope_tables = axial_rope_tables(s)
    tables_l = _hy_dft_tables(s)
    tables_c = _hy_dft_tables(n_ctx)
    spec_l = hyena_filter_spectra(s, hy_w1, hy_b1, hy_freq, hy_w2, hy_b2, hy_w3, tables_l)
    spec_c = hyena_filter_spectra(n_ctx, hy_w1[:DEPTH - 1], hy_b1[:DEPTH - 1], hy_freq[:DEPTH - 1], hy_w2[:DEPTH - 1],
                                  hy_b2[:DEPTH - 1], hy_w3[:DEPTH - 1], tables_c)
    cond = jnp.concatenate([c, c_ctx[None, :], jnp.zeros((8 - (b + 1) % 8, d), c.dtype)], axis=0)
    cond = jax.nn.silu(cond)
    xl, xc = x, ctx
    for l in range(DEPTH):
        need_ctx = l < DEPTH - 1
        mod = (matmul(cond, w_mod[l]) + b_mod[l]).reshape(-1, N_MOD, d)
        ml, mc = mod[:b], mod[b:b + 1]
        w_in_l, w_branch_l, w_out_l = w_in[l].astype(bf), w_branch[l].astype(bf), w_out[l].astype(bf)
        common = (pool_w[l], pool_scale[l])
        hy_c = (hy_short_w[l], hy_short_b[l], hy_skip[l])
        qc, kc, vc, rest_c, gates_c = input_projection(xc, norm1_g[l], mc[:, 0], mc[:, 1], w_in_l)
        xl, hl2, logit_l = latent_mixers(xl, norm1_g[l], ml, w_in_l, rope_tables, kc, vc, rpb[l], *common,
                                         hy_c + (spec_l[0], spec_l[1], l, tables_l), w_branch_l, w_out_l,
                                         norm2_g[l], w_router)
        tokens, logits = hl2.reshape(-1, d // ROW_TILE), logit_l.reshape(b * s, N_EXPERTS)
        if need_ctx:
            ya_c = context_attention(qc, kc, vc)
            yp_c = multiscale_pool(rest_c, *common)
            yh_c = hyena_mixer(rest_c, *hy_c, spec_c[0], spec_c[1], l, tables_c)
            xc, hc2, logit_c = merge_and_norm(ya_c, yp_c, yh_c, gates_c, xc, mc, w_branch_l, w_out_l, norm2_g[l], w_router)
            tokens = jnp.concatenate([tokens, hc2.reshape(-1, d // ROW_TILE)], axis=0)
            logits = jnp.concatenate([logits, logit_c.reshape(b * n_ctx, N_EXPERTS)], axis=0)
        y, wsel = grouped_moe(tokens, logits, router_bias, l, w_gate_e, w_up_e, w_down_e)
        xl = moe_combine(y, wsel, 0, xl, ml, final_g if l == DEPTH - 1 else None)
        if need_ctx:
            xc = moe_combine(y, wsel, b * s, xc, mc)
    return xl
```

```python
import functools
import math

import jax
import jax.numpy as jnp
from jax import lax
from jax.experimental import pallas as pl
from jax.experimental.pallas import tpu as pltpu

D_MODEL = 1024
DEPTH = 4
GRID_W = 64
BRANCH_WIDTH = D_MODEL // 2
N_BRANCH = 3
HEAD_DIM = 64
NA_HEADS = BRANCH_WIDTH // HEAD_DIM
NA_KH_MAX = 8
NA_KW = 16
ROPE_THETA = 10000.0
NEG_INF = -1e30
POOL_WINDOWS = (2, 4, 8, 16)
POOL_GROUPS = len(POOL_WINDOWS)
POOL_GROUP_DIM = BRANCH_WIDTH // POOL_GROUPS
HY_ORDER = 2
HY_SHORT = 3
HY_EMB_DIM = 33
HY_FILTER_HIDDEN = 64
HY_MIN_DECAY = math.log(1e-2) / 0.3
HY_MAX_DECAY = math.log(1e-2) / 1.5
N_EXPERTS = 16
N_GROUPS = 4
EXPERTS_PER_GROUP = N_EXPERTS // N_GROUPS
TOP_K = 2
EXPERT_FF = D_MODEL // 2
N_MOD = 6
RMS_EPS = 1e-6
POOL_OFF = 3 * BRANCH_WIDTH
HY_OFF = POOL_OFF + BRANCH_WIDTH
GATE_OFF = HY_OFF + (HY_ORDER + 1) * BRANCH_WIDTH
IN_WIDTH = GATE_OFF + N_BRANCH * D_MODEL

VMEM_LIMIT_BYTES = 48 * 1024 * 1024
MOE_TILE = 256
ROWS_PER_ISSUE = 8


def _mm_kernel(a_ref, b_ref, o_ref):
    o_ref[...] = jnp.dot(a_ref[...].astype(jnp.bfloat16), b_ref[...].astype(jnp.bfloat16),
                         preferred_element_type=jnp.float32).astype(o_ref.dtype)


def _pick_tile(n, cands):
    for c in cands:
        if n % c == 0:
            return c
    return n


def matmul(a, b, out_dtype=jnp.float32):
    m, k = a.shape
    _, n = b.shape
    tm = _pick_tile(m, (1024, 512, 256, 128, 64, 32, 16, 8))
    tn = _pick_tile(n, (1664, 1024, 512, 256, 128))
    return pl.pallas_call(
        _mm_kernel,
        grid=(m // tm, n // tn),
        in_specs=[pl.BlockSpec((tm, k), lambda i, j: (i, 0)),
                  pl.BlockSpec((k, tn), lambda i, j: (0, j))],
        out_specs=pl.BlockSpec((tm, tn), lambda i, j: (i, j)),
        out_shape=jax.ShapeDtypeStruct((m, n), out_dtype),
        compiler_params=pltpu.CompilerParams(
            dimension_semantics=("parallel", "parallel"), vmem_limit_bytes=VMEM_LIMIT_BYTES),
        name="dense_matmul",
    )(a, b)


ROW_TILE = 8


def _rows_to_tiles(ref, idx, x):
    m = x.shape[0]
    for c in range(ROW_TILE):
        ref[idx + (pl.ds(c, m, stride=ROW_TILE), slice(None))] = x[:, c * 128:(c + 1) * 128]


def _tiles_to_rows(ref, idx, m):
    return jnp.concatenate([ref[idx + (pl.ds(c, m, stride=ROW_TILE), slice(None))] for c in range(ROW_TILE)], axis=1)


def _expert_kernel(src_ref, dst_ref, tile_expert_ref, n_valid_ref, h_hbm, wg_ref, wu_ref, wd_ref, y_hbm,
                   xbuf, obuf, zbuf, gsem, ssem, zsem):
    i = pl.program_id(0)
    n_valid = n_valid_ref[0]
    slot = i % 2

    def token_copies(index_ref, base, make):
        def tokens(g, carry):
            r0 = g * ROWS_PER_ISSUE
            for j in range(ROWS_PER_ISSUE):
                hbm_row = pl.multiple_of(index_ref[base + r0 + j], ROW_TILE)
                vmem_row = pl.multiple_of((r0 + j) * ROW_TILE, ROW_TILE)
                make(pl.ds(hbm_row, ROW_TILE), pl.ds(vmem_row, ROW_TILE)).start()
            return carry
        lax.fori_loop(0, MOE_TILE // ROWS_PER_ISSUE, tokens, 0)

    def gather(tile, s):
        token_copies(src_ref, tile * MOE_TILE,
                     lambda hbm, vmem: pltpu.make_async_copy(h_hbm.at[hbm], xbuf.at[s, vmem], gsem.at[s]))

    def wait_all(buf, sem, s):
        pltpu.make_async_copy(buf.at[s], buf.at[s], sem.at[s]).wait()

    @pl.when((i == 0) & (n_valid > 0))
    def _():
        gather(0, 0)

    @pl.when(i + 1 < n_valid)
    def _():
        gather(i + 1, 1 - slot)

    @pl.when(i < n_valid)
    def _():
        wait_all(xbuf, gsem, slot)

        @pl.when(i >= 2)
        def _():
            wait_all(obuf, ssem, slot)

        x = _tiles_to_rows(xbuf, (slot,), MOE_TILE).astype(jnp.bfloat16)
        g = jnp.dot(x, wg_ref[...].astype(jnp.bfloat16), preferred_element_type=jnp.float32)
        u = jnp.dot(x, wu_ref[...].astype(jnp.bfloat16), preferred_element_type=jnp.float32)
        hid = (g * jax.nn.sigmoid(g)) * u
        y = jnp.dot(hid.astype(jnp.bfloat16), wd_ref[...].astype(jnp.bfloat16), preferred_element_type=jnp.float32)
        _rows_to_tiles(obuf, (slot,), y)
        token_copies(dst_ref, i * MOE_TILE,
                     lambda hbm, vmem: pltpu.make_async_copy(obuf.at[slot, vmem], y_hbm.at[hbm], ssem.at[slot]))

    @pl.when(i >= n_valid)
    def _():
        zbuf[...] = jnp.zeros_like(zbuf)
        token_copies(dst_ref, i * MOE_TILE,
                     lambda hbm, vmem: pltpu.make_async_copy(zbuf.at[vmem], y_hbm.at[hbm], zsem.at[0]))
        pltpu.make_async_copy(zbuf, zbuf, zsem.at[0]).wait()

    @pl.when(i == pl.num_programs(0) - 1)
    def _():
        @pl.when(n_valid >= 1)
        def _():
            wait_all(obuf, ssem, (n_valid - 1) % 2)

        @pl.when(n_valid >= 2)
        def _():
            wait_all(obuf, ssem, n_valid % 2)


def expert_ffn(h, src_row, dst_row, tile_expert, n_valid, layer, w_gate, w_up, w_down):
    d = ROW_TILE * h.shape[1]
    p = src_row.shape[0]
    n_tiles = p // MOE_TILE
    buf = (MOE_TILE * ROW_TILE, h.shape[1])
    grid_spec = pltpu.PrefetchScalarGridSpec(
        num_scalar_prefetch=4,
        grid=(n_tiles,),
        in_specs=[
            pl.BlockSpec(memory_space=pl.ANY),
            pl.BlockSpec((None, None, d, EXPERT_FF), lambda i, s, t, te, nv: (layer, te[i], 0, 0)),
            pl.BlockSpec((None, None, d, EXPERT_FF), lambda i, s, t, te, nv: (layer, te[i], 0, 0)),
            pl.BlockSpec((None, None, EXPERT_FF, d), lambda i, s, t, te, nv: (layer, te[i], 0, 0)),
        ],
        out_specs=pl.BlockSpec(memory_space=pl.ANY),
        scratch_shapes=[pltpu.VMEM((2,) + buf, jnp.float32), pltpu.VMEM((2,) + buf, jnp.float32),
                        pltpu.VMEM(buf, jnp.float32),
                        pltpu.SemaphoreType.DMA((2,)), pltpu.SemaphoreType.DMA((2,)), pltpu.SemaphoreType.DMA((1,))],
    )
    return pl.pallas_call(
        _expert_kernel,
        grid_spec=grid_spec,
        out_shape=jax.ShapeDtypeStruct((p * ROW_TILE, h.shape[1]), jnp.float32),
        compiler_params=pltpu.CompilerParams(
            dimension_semantics=("arbitrary",), vmem_limit_bytes=VMEM_LIMIT_BYTES),
        name="expert_ffn",
    )(src_row * ROW_TILE, dst_row * ROW_TILE, tile_expert, n_valid, h, w_gate, w_up, w_down)


def grouped_moe(h, logits, router_bias, layer, w_gate, w_up, w_down):
    t = logits.shape[0]
    aff = jax.nn.sigmoid(logits)
    biased = (aff + router_bias.astype(jnp.float32)).reshape(-1, N_GROUPS, EXPERTS_PER_GROUP)
    idx = jnp.arange(EXPERTS_PER_GROUP, dtype=jnp.int32)
    vi, vj = biased[..., :, None], biased[..., None, :]
    ahead = (vj > vi) | ((vj == vi) & (idx[None, :] < idx[:, None]))
    rank_in_group = jnp.sum(ahead.astype(jnp.int32), axis=-1)
    group_score = jnp.sum(jnp.where(rank_in_group < TOP_K, biased, 0.0), axis=-1)
    group = jnp.argmax(group_score, axis=-1)
    group_hot = group[:, None] == jnp.arange(N_GROUPS)[None, :]
    rank_sel = jnp.sum(jnp.where(group_hot[:, :, None], rank_in_group, 0), axis=1)
    local = jnp.sum(jnp.where(rank_sel[:, None, :] == jnp.arange(TOP_K)[None, :, None], idx[None, None, :], 0), axis=-1)
    expert = (group[:, None] * EXPERTS_PER_GROUP + local).astype(jnp.int32)
    expert_hot = expert[:, :, None] == jnp.arange(N_EXPERTS, dtype=jnp.int32)[None, None, :]
    wsel = jnp.sum(jnp.where(expert_hot, aff[:, None, :], 0.0), axis=-1)
    wsel = wsel / jnp.sum(wsel, axis=-1, keepdims=True)

    flat_e = expert.reshape(-1)
    onehot = (flat_e[:, None] == jnp.arange(N_EXPERTS, dtype=jnp.int32)[None, :]).astype(jnp.int32)
    csum = jnp.cumsum(onehot, axis=0)
    rank = jnp.sum(csum * onehot, axis=1) - 1
    counts = csum[-1]
    tiles_per = (counts + MOE_TILE - 1) // MOE_TILE
    tile_end = jnp.cumsum(tiles_per)
    tile_start = tile_end - tiles_per
    pos = jnp.sum(onehot * tile_start[None, :], axis=1) * MOE_TILE + rank
    n_tiles = (t * TOP_K) // MOE_TILE + N_EXPERTS
    p = n_tiles * MOE_TILE
    n_assign = t * TOP_K
    assign_of = jnp.full((p,), -1, jnp.int32).at[pos].set(jnp.arange(n_assign, dtype=jnp.int32))
    is_pad = assign_of < 0
    src_row = jnp.where(is_pad, 0, assign_of // TOP_K)
    dst_row = jnp.where(is_pad, n_assign + jnp.cumsum(is_pad.astype(jnp.int32)) - 1,
                        (assign_of % TOP_K) * t + assign_of // TOP_K)
    tile_ids = jnp.arange(n_tiles, dtype=jnp.int32)
    n_valid = tile_end[-1].astype(jnp.int32)
    tile_expert = jnp.minimum(jnp.sum((tile_end[None, :] <= tile_ids[:, None]).astype(jnp.int32), axis=1), N_EXPERTS - 1)
    last_expert = tile_expert[jnp.maximum(n_valid - 1, 0)]
    tile_expert = jnp.where(tile_ids < n_valid, tile_expert, last_expert)

    y = expert_ffn(h, src_row, dst_row, tile_expert, n_valid.reshape(1), layer, w_gate, w_up, w_down)
    return y, wsel


def axial_rope_tables(n_tokens):
    t = jnp.arange(n_tokens)
    row = (t // GRID_W).astype(jnp.float32)
    col = (t % GRID_W).astype(jnp.float32)
    n_freq = HEAD_DIM // 4
    inv = ROPE_THETA ** (-jnp.arange(n_freq, dtype=jnp.float32) / n_freq)
    ang = jnp.concatenate([row[:, None] * inv, col[:, None] * inv], axis=-1)
    cos, sin = jnp.cos(ang), jnp.sin(ang)
    cos_t = jnp.tile(jnp.concatenate([cos, cos], axis=-1), (1, NA_HEADS))
    sin_t = jnp.tile(jnp.concatenate([-sin, sin], axis=-1), (1, NA_HEADS))
    return cos_t, sin_t


PROJ_TILE = 512
PROJ_COLS = 512
REST_WIDTH = GATE_OFF - POOL_OFF
GATE_WIDTH = N_BRANCH * D_MODEL
PROJ_VMEM_LIMIT = 56 * 1024 * 1024


def _swap_head_halves(x):
    w = x.shape[-1]
    lane = lax.broadcasted_iota(jnp.int32, (1, w), 1)
    lower = (lane % HEAD_DIM) < HEAD_DIM // 2
    return jnp.where(lower, pltpu.roll(x, w - HEAD_DIM // 2, axis=1), pltpu.roll(x, HEAD_DIM // 2, axis=1))


def _proj_kernel(x_ref, g_ref, shift_ref, scale_ref, w_ref, *refs, rope):
    if rope:
        cos_ref, sin_ref, qr_ref, qp_ref, k_ref, v_ref, rest_ref, gate_ref = refs
    else:
        qp_ref, k_ref, v_ref, rest_ref, gate_ref = refs
    x = x_ref[...]
    y = x * lax.rsqrt(jnp.mean(x * x, axis=-1, keepdims=True) + RMS_EPS) * g_ref[...]
    h = (y * (1.0 + scale_ref[...]) + shift_ref[...]).astype(jnp.bfloat16)

    def cols(c0, width=PROJ_COLS):
        return jnp.dot(h, w_ref[:, c0:c0 + width], preferred_element_type=jnp.float32)

    q = cols(0) * (HEAD_DIM ** -0.5)
    k = cols(BRANCH_WIDTH)
    qp_ref[...] = q.astype(jnp.bfloat16)
    if rope:
        cos_t, sin_t = cos_ref[...], sin_ref[...]
        qr_ref[...] = (q * cos_t + _swap_head_halves(q) * sin_t).astype(jnp.bfloat16)
        k = k * cos_t + _swap_head_halves(k) * sin_t
    k_ref[...] = k.astype(jnp.bfloat16)
    v_ref[...] = cols(2 * BRANCH_WIDTH).astype(jnp.bfloat16)
    for c0 in range(0, REST_WIDTH, PROJ_COLS):
        rest_ref[:, c0:c0 + PROJ_COLS] = cols(POOL_OFF + c0)
    for c0 in range(0, GATE_WIDTH, PROJ_COLS):
        gate_ref[:, c0:c0 + PROJ_COLS] = jax.nn.sigmoid(cols(GATE_OFF + c0)).astype(jnp.bfloat16)


def input_projection(x, g, shift, scale, w, rope_tables=None):
    b, n, d = x.shape
    tm = min(PROJ_TILE, n)
    rope = rope_tables is not None
    per_batch = shift.shape[0] > 1
    mod_spec = pl.BlockSpec((None, 1, d), (lambda i, j: (i, 0, 0)) if per_batch else (lambda i, j: (0, 0, 0)))
    tok = lambda width: pl.BlockSpec((None, tm, width), lambda i, j: (i, j, 0))
    in_specs = [tok(d), pl.BlockSpec((1, d), lambda i, j: (0, 0)), mod_spec, mod_spec,
                pl.BlockSpec((d, IN_WIDTH), lambda i, j: (0, 0), pipeline_mode=pl.Buffered(1))]
    args = [x, g.reshape(1, d), shift[:, None, :], scale[:, None, :], w]
    n_qkv = 3
    if rope:
        in_specs += [pl.BlockSpec((tm, BRANCH_WIDTH), lambda i, j: (j, 0))] * 2
        args += list(rope_tables)
        n_qkv = 4
    bf = jnp.bfloat16
    out_shape = ([jax.ShapeDtypeStruct((b, n, BRANCH_WIDTH), bf)] * n_qkv
                 + [jax.ShapeDtypeStruct((b, n, REST_WIDTH), jnp.float32), jax.ShapeDtypeStruct((b, n, GATE_WIDTH), bf)])
    out_specs = [tok(BRANCH_WIDTH)] * n_qkv + [tok(REST_WIDTH), tok(GATE_WIDTH)]
    return pl.pallas_call(
        functools.partial(_proj_kernel, rope=rope),
        grid=(b, n // tm),
        in_specs=in_specs,
        out_specs=out_specs,
        out_shape=out_shape,
        compiler_params=pltpu.CompilerParams(
            dimension_semantics=("parallel", "parallel"), vmem_limit_bytes=PROJ_VMEM_LIMIT),
        name="input_projection",
    )(*args)


NA_GROUP = 4
NA_SLAB = NA_GROUP + NA_KH_MAX
NA_HEADS_PER_STEP = 128 // HEAD_DIM


def _natten_bias_patterns(rpb, rows):
    j = jnp.arange(GRID_W)
    col_start = jnp.clip(j - NA_KW // 2, 0, GRID_W - NA_KW)
    col_mask = (j[None, :] >= col_start[:, None]) & (j[None, :] < col_start[:, None] + NA_KW)
    dc = jnp.clip(j[None, :] - j[:, None], -(NA_KW - 1), NA_KW - 1) + NA_KW - 1
    dc_hot = (dc[None, :, :] == jnp.arange(2 * NA_KW - 1)[:, None, None]).astype(jnp.float32)
    by_col = jnp.einsum('hrc,cqk->hrqk', rpb.astype(jnp.float32), dc_hot, precision=lax.Precision.HIGHEST)
    by_col = jnp.where(col_mask[None, None], by_col, NEG_INF)
    masked = jnp.full((NA_HEADS, GRID_W, GRID_W), NEG_INF, jnp.float32)
    n_groups = rows // NA_GROUP
    patterns = []
    for g in (0, 1, n_groups - 1):
        slab0 = min(max(g * NA_GROUP - NA_KH_MAX // 2, 0), rows - NA_SLAB)
        q_rows = []
        for r in range(g * NA_GROUP, (g + 1) * NA_GROUP):
            win0 = min(max(r - NA_KH_MAX // 2, 0), rows - NA_KH_MAX)
            blocks = [by_col[:, kr - r + NA_KH_MAX - 1] if win0 <= kr < win0 + NA_KH_MAX else masked
                      for kr in range(slab0, slab0 + NA_SLAB)]
            q_rows.append(jnp.concatenate(blocks, axis=-1))
        patterns.append(jnp.concatenate(q_rows, axis=1))
    return jnp.stack(patterns, axis=1)


def _softmax_pv(s_lat, s_ctx, v_lat, v_ctx):
    m = jnp.max(s_lat, axis=-1, keepdims=True)
    if s_ctx is not None:
        m = jnp.maximum(m, jnp.max(s_ctx, axis=-1, keepdims=True))
    p_lat = jnp.exp(s_lat - m)
    den = jnp.sum(p_lat, axis=-1, keepdims=True)
    o = jnp.dot(p_lat.astype(jnp.bfloat16), v_lat, preferred_element_type=jnp.float32)
    if s_ctx is not None:
        p_ctx = jnp.exp(s_ctx - m)
        den = den + jnp.sum(p_ctx, axis=-1, keepdims=True)
        o = o + jnp.dot(p_ctx.astype(jnp.bfloat16), v_ctx, preferred_element_type=jnp.float32)
    return o / den


def _head_masks():
    lane = lax.broadcasted_iota(jnp.int32, (1, NA_HEADS_PER_STEP * HEAD_DIM), 1)
    return [(lane >= h * HEAD_DIM) & (lane < (h + 1) * HEAD_DIM) for h in range(NA_HEADS_PER_STEP)]


def _natten_kernel(qr_ref, qp_ref, kr_ref, v_ref, kc_ref, vc_ref, bias_ref, o_ref):
    rows = qr_ref.shape[0] // GRID_W
    n_groups = rows // NA_GROUP
    nq = NA_GROUP * GRID_W
    nk = NA_SLAB * GRID_W
    kc = kc_ref[...]
    vc = vc_ref[...]
    nt = (((1,), (1,)), ((), ()))
    masks = _head_masks()

    def body(g, carry):
        slab0 = jnp.clip(g * NA_GROUP - NA_KH_MAX // 2, 0, rows - NA_SLAB)
        pattern = jnp.where(g == 0, 0, jnp.where(g == n_groups - 1, 2, 1))
        q0 = pl.multiple_of(g * nq, nq)
        k0 = pl.multiple_of(slab0 * GRID_W, GRID_W)
        qr = qr_ref[pl.ds(q0, nq), :]
        qp = qp_ref[pl.ds(q0, nq), :]
        ks = kr_ref[pl.ds(k0, nk), :]
        vs = v_ref[pl.ds(k0, nk), :]
        out = None
        for h, mask in enumerate(masks):
            s_lat = lax.dot_general(jnp.where(mask, qr, 0), ks, nt, preferred_element_type=jnp.float32)
            s_lat = s_lat + bias_ref[h, pattern]
            s_ctx = lax.dot_general(jnp.where(mask, qp, 0), kc, nt, preferred_element_type=jnp.float32)
            o = _softmax_pv(s_lat, s_ctx, vs, vc)
            out = o if out is None else jnp.where(mask, o, out)
        o_ref[pl.ds(q0, nq), :] = out.astype(o_ref.dtype)
        return carry

    lax.fori_loop(0, n_groups, body, 0)


def neighbourhood_attention(q_rot, q_plain, k_rot, v, k_ctx, v_ctx, rpb):
    b, s, width = q_rot.shape
    n_ctx = k_ctx.shape[1]
    lanes = NA_HEADS_PER_STEP * HEAD_DIM
    bias = _natten_bias_patterns(rpb, s // GRID_W)
    lat = pl.BlockSpec((None, s, lanes), lambda i, j: (i, 0, j))
    cx = pl.BlockSpec((None, n_ctx, lanes), lambda i, j: (i, 0, j))
    return pl.pallas_call(
        _natten_kernel,
        grid=(b, width // lanes),
        in_specs=[lat, lat, lat, lat, cx, cx,
                  pl.BlockSpec((NA_HEADS_PER_STEP, 3, NA_GROUP * GRID_W, NA_SLAB * GRID_W), lambda i, j: (j, 0, 0, 0))],
        out_specs=lat,
        out_shape=jax.ShapeDtypeStruct((b, s, width), jnp.float32),
        compiler_params=pltpu.CompilerParams(
            dimension_semantics=("parallel", "parallel"), vmem_limit_bytes=VMEM_LIMIT_BYTES),
        name="natten",
    )(q_rot, q_plain, k_rot, v, k_ctx, v_ctx, bias)


def _ctx_attn_kernel(q_ref, k_ref, v_ref, o_ref):
    nt = (((1,), (1,)), ((), ()))
    q, k, v = q_ref[...], k_ref[...], v_ref[...]
    out = None
    for mask in _head_masks():
        s = lax.dot_general(jnp.where(mask, q, 0), k, nt, preferred_element_type=jnp.float32)
        o = _softmax_pv(s, None, v, None)
        out = o if out is None else jnp.where(mask, o, out)
    o_ref[...] = out.astype(o_ref.dtype)


def context_attention(q, k, v):
    b, n, width = q.shape
    lanes = NA_HEADS_PER_STEP * HEAD_DIM
    spec = pl.BlockSpec((None, n, lanes), lambda i, j: (i, 0, j))
    return pl.pallas_call(
        _ctx_attn_kernel,
        grid=(b, width // lanes),
        in_specs=[spec, spec, spec],
        out_specs=spec,
        out_shape=jax.ShapeDtypeStruct((b, n, width), jnp.float32),
        compiler_params=pltpu.CompilerParams(dimension_semantics=("parallel", "parallel")),
        name="ctx_attention",
    )(q, k, v)


POOL_PAD = 16


def _pool_kernel(u_ref, w_ref, scale_ref, o_ref, pad_ref):
    n = u_ref.shape[0]
    u = u_ref[...]
    pad_ref[pl.ds(0, POOL_PAD), :] = jnp.zeros((POOL_PAD, POOL_GROUP_DIM), jnp.float32)
    pad_ref[pl.ds(POOL_PAD + n, POOL_PAD), :] = jnp.zeros((POOL_PAD, POOL_GROUP_DIM), jnp.float32)
    pad_ref[pl.ds(POOL_PAD, n), :] = u
    t = lax.broadcasted_iota(jnp.int32, (n, 1), 0)
    for g, w in enumerate(POOL_WINDOWS):
        @pl.when(pl.program_id(1) == g)
        def _(w=w):
            first = POOL_PAD - w // 2
            acc = pad_ref[pl.ds(first, n), :]
            for k in range(1, w):
                acc = acc + pad_ref[pl.ds(first + k, n), :]
            cnt = jnp.minimum(t + (w - w // 2), n) - jnp.maximum(t - w // 2, 0)
            y = acc / cnt.astype(jnp.float32) - u
            y = jnp.dot(y.astype(jnp.bfloat16), w_ref[...].astype(jnp.bfloat16), preferred_element_type=jnp.float32)
            o_ref[...] = y * scale_ref[...]


def multiscale_pool(rest, pool_w, pool_scale):
    b, n, _ = rest.shape
    col0 = 0
    return pl.pallas_call(
        _pool_kernel,
        grid=(b, POOL_GROUPS),
        in_specs=[pl.BlockSpec((None, n, POOL_GROUP_DIM), lambda i, g: (i, 0, col0 + g)),
                  pl.BlockSpec((None, POOL_GROUP_DIM, POOL_GROUP_DIM), lambda i, g: (g, 0, 0)),
                  pl.BlockSpec((1, POOL_GROUP_DIM), lambda i, g: (0, g))],
        out_specs=pl.BlockSpec((None, n, POOL_GROUP_DIM), lambda i, g: (i, 0, g)),
        out_shape=jax.ShapeDtypeStruct((b, n, BRANCH_WIDTH), jnp.float32),
        scratch_shapes=[pltpu.VMEM((n + 2 * POOL_PAD, POOL_GROUP_DIM), jnp.float32)],
        compiler_params=pltpu.CompilerParams(
            dimension_semantics=("parallel", "parallel"), vmem_limit_bytes=VMEM_LIMIT_BYTES),
        name="multiscale_pool",
    )(rest, pool_w, pool_scale.reshape(1, BRANCH_WIDTH))


HY_CHUNK = 64
HY_PITCH = 72
HY_LANES = 128
HY_FEAT = 128
HY_VMEM_LIMIT = 60 * 1024 * 1024


def _cis_tables(num, den):
    ang = (-2.0 * math.pi / den) * (num % den).astype(jnp.float32)
    return jnp.cos(ang), jnp.sin(ang)


def _stack_complex(mr, mi):
    return jnp.concatenate([jnp.concatenate([mr, -mi], axis=-1), jnp.concatenate([mi, mr], axis=-1)], axis=-2)


def _hy_dft_tables(n):
    big = 2 * n
    n1_full = big // HY_CHUNK
    k1 = jnp.arange(n1_full, dtype=jnp.int32)
    n1 = jnp.arange(n1_full, dtype=jnp.int32)
    n2 = jnp.arange(HY_CHUNK, dtype=jnp.int32)
    mr, mi = _cis_tables(k1[None, :, None] * (HY_CHUNK * n1[None, None, :] + n2[:, None, None]), big)
    half = n1_full // 2
    fwd_a = _stack_complex(mr[:, :, :half], mi[:, :, :half])
    fwd_a_real = jnp.concatenate([mr, mi], axis=-2)
    mrt = jnp.transpose(mr[:, :, :half], (0, 2, 1))
    mit = -jnp.transpose(mi[:, :, :half], (0, 2, 1))
    inv_a = _stack_complex(mrt, mit)
    er, ei = _cis_tables(n2[:, None] * n2[None, :], HY_CHUNK)
    fwd_b = _stack_complex(er, ei)
    inv_b = _stack_complex(er, -ei)
    bf = jnp.bfloat16
    return fwd_a.astype(bf), fwd_a_real.astype(bf), inv_a.astype(bf), fwd_b.astype(bf), inv_b.astype(bf)


def _hy_positions(n):
    f32 = jnp.float32
    t = jnp.linspace(0.0, 1.0, n, dtype=f32)[:, None]
    bands = (HY_EMB_DIM - 1) // 2
    ang = (2.0 * math.pi / n) * jnp.arange(n, dtype=f32)[:, None]
    f = jnp.linspace(1e-4, bands - 1, bands, dtype=f32)[None, :]
    z = jnp.concatenate([t, jnp.cos(f * ang), -jnp.sin(f * ang)], axis=-1)
    z = jnp.pad(z, ((0, 0), (0, HY_FEAT - HY_EMB_DIM)))
    zb = jnp.concatenate([z[:1], z[:0:-1]], axis=0)
    return z, zb


def _filter_mlp(z, w1, b1, fr, w2, b2):
    hp = lax.Precision.HIGHEST
    h = jnp.sin(fr * (jnp.dot(z, w1, precision=hp, preferred_element_type=jnp.float32) + b1))
    return jnp.sin(fr * (jnp.dot(h, w2, precision=hp, preferred_element_type=jnp.float32) + b2))


def _filter_taps(zf_ref, zb_ref, w1_ref, b1_ref, fr_ref, w2_ref, b2_ref, w3f_ref, w3b_ref, delta_ref, hid_ref):
    n = zf_ref.shape[0]
    hp = lax.Precision.HIGHEST

    @pl.when((pl.program_id(1) == 0) & (pl.program_id(2) == 0))
    def _():
        args = (w1_ref[...], b1_ref[...], fr_ref[...], w2_ref[...], b2_ref[...])
        hid_ref[0] = _filter_mlp(zf_ref[...], *args)
        hid_ref[1] = _filter_mlp(zb_ref[...], *args)

    delta = delta_ref[...]
    hf = jnp.dot(hid_ref[0], w3f_ref[...], precision=hp, preferred_element_type=jnp.float32)
    hf = hf * jnp.exp(-zf_ref[:, 0:1] * delta)
    hb = jnp.dot(hid_ref[1], w3b_ref[...], precision=hp, preferred_element_type=jnp.float32)
    hb = hb * jnp.exp(-zb_ref[:, 0:1] * delta)
    row = lax.broadcasted_iota(jnp.int32, (n, 1), 0)
    hb = jnp.where(row > 0, hb, 0.0)
    norm = jnp.sum(jnp.abs(hf), axis=0, keepdims=True) + jnp.sum(jnp.abs(hb), axis=0, keepdims=True)
    return hf / norm, hb / norm


def _filter_spec_kernel(zf_ref, zb_ref, w1_ref, b1_ref, fr_ref, w2_ref, b2_ref, w3f_ref, w3b_ref, delta_ref,
                        fa_ref, fb_ref, hr_ref, hi_ref, hid_ref, tbuf, sbuf):
    n = zf_ref.shape[0]
    chunks = n // HY_CHUNK
    n1_full = 2 * chunks
    hf, hb = _filter_taps(zf_ref, zb_ref, w1_ref, b1_ref, fr_ref, w2_ref, b2_ref, w3f_ref, w3b_ref, delta_ref, hid_ref)
    for c in range(chunks):
        tbuf[pl.ds(c * HY_PITCH, HY_CHUNK), :] = hf[c * HY_CHUNK:(c + 1) * HY_CHUNK]
        tbuf[pl.ds((chunks + c) * HY_PITCH, HY_CHUNK), :] = hb[c * HY_CHUNK:(c + 1) * HY_CHUNK]

    def stage_a(n2, carry):
        rows = tbuf[pl.ds(n2, n1_full, stride=HY_PITCH), :].astype(jnp.bfloat16)
        a = jnp.dot(fa_ref[n2], rows, preferred_element_type=jnp.float32)
        sbuf[0, pl.ds(n2, n1_full, stride=HY_PITCH), :] = a[:n1_full]
        sbuf[1, pl.ds(n2, n1_full, stride=HY_PITCH), :] = a[n1_full:]
        return carry

    lax.fori_loop(0, HY_CHUNK, stage_a, 0, unroll=4)
    inv_len = 1.0 / (2 * n)

    def stage_b(k1, carry):
        r0 = pl.multiple_of(k1 * HY_PITCH, 8)
        x = jnp.concatenate([sbuf[0, pl.ds(r0, HY_CHUNK), :], sbuf[1, pl.ds(r0, HY_CHUNK), :]], axis=0)
        y = jnp.dot(fb_ref[...], x.astype(jnp.bfloat16), preferred_element_type=jnp.float32) * inv_len
        o0 = pl.multiple_of(k1 * HY_CHUNK, HY_CHUNK)
        hr_ref[pl.ds(o0, HY_CHUNK), :] = y[:HY_CHUNK]
        hi_ref[pl.ds(o0, HY_CHUNK), :] = y[HY_CHUNK:]
        return carry

    lax.fori_loop(0, n1_full, stage_b, 0, unroll=8)


def hyena_filter_spectra(n, hy_w1, hy_b1, hy_freq, hy_w2, hy_b2, hy_w3, tables):
    depth = hy_w1.shape[0]
    zf, zb = _hy_positions(n)
    tiles = BRANCH_WIDTH // HY_LANES
    w1 = jnp.pad(hy_w1, ((0, 0), (0, HY_FEAT - HY_EMB_DIM), (0, 0)))
    deltas = jnp.abs(jnp.linspace(HY_MIN_DECAY, HY_MAX_DECAY, BRANCH_WIDTH, dtype=jnp.float32)).reshape(1, BRANCH_WIDTH)
    fwd_a_real, fwd_b = tables[1], tables[3]
    n1_full = 2 * n // HY_CHUNK
    one = pl.Buffered(1)
    full = lambda shape: pl.BlockSpec(shape, lambda l, o, j: (0,) * len(shape), pipeline_mode=one)
    per_layer = lambda shape: pl.BlockSpec((None,) + shape, lambda l, o, j: (l,) + (0,) * len(shape))
    out_spec = pl.BlockSpec((None, None, 2 * n, HY_LANES), lambda l, o, j: (l, o, 0, j), pipeline_mode=one)
    out_sds = jax.ShapeDtypeStruct((depth, HY_ORDER, 2 * n, BRANCH_WIDTH), jnp.float32)
    return pl.pallas_call(
        _filter_spec_kernel,
        grid=(depth, HY_ORDER, tiles),
        in_specs=[full((n, HY_FEAT)), full((n, HY_FEAT)),
                  per_layer((HY_FEAT, HY_FILTER_HIDDEN)), per_layer((1, HY_FILTER_HIDDEN)),
                  per_layer((1, HY_FILTER_HIDDEN)),
                  per_layer((HY_FILTER_HIDDEN, HY_FILTER_HIDDEN)), per_layer((1, HY_FILTER_HIDDEN)),
                  pl.BlockSpec((None, HY_FILTER_HIDDEN, HY_LANES), lambda l, o, j: (l, 0, o * 2 * tiles + j)),
                  pl.BlockSpec((None, HY_FILTER_HIDDEN, HY_LANES), lambda l, o, j: (l, 0, o * 2 * tiles + tiles + j)),
                  pl.BlockSpec((1, HY_LANES), lambda l, o, j: (0, j)),
                  full((HY_CHUNK, 2 * n1_full, n1_full)), full((2 * HY_CHUNK, 2 * HY_CHUNK))],
        out_specs=[out_spec, out_spec],
        out_shape=[out_sds, out_sds],
        scratch_shapes=[pltpu.VMEM((2, n, HY_FILTER_HIDDEN), jnp.float32),
                        pltpu.VMEM((n1_full * HY_PITCH, HY_LANES), jnp.float32),
                        pltpu.VMEM((2, n1_full * HY_PITCH, HY_LANES), jnp.float32)],
        compiler_params=pltpu.CompilerParams(
            dimension_semantics=("arbitrary", "arbitrary", "arbitrary"), vmem_limit_bytes=HY_VMEM_LIMIT),
        name="hyena_filter_spectra",
    )(zf, zb, w1, hy_b1[:, None, :], hy_freq[:, None, :], hy_w2, hy_b2[:, None, :], hy_w3, hy_w3, deltas,
      fwd_a_real, fwd_b)


def _short_conv(u_ref, b, w_ref, bias_ref, pad_ref):
    n = u_ref.shape[1]
    x = u_ref[b]
    pad_ref[pl.ds(0, 8), :] = jnp.zeros((8, HY_LANES), jnp.float32)
    pad_ref[pl.ds(8 + n, 8), :] = jnp.zeros((8, HY_LANES), jnp.float32)
    pad_ref[pl.ds(8, n), :] = x
    return (pad_ref[pl.ds(7, n), :] * w_ref[0:1, :] + x * w_ref[1:2, :] + pad_ref[pl.ds(9, n), :] * w_ref[2:3, :]
            + bias_ref[...])


def _hyena_conv_kernel(z_ref, g_ref, wz_ref, bz_ref, wg_ref, bg_ref, skip_ref, hr_ref, hi_ref,
                       fa_ref, ia_ref, fb_ref, ib_ref, o_ref, pad_ref, zbuf, sbuf, *, conv_z):
    n = z_ref.shape[1]
    chunks = n // HY_CHUNK
    n1_full = 2 * chunks
    for b in range(2):
        z = _short_conv(z_ref, b, wz_ref, bz_ref, pad_ref) if conv_z else z_ref[b]
        for c in range(chunks):
            zbuf[b, pl.ds(c * HY_PITCH, HY_CHUNK), :] = z[c * HY_CHUNK:(c + 1) * HY_CHUNK]

    def stage_a(n2, carry):
        x = jnp.concatenate([zbuf[0, pl.ds(n2, chunks, stride=HY_PITCH), :],
                             zbuf[1, pl.ds(n2, chunks, stride=HY_PITCH), :]], axis=0)
        a = jnp.dot(fa_ref[n2], x.astype(jnp.bfloat16), preferred_element_type=jnp.float32)
        sbuf[0, pl.ds(n2, n1_full, stride=HY_PITCH), :] = a[:n1_full]
        sbuf[1, pl.ds(n2, n1_full, stride=HY_PITCH), :] = a[n1_full:]
        return carry

    lax.fori_loop(0, HY_CHUNK, stage_a, 0, unroll=4)

    def stage_b(k1, carry):
        r0 = pl.multiple_of(k1 * HY_PITCH, 8)
        x = jnp.concatenate([sbuf[0, pl.ds(r0, HY_CHUNK), :], sbuf[1, pl.ds(r0, HY_CHUNK), :]], axis=0)
        s = jnp.dot(fb_ref[...], x.astype(jnp.bfloat16), preferred_element_type=jnp.float32)
        h0 = pl.multiple_of(k1 * HY_CHUNK, HY_CHUNK)
        hr = hr_ref[pl.ds(h0, HY_CHUNK), :]
        hi = hi_ref[pl.ds(h0, HY_CHUNK), :]
        sr, si = s[:HY_CHUNK], s[HY_CHUNK:]
        y = jnp.concatenate([sr * hr - si * hi, sr * hi + si * hr], axis=0)
        c = jnp.dot(ib_ref[...], y.astype(jnp.bfloat16), preferred_element_type=jnp.float32)
        sbuf[0, pl.ds(r0, HY_CHUNK), :] = c[:HY_CHUNK]
        sbuf[1, pl.ds(r0, HY_CHUNK), :] = c[HY_CHUNK:]
        return carry

    lax.fori_loop(0, n1_full, stage_b, 0, unroll=8)
    skip = skip_ref[...]

    def stage_c(n2, carry):
        x = jnp.concatenate([sbuf[0, pl.ds(n2, n1_full, stride=HY_PITCH), :],
                             sbuf[1, pl.ds(n2, n1_full, stride=HY_PITCH), :]], axis=0)
        y = jnp.dot(ia_ref[n2], x.astype(jnp.bfloat16), preferred_element_type=jnp.float32)
        for b in range(2):
            zb = zbuf[b, pl.ds(n2, chunks, stride=HY_PITCH), :]
            zbuf[b, pl.ds(n2, chunks, stride=HY_PITCH), :] = y[b * chunks:(b + 1) * chunks] + zb * skip
        return carry

    lax.fori_loop(0, HY_CHUNK, stage_c, 0, unroll=4)
    for b in range(2):
        g = _short_conv(g_ref, b, wg_ref, bg_ref, pad_ref)
        for c in range(chunks):
            o_ref[b, pl.ds(c * HY_CHUNK, HY_CHUNK), :] = (
                g[c * HY_CHUNK:(c + 1) * HY_CHUNK] * zbuf[b, pl.ds(c * HY_PITCH, HY_CHUNK), :])


def hyena_long_conv(z, z_col0, g, g_col0, conv_z, short_w, short_b, w_off_z, w_off_g, skip, hr, hi, layer, order,
                    tables):
    b, n, _ = g.shape
    tiles = BRANCH_WIDTH // HY_LANES
    fwd_a, _, inv_a, fwd_b, inv_b = tables
    n1_full = 2 * n // HY_CHUNK
    one = pl.Buffered(1)
    zc, gc = z_col0 // HY_LANES, g_col0 // HY_LANES
    wz, wg = w_off_z // HY_LANES, w_off_g // HY_LANES
    const = lambda shape: pl.BlockSpec(shape, lambda j, p: (0,) * len(shape), pipeline_mode=one)
    return pl.pallas_call(
        functools.partial(_hyena_conv_kernel, conv_z=conv_z),
        grid=(tiles, b // 2),
        in_specs=[pl.BlockSpec((2, n, HY_LANES), lambda j, p: (p, 0, zc + j)),
                  pl.BlockSpec((2, n, HY_LANES), lambda j, p: (p, 0, gc + j)),
                  pl.BlockSpec((HY_SHORT, HY_LANES), lambda j, p: (0, wz + j)),
                  pl.BlockSpec((1, HY_LANES), lambda j, p: (0, wz + j)),
                  pl.BlockSpec((HY_SHORT, HY_LANES), lambda j, p: (0, wg + j)),
                  pl.BlockSpec((1, HY_LANES), lambda j, p: (0, wg + j)),
                  pl.BlockSpec((1, HY_LANES), lambda j, p: (0, j)),
                  pl.BlockSpec((None, None, 2 * n, HY_LANES), lambda j, p: (layer, order, 0, j), pipeline_mode=one),
                  pl.BlockSpec((None, None, 2 * n, HY_LANES), lambda j, p: (layer, order, 0, j), pipeline_mode=one),
                  const((HY_CHUNK, 2 * n1_full, n1_full)), const((HY_CHUNK, n1_full, 2 * n1_full)),
                  const((2 * HY_CHUNK, 2 * HY_CHUNK)), const((2 * HY_CHUNK, 2 * HY_CHUNK))],
        out_specs=pl.BlockSpec((2, n, HY_LANES), lambda j, p: (p, 0, j), pipeline_mode=one),
        out_shape=jax.ShapeDtypeStruct((b, n, BRANCH_WIDTH), jnp.float32),
        scratch_shapes=[pltpu.VMEM((n + 16, HY_LANES), jnp.float32),
                        pltpu.VMEM((2, (n // HY_CHUNK) * HY_PITCH, HY_LANES), jnp.float32),
                        pltpu.VMEM((2, n1_full * HY_PITCH, HY_LANES), jnp.float32)],
        compiler_params=pltpu.CompilerParams(
            dimension_semantics=("parallel", "parallel"), vmem_limit_bytes=HY_VMEM_LIMIT),
        name="hyena_long_conv",
    )(z, g, short_w, short_b.reshape(1, -1), short_w, short_b.reshape(1, -1), skip.reshape(1, -1), hr, hi,
      fwd_a, inv_a, fwd_b, inv_b)


def hyena_mixer(rest, short_w, short_b, skip, hr, hi, layer, tables):
    off = HY_OFF - POOL_OFF
    z1 = hyena_long_conv(rest, off, rest, off + BRANCH_WIDTH, True, short_w, short_b, 0, BRANCH_WIDTH,
                         skip[0], hr, hi, layer, 0, tables)
    return hyena_long_conv(z1, 0, rest, off + 2 * BRANCH_WIDTH, False, short_w, short_b, 0, 2 * BRANCH_WIDTH,
                           skip[1], hr, hi, layer, 1, tables)


MERGE_TILE = 512


def _merge_kernel(ya_ref, yp_ref, yh_ref, gate_ref, x_ref, res_gate_ref, wb_ref, wo_ref, g2_ref, shift_ref, scale_ref,
                  wr_hi_ref, wr_lo_ref, xo_ref, h_ref, logit_ref):
    d = x_ref.shape[-1]
    merged = None
    for i, y_ref in enumerate((ya_ref, yp_ref, yh_ref)):
        br = jnp.dot(y_ref[...].astype(jnp.bfloat16), wb_ref[i], preferred_element_type=jnp.float32)
        term = gate_ref[:, i * d:(i + 1) * d].astype(jnp.float32) * br
        merged = term if merged is None else merged + term
    out = jnp.dot(merged.astype(jnp.bfloat16), wo_ref[...], preferred_element_type=jnp.float32)
    x = x_ref[...] + res_gate_ref[...] * out
    xo_ref[...] = x
    y = x * lax.rsqrt(jnp.mean(x * x, axis=-1, keepdims=True) + RMS_EPS) * g2_ref[...]
    h = y * (1.0 + scale_ref[...]) + shift_ref[...]
    _rows_to_tiles(h_ref, (), h)
    h_hi = h.astype(jnp.bfloat16)
    h_lo = (h - h_hi.astype(jnp.float32)).astype(jnp.bfloat16)
    wr_hi = wr_hi_ref[...]
    logit_ref[...] = (jnp.dot(h_hi, wr_hi, preferred_element_type=jnp.float32)
                      + (jnp.dot(h_hi, wr_lo_ref[...], preferred_element_type=jnp.float32)
                         + jnp.dot(h_lo, wr_hi, preferred_element_type=jnp.float32)))


def merge_and_norm(y_attn, y_pool, y_hy, gates, x, mod, w_branch, w_out, g2, w_router):
    b, n, d = x.shape
    tm = min(MERGE_TILE, n)
    per_batch = mod.shape[0] > 1
    mod_spec = lambda k: pl.BlockSpec((None, None, 1, d),
                                      (lambda i, j: (i, k, 0, 0)) if per_batch else (lambda i, j: (0, k, 0, 0)))
    tok = lambda width: pl.BlockSpec((None, tm, width), lambda i, j: (i, j, 0))
    const = lambda shape: pl.BlockSpec(shape, lambda i, j: (0,) * len(shape))
    call = pl.pallas_call(
        _merge_kernel,
        grid=(b, n // tm),
        in_specs=[tok(BRANCH_WIDTH), tok(BRANCH_WIDTH), tok(BRANCH_WIDTH), tok(GATE_WIDTH), tok(d), mod_spec(2),
                  const((N_BRANCH, BRANCH_WIDTH, d)), const((d, d)), const((1, d)), mod_spec(3), mod_spec(4),
                  const((d, N_EXPERTS)), const((d, N_EXPERTS))],
        out_specs=[tok(d), pl.BlockSpec((None, tm * ROW_TILE, d // ROW_TILE), lambda i, j: (i, j, 0)), tok(N_EXPERTS)],
        out_shape=[jax.ShapeDtypeStruct((b, n, d), jnp.float32),
                   jax.ShapeDtypeStruct((b, n * ROW_TILE, d // ROW_TILE), jnp.float32),
                   jax.ShapeDtypeStruct((b, n, N_EXPERTS), jnp.float32)],
        compiler_params=pltpu.CompilerParams(
            dimension_semantics=("parallel", "parallel"), vmem_limit_bytes=VMEM_LIMIT_BYTES),
        name="merge_and_norm",
    )
    mod4 = mod[:, :, None, :]
    wr_hi = w_router.astype(jnp.bfloat16)
    wr_lo = (w_router - wr_hi.astype(jnp.float32)).astype(jnp.bfloat16)
    return call(y_attn, y_pool, y_hy, gates, x, mod4, w_branch, w_out, g2.reshape(1, d), mod4, mod4, wr_hi, wr_lo)


def _combine_kernel(y0_ref, y1_ref, w_ref, x_ref, gate_ref, g_ref, o_ref, *, final_norm):
    w = w_ref[...]
    m = x_ref.shape[0]
    f = _tiles_to_rows(y0_ref, (), m) * w[:, 0:1] + _tiles_to_rows(y1_ref, (), m) * w[:, 1:2]
    x = x_ref[...] + gate_ref[...] * f
    if final_norm:
        x = x * lax.rsqrt(jnp.mean(x * x, axis=-1, keepdims=True) + RMS_EPS) * g_ref[...]
    o_ref[...] = x


def moe_combine(y, wsel, row0, x, mod, final_g=None):
    b, n, d = x.shape
    tm = min(MERGE_TILE, n)
    tiles = n // tm
    blk0 = row0 // tm
    second = wsel.shape[0] // tm
    per_batch = mod.shape[0] > 1
    tok = lambda width: pl.BlockSpec((None, tm, width), lambda i, j: (i, j, 0))
    flat = lambda width, off=0: pl.BlockSpec((tm, width), lambda i, j: (off + blk0 + i * tiles + j, 0))
    tiled = lambda off: pl.BlockSpec((tm * ROW_TILE, d // ROW_TILE), lambda i, j: (off + blk0 + i * tiles + j, 0))
    k = N_MOD - 1
    gate_spec = pl.BlockSpec((None, None, 1, d), (lambda i, j: (i, k, 0, 0)) if per_batch else (lambda i, j: (0, k, 0, 0)))
    g = jnp.ones((1, d), jnp.float32) if final_g is None else final_g.reshape(1, d)
    return pl.pallas_call(
        functools.partial(_combine_kernel, final_norm=final_g is not None),
        grid=(b, n // tm),
        in_specs=[tiled(0), tiled(second), flat(TOP_K), tok(d), gate_spec, pl.BlockSpec((1, d), lambda i, j: (0, 0))],
        out_specs=tok(d),
        out_shape=jax.ShapeDtypeStruct((b, n, d), jnp.float32),
        compiler_params=pltpu.CompilerParams(
            dimension_semantics=("parallel", "parallel"), vmem_limit_bytes=VMEM_LIMIT_BYTES),
        name="moe_combine",
    )(y, y, wsel, x, mod[:, :, None, :], g)


def latent_mixers(x, g1, mod, w_in, rope_tables, k_ctx, v_ctx, rpb, pool_w, pool_scale, hy, w_branch, w_out, g2, w_router):
    q_rot, q_plain, k, v, rest, gates = input_projection(x, g1, mod[:, 0], mod[:, 1], w_in, rope_tables)
    y_attn = neighbourhood_attention(q_rot, q_plain, k, v, k_ctx, v_ctx, rpb)
    y_pool = multiscale_pool(rest, pool_w, pool_scale)
    y_hy = hyena_mixer(rest, *hy)
    return merge_and_norm(y_attn, y_pool, y_hy, gates, x, mod, w_branch, w_out, g2, w_router)


def kernel(x, c, ctx, c_ctx, norm1_g, norm2_g, w_mod, b_mod, w_in, rpb, pool_w, pool_scale, hy_short_w, hy_short_b, hy_w1, hy_b1, hy_freq, hy_w2, hy_b2, hy_w3, hy_skip, w_branch, w_out, w_router, router_bias, w_gate_e, w_up_e, w_down_e, final_g):
    b, s, d = x.shape
    n_ctx = ctx.shape[1]
    bf = jnp.bfloat16
    rope_tables = axial_rope_tables(s)
    tables_l = _hy_dft_tables(s)
    tables_c = _hy_dft_tables(n_ctx)
    spec_l = hyena_filter_spectra(s, hy_w1, hy_b1, hy_freq, hy_w2, hy_b2, hy_w3, tables_l)
    spec_c = hyena_filter_spectra(n_ctx, hy_w1[:DEPTH - 1], hy_b1[:DEPTH - 1], hy_freq[:DEPTH - 1], hy_w2[:DEPTH - 1],
                                  hy_b2[:DEPTH - 1], hy_w3[:DEPTH - 1], tables_c)
    cond = jnp.concatenate([c, c_ctx[None, :], jnp.zeros((8 - (b + 1) % 8, d), c.dtype)], axis=0)
    cond = jax.nn.silu(cond)
    xl, xc = x, ctx
    for l in range(DEPTH):
        need_ctx = l < DEPTH - 1
        mod = (matmul(cond, w_mod[l]) + b_mod[l]).reshape(-1, N_MOD, d)
        ml, mc = mod[:b], mod[b:b + 1]
        w_in_l, w_branch_l, w_out_l = w_in[l].astype(bf), w_branch[l].astype(bf), w_out[l].astype(bf)
        common = (pool_w[l], pool_scale[l])
        hy_c = (hy_short_w[l], hy_short_b[l], hy_skip[l])
        qc, kc, vc, rest_c, gates_c = input_projection(xc, norm1_g[l], mc[:, 0], mc[:, 1], w_in_l)
        xl, hl2, logit_l = latent_mixers(xl, norm1_g[l], ml, w_in_l, rope_tables, kc, vc, rpb[l], *common,
                                         hy_c + (spec_l[0], spec_l[1], l, tables_l), w_branch_l, w_out_l,
                                         norm2_g[l], w_router)
        tokens, logits = hl2.reshape(-1, d // ROW_TILE), logit_l.reshape(b * s, N_EXPERTS)
        if need_ctx:
            ya_c = context_attention(qc, kc, vc)
            yp_c = multiscale_pool(rest_c, *common)
            yh_c = hyena_mixer(rest_c, *hy_c, spec_c[0], spec_c[1], l, tables_c)
            xc, hc2, logit_c = merge_and_norm(ya_c, yp_c, yh_c, gates_c, xc, mc, w_branch_l, w_out_l, norm2_g[l], w_router)
            tokens = jnp.concatenate([tokens, hc2.reshape(-1, d // ROW_TILE)], axis=0)
            logits = jnp.concatenate([logits, logit_c.reshape(b * n_ctx, N_EXPERTS)], axis=0)
        y, wsel = grouped_moe(tokens, logits, router_bias, l, w_gate_e, w_up_e, w_down_e)
        xl = moe_combine(y, wsel, 0, xl, ml, final_g if l == DEPTH - 1 else None)
        if need_ctx:
            xc = moe_combine(y, wsel, b * s, xc, mc)
    return xl
```
